```python
import math
import jax, jax.numpy as jnp
from jax import lax
import numpy as np

D_MODEL = 1024
BATCH = 8
SEQ = 4096
DEPTH = 2

HEAD_DIM = 64
ROPE_THETA = 500000.0
ROPE_DIM = HEAD_DIM // 4
NORM_EPS = 1e-6
Q_BLOCK = 128
NEG = -1e30
D_FF = 4 * D_MODEL

A_HEADS = 4
A_NOPE = HEAD_DIM - ROPE_DIM
A_VDIM = HEAD_DIM
A_KV_RANK = 128
IDX_HEADS = 4
IDX_DIM = 32
IDX_ROPE = IDX_DIM // 4
DSA_TOPK = 256

B_WIDTH = 256
S5_GROUP = 16
S5_GROUPS = B_WIDTH // S5_GROUP
S5_STATE = 64
S5_DT_MIN = 1e-3
S5_DT_MAX = 1e-1

C_HEADS = 8
C_KV_HEADS = 2
C_GROUP = C_HEADS // C_KV_HEADS
CMP_BLOCK = 32
CMP_STRIDE = 16
SEL_BLOCK = 64
SEL_TOPN = 16
SEL_FORCE = 1e9
WINDOW = 512

D_MIX = A_HEADS * A_VDIM + B_WIDTH + C_HEADS * HEAD_DIM
IN_SIZES = (
    A_HEADS * HEAD_DIM,
    A_KV_RANK,
    ROPE_DIM,
    IDX_HEADS * IDX_DIM,
    IDX_DIM,
    IDX_HEADS,
    B_WIDTH,
    C_HEADS * HEAD_DIM,
    6 * C_KV_HEADS * HEAD_DIM,
    3 * C_HEADS,
)
N_IN = sum(IN_SIZES)

kernel_name = "hybrid_dsa_s5_nsa_parallel_heads"


def rms_norm(x, g):
    xf = x.astype(jnp.float32)
    y = xf * lax.rsqrt(jnp.mean(xf * xf, axis=-1, keepdims=True) + NORM_EPS)
    return (y * g.astype(jnp.float32)).astype(x.dtype)


def rope_tables(positions, rot_dim):
    inv = ROPE_THETA ** (-jnp.arange(0, rot_dim, 2, dtype=jnp.float32) / rot_dim)
    ang = positions.astype(jnp.float32)[..., None] * inv
    return jnp.cos(ang), jnp.sin(ang)


def partial_rope(x, cos, sin):
    half = cos.shape[-1]
    shape = cos.shape[:2] + (1,) * (x.ndim - 3) + (half,)
    c = cos.reshape(shape).astype(x.dtype)
    s = sin.reshape(shape).astype(x.dtype)
    x1, x2, rest = x[..., :half], x[..., half:2 * half], x[..., 2 * half:]
    return jnp.concatenate([x1 * c - x2 * s, x2 * c + x1 * s, rest], axis=-1)


def masked_softmax(s, mask):
    s = jnp.where(mask, s.astype(jnp.float32), NEG)
    p = jax.nn.softmax(s, axis=-1)
    return jnp.where(mask, p, 0.0)


def to_blocks(a):
    b, l = a.shape[:2]
    return jnp.moveaxis(a.reshape((b, l // Q_BLOCK, Q_BLOCK) + a.shape[2:]), 1, 0)


def from_blocks(a):
    nb, b, q = a.shape[:3]
    return jnp.moveaxis(a, 0, 1).reshape((b, nb * q) + a.shape[3:])


def sel_overlap(n_cmp, n_blk):
    cs = np.arange(n_cmp)[:, None] * CMP_STRIDE
    ss = np.arange(n_blk)[None, :] * SEL_BLOCK
    ov = np.minimum(cs + CMP_BLOCK, ss + SEL_BLOCK) - np.maximum(cs, ss)
    return np.clip(ov, 0, None).astype(np.float32) / CMP_BLOCK


def dsa_mixer(q, ckv, k_rope, iq, ik, iw, cos, sin, cos_i, sin_i, kv_g, w_uk, w_uv):
    bsz, seq = q.shape[:2]
    k_top = min(DSA_TOPK, seq // 4)
    q = partial_rope(q.reshape(bsz, seq, A_HEADS, HEAD_DIM), cos, sin)
    q_rope, q_nope = q[..., :ROPE_DIM], q[..., ROPE_DIM:]
    q_lat = jnp.einsum('bthn,rhn->bthr', q_nope, w_uk)
    c = rms_norm(ckv, kv_g)
    k_rope = partial_rope(k_rope, cos, sin)
    iq = partial_rope(iq.reshape(bsz, seq, IDX_HEADS, IDX_DIM), cos_i, sin_i)
    ik = partial_rope(ik, cos_i, sin_i)
    iw = iw.astype(jnp.float32) * IDX_HEADS ** -0.5
    key_pos = jnp.arange(seq)
    take = jax.vmap(lambda a, i: a[i])

    def block(args):
        ql, qr, iqb, iwb, t = args
        rel = jax.nn.relu(jnp.einsum('bqhd,bsd->bqhs', iqb, ik).astype(jnp.float32) * IDX_DIM ** -0.5)
        score = jnp.einsum('bqhs,bqh->bqs', rel, iwb)
        score = jnp.where(key_pos[None, None, :] <= t[None, :, None], score, -jnp.inf)
        _, sel = lax.top_k(score, k_top)
        c_sel = take(c, sel)
        kr_sel = take(k_rope, sel)
        s = (jnp.einsum('bqhr,bqkr->bqhk', ql, c_sel)
             + jnp.einsum('bqhd,bqkd->bqhk', qr, kr_sel)) * HEAD_DIM ** -0.5
        p = masked_softmax(s, (sel <= t[None, :, None])[:, :, None, :])
        o_lat = jnp.einsum('bqhk,bqkr->bqhr', p.astype(c.dtype), c_sel)
        return jnp.einsum('bqhr,rhv->bqhv', o_lat, w_uv)

    t_blocks = key_pos.reshape(-1, Q_BLOCK)
    out = lax.map(block, (to_blocks(q_lat), to_blocks(q_rope), to_blocks(iq), to_blocks(iw), t_blocks))
    return from_blocks(out).reshape(bsz, seq, A_HEADS * A_VDIM)


def _complex_linear_combine(e1, e2):
    a1r, a1i, b1r, b1i = e1
    a2r, a2i, b2r, b2i = e2
    return (a2r * a1r - a2i * a1i, a2r * a1i + a2i * a1r,
            a2r * b1r - a2i * b1i + b2r, a2r * b1i + a2i * b1r + b2i)


def s5_mixer(u, lam_re, lam_im, log_step, b_re, b_im, c_re, c_im, d, glu_w, glu_b):
    f32 = jnp.float32
    bsz, seq, _ = u.shape
    ug = u.astype(f32).reshape(bsz, seq, S5_GROUPS, S5_GROUP)
    lr, li = lam_re.astype(f32), lam_im.astype(f32)
    step = jnp.exp(log_step.astype(f32))[:, None]
    mag = jnp.exp(lr * step)
    ar, ai = mag * jnp.cos(li * step), mag * jnp.sin(li * step)
    den = lr * lr + li * li
    zr = ((ar - 1.0) * lr + ai * li) / den
    zi = (ai * lr - (ar - 1.0) * li) / den
    br, bi = b_re.astype(f32), b_im.astype(f32)
    bbr = zr[..., None] * br - zi[..., None] * bi
    bbi = zr[..., None] * bi + zi[..., None] * br
    xr = jnp.einsum('blgh,gph->lbgp', ug, bbr)
    xi = jnp.einsum('blgh,gph->lbgp', ug, bbi)
    a_shape = (seq, 1) + ar.shape
    _, _, hr, hi = lax.associative_scan(
        _complex_linear_combine,
        (jnp.broadcast_to(ar, a_shape), jnp.broadcast_to(ai, a_shape), xr, xi), axis=0)
    y = (jnp.einsum('lbgp,ghp->blgh', hr, c_re.astype(f32))
         - jnp.einsum('lbgp,ghp->blgh', hi, c_im.astype(f32))
         + d.astype(f32) * ug)
    y = jax.nn.gelu(y.reshape(bsz, seq, B_WIDTH))
    y = y * jax.nn.sigmoid(y @ glu_w.astype(f32) + glu_b.astype(f32))
    return y.astype(u.dtype)


def nsa_mixer(q, kv, gate, cos, sin, pos_k, pos_v, k_w1, k_w2, v_w1, v_w2):
    bsz, seq = q.shape[:2]
    G, HG, DH = C_KV_HEADS, C_GROUP, HEAD_DIM
    q = q.reshape(bsz, seq, G, HG, DH)
    q_rot = partial_rope(q, cos, sin)
    kv = kv.reshape(bsz, seq, 6, G, DH)
    k_c, v_c, k_s, v_s, k_w, v_w = (kv[:, :, i] for i in range(6))
    k_s = partial_rope(k_s, cos, sin)
    k_w = partial_rope(k_w, cos, sin)
    gate = jax.nn.sigmoid(gate.astype(jnp.float32)).reshape(bsz, seq, G, HG, 3)

    n_cmp = (seq - CMP_BLOCK) // CMP_STRIDE + 1
    n_blk = seq // SEL_BLOCK
    n_top = min(SEL_TOPN, n_blk)
    cmp_tok = np.arange(n_cmp)[:, None] * CMP_STRIDE + np.arange(CMP_BLOCK)[None, :]
    cmp_end = jnp.asarray(cmp_tok[:, -1])
    overlap = jnp.asarray(sel_overlap(n_cmp, n_blk))

    def compress(a, pos, w1, w2):
        blk = a[:, cmp_tok] + pos[:, None, :]
        blk = jnp.moveaxis(blk, 3, 2).reshape(bsz, n_cmp, G, CMP_BLOCK * DH)
        return jax.nn.gelu(blk @ w1) @ w2

    k_cmp = compress(k_c, pos_k, k_w1, k_w2)
    v_cmp = compress(v_c, pos_v, v_w1, v_w2)
    pad = ((0, 0), (WINDOW, 0), (0, 0), (0, 0))
    k_wp, v_wp = jnp.pad(k_w, pad), jnp.pad(v_w, pad)
    blk_id = jnp.arange(n_blk)
    tok_in_blk = jnp.arange(SEL_BLOCK)
    slab = jnp.arange(WINDOW + Q_BLOCK)
    b_idx = jnp.arange(bsz)[:, None, None, None]
    g_idx = jnp.arange(G)[None, None, :, None]
    scale = DH ** -0.5

    def block(args):
        qb, qrb, gb, t = args
        s_c = jnp.einsum('bqghd,bngd->bqghn', qb, k_cmp) * scale
        p_c = masked_softmax(s_c, (cmp_end[None, :] <= t[:, None])[None, :, None, None, :])
        o_c = jnp.einsum('bqghn,bngd->bqghd', p_c.astype(v_cmp.dtype), v_cmp)
        imp = jnp.einsum('bqghn,nm->bqgm', p_c, overlap)
        cur = (t // SEL_BLOCK)[:, None]
        forced = (blk_id[None] == 0) | (blk_id[None] == cur) | (blk_id[None] == cur - 1)
        imp = jnp.where(forced[None, :, None, :], SEL_FORCE, imp)
        imp = jnp.where((blk_id[None] * SEL_BLOCK <= t[:, None])[None, :, None, :], imp, -jnp.inf)
        _, sel = lax.top_k(imp, n_top)
        tok = (sel[..., None] * SEL_BLOCK + tok_in_blk).reshape(bsz, Q_BLOCK, G, n_top * SEL_BLOCK)
        ks = k_s[b_idx, tok, g_idx]
        vs = v_s[b_idx, tok, g_idx]
        s_s = jnp.einsum('bqghd,bqgtd->bqght', qrb, ks) * scale
        p_s = masked_softmax(s_s, (tok <= t[None, :, None, None])[:, :, :, None, :])
        o_s = jnp.einsum('bqght,bqgtd->bqghd', p_s.astype(vs.dtype), vs)
        q0 = t[0]
        kw = lax.dynamic_slice_in_dim(k_wp, q0, WINDOW + Q_BLOCK, axis=1)
        vw = lax.dynamic_slice_in_dim(v_wp, q0, WINDOW + Q_BLOCK, axis=1)
        s_pos = q0 - WINDOW + slab
        win = ((s_pos[None] <= t[:, None]) & (s_pos[None] > t[:, None] - WINDOW)
               & (s_pos[None] >= 0))
        s_w = jnp.einsum('bqghd,bsgd->bqghs', qrb, kw) * scale
        p_w = masked_softmax(s_w, win[None, :, None, None, :])
        o_w = jnp.einsum('bqghs,bsgd->bqghd', p_w.astype(vw.dtype), vw)
        o = gb[..., 0:1] * o_c + gb[..., 1:2] * o_s + gb[..., 2:3] * o_w
        return o.astype(qb.dtype)

    t_blocks = jnp.arange(seq).reshape(-1, Q_BLOCK)
    out = lax.map(block, (to_blocks(q), to_blocks(q_rot), to_blocks(gate), t_blocks))
    return from_blocks(out).reshape(bsz, seq, C_HEADS * HEAD_DIM)


def setup_inputs(seed: int = 0) -> dict:
    key = jax.random.key(seed)
    ks = iter(jax.random.split(key, 32))
    f32 = jnp.float32

    def nrm(shape, scale):
        return jax.random.normal(next(ks), shape, f32) * scale

    def gain(shape):
        return 1.0 + 0.02 * jax.random.normal(next(ks), shape, f32)

    x = nrm((BATCH, SEQ, D_MODEL), 1.0)
    start = jax.random.randint(next(ks), (BATCH, 1), 0, 1024, dtype=jnp.int32)
    positions = start + jnp.arange(SEQ, dtype=jnp.int32)[None, :]
    gp = (DEPTH, S5_GROUPS, S5_STATE)
    return {
        'x': x,
        'positions': positions,
        'ln1_g': gain((DEPTH, D_MODEL)),
        'w_in': nrm((DEPTH, D_MODEL, N_IN), D_MODEL ** -0.5),
        'kv_norm_g': gain((DEPTH, A_KV_RANK)),
        'w_uk': nrm((DEPTH, A_KV_RANK, A_HEADS, A_NOPE), A_KV_RANK ** -0.5),
        'w_uv': nrm((DEPTH, A_KV_RANK, A_HEADS, A_VDIM), A_KV_RANK ** -0.5),
        's5_lambda_re': -0.5 + nrm(gp, 0.01),
        's5_lambda_im': math.pi * jnp.arange(S5_STATE, dtype=f32) + nrm(gp, 0.01),
        's5_log_step': jax.random.uniform(next(ks), (DEPTH, S5_GROUPS), f32,
                                          math.log(S5_DT_MIN), math.log(S5_DT_MAX)),
        's5_b_re': nrm((DEPTH, S5_GROUPS, S5_STATE, S5_GROUP), (2 * S5_GROUP) ** -0.5),
        's5_b_im': nrm((DEPTH, S5_GROUPS, S5_STATE, S5_GROUP), (2 * S5_GROUP) ** -0.5),
        's5_c_re': nrm((DEPTH, S5_GROUPS, S5_GROUP, S5_STATE), (2 * S5_STATE) ** -0.5),
        's5_c_im': nrm((DEPTH, S5_GROUPS, S5_GROUP, S5_STATE), (2 * S5_STATE) ** -0.5),
        's5_d': nrm((DEPTH, S5_GROUPS, S5_GROUP), 1.0),
        's5_glu_w': nrm((DEPTH, B_WIDTH, B_WIDTH), B_WIDTH ** -0.5),
        's5_glu_b': nrm((DEPTH, B_WIDTH), 0.01),
        'cmp_pos_k': nrm((DEPTH, CMP_BLOCK, HEAD_DIM), 0.02),
        'cmp_pos_v': nrm((DEPTH, CMP_BLOCK, HEAD_DIM), 0.02),
        'cmp_k_w1': nrm((DEPTH, CMP_BLOCK * HEAD_DIM, HEAD_DIM), (CMP_BLOCK * HEAD_DIM) ** -0.5),
        'cmp_k_w2': nrm((DEPTH, HEAD_DIM, HEAD_DIM), HEAD_DIM ** -0.5),
        'cmp_v_w1': nrm((DEPTH, CMP_BLOCK * HEAD_DIM, HEAD_DIM), (CMP_BLOCK * HEAD_DIM) ** -0.5),
        'cmp_v_w2': nrm((DEPTH, HEAD_DIM, HEAD_DIM), HEAD_DIM ** -0.5),
        'gain_a': gain((DEPTH, A_HEADS * A_VDIM)),
        'gain_b': gain((DEPTH, B_WIDTH)),
        'gain_c': gain((DEPTH, C_HEADS * HEAD_DIM)),
        'w_out': nrm((DEPTH, D_MIX, D_MODEL), D_MIX ** -0.5),
        'ln2_g': gain((DEPTH, D_MODEL)),
        'w_up': nrm((DEPTH, D_MODEL, D_FF), D_MODEL ** -0.5),
        'w_down': nrm((DEPTH, D_FF, D_MODEL), D_FF ** -0.5),
        'final_g': gain((D_MODEL,)),
    }


def reference(x, positions, ln1_g, w_in, kv_norm_g, w_uk, w_uv,
              s5_lambda_re, s5_lambda_im, s5_log_step, s5_b_re, s5_b_im,
              s5_c_re, s5_c_im, s5_d, s5_glu_w, s5_glu_b,
              cmp_pos_k, cmp_pos_v, cmp_k_w1, cmp_k_w2, cmp_v_w1, cmp_v_w2,
              gain_a, gain_b, gain_c, w_out, ln2_g, w_up, w_down, final_g):
    cos, sin = rope_tables(positions, ROPE_DIM)
    cos_i, sin_i = rope_tables(positions, IDX_ROPE)
    offsets = np.cumsum(IN_SIZES)[:-1].tolist()
    for layer in range(DEPTH):
        h = rms_norm(x, ln1_g[layer])
        (a_q, a_ckv, a_kr, i_q, i_k, i_w, b_u, c_q, c_kv, c_gate) = jnp.split(
            h @ w_in[layer], offsets, axis=-1)
        a_out = dsa_mixer(a_q, a_ckv, a_kr, i_q, i_k, i_w, cos, sin, cos_i, sin_i,
                          kv_norm_g[layer], w_uk[layer], w_uv[layer])
        b_out = s5_mixer(b_u, s5_lambda_re[layer], s5_lambda_im[layer], s5_log_step[layer],
                         s5_b_re[layer], s5_b_im[layer], s5_c_re[layer], s5_c_im[layer],
                         s5_d[layer], s5_glu_w[layer], s5_glu_b[layer])
        c_out = nsa_mixer(c_q, c_kv, c_gate, cos, sin, cmp_pos_k[layer], cmp_pos_v[layer],
                          cmp_k_w1[layer], cmp_k_w2[layer], cmp_v_w1[layer], cmp_v_w2[layer])
        mixed = jnp.concatenate([rms_norm(a_out, gain_a[layer]),
                                 rms_norm(b_out, gain_b[layer]),
                                 rms_norm(c_out, gain_c[layer])], axis=-1)
        x = x + mixed @ w_out[layer]
        z = rms_norm(x, ln2_g[layer]) @ w_up[layer]
        x = x + jnp.square(jax.nn.relu(z)) @ w_down[layer]
    return rms_norm(x, final_g)
```

```python
import functools
import math

import numpy as np
import jax
import jax.numpy as jnp
from jax import lax
from jax.experimental import pallas as pl
from jax.experimental.pallas import tpu as pltpu

F32 = jnp.float32
I32 = jnp.int32
MXU_DTYPE = jnp.bfloat16

D_MODEL = 1024
HEAD_DIM = 64
ROPE_THETA = 500000.0
ROPE_DIM = HEAD_DIM // 4
NORM_EPS = 1e-6
Q_BLOCK = 128
NEG = -1e30
D_FF = 4 * D_MODEL

A_HEADS = 4
A_NOPE = HEAD_DIM - ROPE_DIM
A_VDIM = HEAD_DIM
A_KV_RANK = 128
IDX_HEADS = 4
IDX_DIM = 32
IDX_ROPE = IDX_DIM // 4
DSA_TOPK = 256

B_WIDTH = 256
S5_GROUP = 16
S5_GROUPS = B_WIDTH // S5_GROUP
S5_STATE = 64
S5_NSTATE = S5_GROUPS * S5_STATE

C_HEADS = 8
C_KV_HEADS = 2
C_GROUP = C_HEADS // C_KV_HEADS
CMP_BLOCK = 32
CMP_STRIDE = 16
SEL_BLOCK = 64
SEL_TOPN = 16
SEL_FORCE = 1e9
WINDOW = 512

IN_SIZES = (A_HEADS * HEAD_DIM, A_KV_RANK, ROPE_DIM, IDX_HEADS * IDX_DIM, IDX_DIM, IDX_HEADS,
            B_WIDTH, C_HEADS * HEAD_DIM, 6 * C_KV_HEADS * HEAD_DIM, 3 * C_HEADS)

LANES = 128
SUBLANES = 8
VMEM_LIMIT = 56 * 1024 * 1024

SEG_AQ, SEG_AKV, SEG_IQ, SEG_MISC, SEG_BU, SEG_CQ, SEG_KVC, SEG_KS, SEG_VS, SEG_KW, SEG_VW = (
    0, 256, 512, 640, 768, 1024, 1536, 1792, 1920, 2048, 2176)
N_PROJ = 2304
MISC_IK, MISC_IW, MISC_GATE = 0, 32, 36
TAB_COS, TAB_SIN, TAB_ICOS, TAB_ISIN, TAB_ONE = 0, 8, 16, 20, 24
TAB_PART = 32

PROJ_TM = 512
KEY_CHUNK = 512
S5_TC = 64
MLP_TM = 1024
MLP_TF = 1024


def _cparams(sem):
    return pltpu.CompilerParams(dimension_semantics=sem, vmem_limit_bytes=VMEM_LIMIT)


def _dot(a, b):
    return jnp.dot(a.astype(MXU_DTYPE), b.astype(MXU_DTYPE), preferred_element_type=F32)


def _rms(x, axis):
    return x * lax.rsqrt(jnp.mean(x * x, axis=axis, keepdims=True) + NORM_EPS)


def _rope_tab_kernel(pos_ref, freq_ref, out_ref):
    ang = pos_ref[...].astype(F32) * freq_ref[...]
    lane = lax.broadcasted_iota(I32, ang.shape, 1)
    l32 = lane % TAB_PART
    is_cos = (l32 < TAB_SIN) | ((l32 >= TAB_ICOS) & (l32 < TAB_ISIN))
    is_sin = ((l32 >= TAB_SIN) & (l32 < TAB_ICOS)) | ((l32 >= TAB_ISIN) & (l32 < TAB_ONE))
    val = jnp.where(is_cos, jnp.cos(ang),
                    jnp.where(is_sin, jnp.sin(ang), jnp.where(l32 == TAB_ONE, 1.0, 0.0)))
    hi = val.astype(jnp.bfloat16).astype(F32)
    r1 = val - hi
    mid = r1.astype(jnp.bfloat16).astype(F32)
    lo = r1 - mid
    part = lane // TAB_PART
    out = jnp.where(part == 0, hi, jnp.where(part == 1, mid, jnp.where(part == 2, lo, 0.0)))
    out_ref[...] = out.astype(jnp.bfloat16)


def _rope_table(positions):
    bsz, seq = positions.shape
    t = bsz * seq
    inv_r = (np.float32(ROPE_THETA) ** (-np.arange(0, ROPE_DIM, 2, dtype=np.float32) / ROPE_DIM))
    inv_i = (np.float32(ROPE_THETA) ** (-np.arange(0, IDX_ROPE, 2, dtype=np.float32) / IDX_ROPE))
    f32 = np.zeros(TAB_PART, np.float32)
    f32[TAB_COS:TAB_COS + 8] = inv_r
    f32[TAB_SIN:TAB_SIN + 8] = inv_r
    f32[TAB_ICOS:TAB_ICOS + 4] = inv_i
    f32[TAB_ISIN:TAB_ISIN + 4] = inv_i
    freq = jnp.asarray(np.tile(f32, LANES // TAB_PART)[None, :])
    tq = min(1024, t)
    return pl.pallas_call(
        _rope_tab_kernel,
        out_shape=jax.ShapeDtypeStruct((t, LANES), jnp.bfloat16),
        grid=(t // tq,),
        in_specs=[pl.BlockSpec((tq, 1), lambda i: (i, 0)),
                  pl.BlockSpec((1, LANES), lambda i: (0, 0))],
        out_specs=pl.BlockSpec((tq, LANES), lambda i: (i, 0)),
        compiler_params=_cparams(("arbitrary",)),
        name="rope_table",
    )(positions.reshape(t, 1), freq).reshape(bsz, seq, LANES)


def _proj_layout():
    offs = np.concatenate([[0], np.cumsum(IN_SIZES)])
    o_aq, o_ckv, o_kr, o_iq, o_ik, o_iw, o_bu, o_cq, o_kv, o_gate = offs[:10]
    src = -np.ones(N_PROJ, np.int64)
    partner = -np.ones(N_PROJ, np.int64)
    clane = np.full(N_PROJ, TAB_ONE, np.int64)
    slane = -np.ones(N_PROJ, np.int64)
    ssign = np.zeros(N_PROJ, np.float32)

    def plain(c0, o0, w):
        src[c0:c0 + w] = np.arange(o0, o0 + w)

    def rope(c0, o0, half, cos_lane, sin_lane):
        for j in range(half):
            partner[c0 + j] = o0 + half + j
            partner[c0 + half + j] = o0 + j
            clane[c0 + j] = clane[c0 + half + j] = cos_lane + j
            slane[c0 + j] = slane[c0 + half + j] = sin_lane + j
            ssign[c0 + j] = -1.0
            ssign[c0 + half + j] = 1.0

    plain(SEG_AQ, o_aq, A_HEADS * HEAD_DIM)
    for h in range(A_HEADS):
        rope(SEG_AQ + h * HEAD_DIM, o_aq + h * HEAD_DIM, ROPE_DIM // 2, TAB_COS, TAB_SIN)
    plain(SEG_AKV, o_ckv, A_KV_RANK)
    plain(SEG_AKV + A_KV_RANK, o_kr, ROPE_DIM)
    rope(SEG_AKV + A_KV_RANK, o_kr, ROPE_DIM // 2, TAB_COS, TAB_SIN)
    plain(SEG_IQ, o_iq, IDX_HEADS * IDX_DIM)
    for h in range(IDX_HEADS):
        rope(SEG_IQ + h * IDX_DIM, o_iq + h * IDX_DIM, IDX_ROPE // 2, TAB_ICOS, TAB_ISIN)
    plain(SEG_MISC + MISC_IK, o_ik, IDX_DIM)
    rope(SEG_MISC + MISC_IK, o_ik, IDX_ROPE // 2, TAB_ICOS, TAB_ISIN)
    plain(SEG_MISC + MISC_IW, o_iw, IDX_HEADS)
    plain(SEG_MISC + MISC_GATE, o_gate, 3 * C_HEADS)
    plain(SEG_BU, o_bu, B_WIDTH)
    plain(SEG_CQ, o_cq, C_HEADS * HEAD_DIM)
    for h in range(C_HEADS):
        rope(SEG_CQ + h * HEAD_DIM, o_cq + h * HEAD_DIM, ROPE_DIM // 2, TAB_COS, TAB_SIN)
    plain(SEG_KVC, o_kv, 6 * C_KV_HEADS * HEAD_DIM)
    for seg, sub in ((SEG_KS, 2), (SEG_KW, 4)):
        for g in range(C_KV_HEADS):
            rope(seg + g * HEAD_DIM, o_kv + sub * C_KV_HEADS * HEAD_DIM + g * HEAD_DIM,
                 ROPE_DIM // 2, TAB_COS, TAB_SIN)

    k = np.arange(LANES)[:, None]
    live = k < 3 * TAB_PART
    ec = (live & ((k % TAB_PART) == clane[None, :])).astype(np.float32)
    es = (live & ((k % TAB_PART) == slane[None, :])).astype(np.float32) * ssign[None, :]
    return src, partner, ec, es


def _in_proj_kernel(x_ref, tab_ref, g_ref, kvg_ref, w_ref, wsw_ref, ec_ref, es_ref,
                    aqT_ref, akv_ref, acT_ref, iqT_ref, misc_ref, miscT_ref, u_ref,
                    cqT_ref, cqrT_ref, kvc_ref, ks_ref, vsT_ref, kw_ref, vwT_ref):
    x = x_ref[...]
    n = (_rms(x, -1) * g_ref[...]).astype(MXU_DTYPE)
    tab = tab_ref[...]

    def plain(c0, w):
        return jnp.dot(n, w_ref[:, c0:c0 + w], preferred_element_type=F32)

    def roped(c0, w):
        p = plain(c0, w)
        psw = jnp.dot(n, wsw_ref[:, c0:c0 + w], preferred_element_type=F32)
        cos = jnp.dot(tab, ec_ref[:, c0:c0 + w], preferred_element_type=F32)
        sin = jnp.dot(tab, es_ref[:, c0:c0 + w], preferred_element_type=F32)
        return p * cos + psw * sin

    def chunked_t(val, ref):
        vt = val.T.astype(ref.dtype)
        for c in range(ref.shape[0]):
            ref[c] = vt[:, c * LANES:(c + 1) * LANES]

    aqT_ref[...] = roped(SEG_AQ, 256).T.astype(aqT_ref.dtype)

    akv = roped(SEG_AKV, 256)
    lat = _rms(akv[:, :A_KV_RANK], -1) * kvg_ref[...]
    akv_ref[...] = jnp.concatenate([lat, akv[:, A_KV_RANK:]], axis=1).astype(akv_ref.dtype)
    chunked_t(lat, acT_ref)

    iqT_ref[...] = roped(SEG_IQ, 128).T.astype(iqT_ref.dtype)
    misc = roped(SEG_MISC, 128)
    misc_ref[...] = misc
    miscT_ref[...] = misc.T
    u_ref[...] = plain(SEG_BU, 256)
    cqT_ref[...] = plain(SEG_CQ, 512).T.astype(cqT_ref.dtype)
    cqrT_ref[...] = roped(SEG_CQ, 512).T.astype(cqrT_ref.dtype)
    kvc_ref[...] = plain(SEG_KVC, 256)
    ks_ref[...] = roped(SEG_KS, 128).astype(ks_ref.dtype)
    chunked_t(plain(SEG_VS, 128), vsT_ref)
    kw_ref[...] = roped(SEG_KW, 128).astype(kw_ref.dtype)
    chunked_t(plain(SEG_VW, 128), vwT_ref)


def _in_proj(x, tab, ln_g, kv_g, w_r, w_sw, ec, es):
    bsz, seq, _ = x.shape
    tm = min(PROJ_TM, seq)
    nj = seq // tm
    nck = tm // LANES
    bf = MXU_DTYPE

    def rows(w):
        return pl.BlockSpec((None, tm, w), lambda b, j: (b, j, 0))

    def cols(w):
        return pl.BlockSpec((None, w, tm), lambda b, j: (b, 0, j))

    def chunks():
        return pl.BlockSpec((None, nck, LANES, LANES), lambda b, j: (b, j, 0, 0))

    def const(shape):
        return pl.BlockSpec(shape, lambda b, j: (0,) * len(shape))

    sds = jax.ShapeDtypeStruct
    out_shape = (
        sds((bsz, 256, seq), bf),
        sds((bsz, seq, 256), bf),
        sds((bsz, seq // LANES, LANES, LANES), bf),
        sds((bsz, 128, seq), bf),
        sds((bsz, seq, 128), F32),
        sds((bsz, 128, seq), F32),
        sds((seq, bsz * B_WIDTH), F32),
        sds((bsz, 512, seq), bf),
        sds((bsz, 512, seq), bf),
        sds((bsz, seq, 256), F32),
        sds((bsz, seq, 128), bf),
        sds((bsz, seq // LANES, LANES, LANES), bf),
        sds((bsz, seq, 128), bf),
        sds((bsz, seq // LANES, LANES, LANES), bf),
    )
    out_specs = (cols(256), rows(256), chunks(), cols(128), rows(128), cols(128),
                 pl.BlockSpec((tm, B_WIDTH), lambda b, j: (j, b)),
                 cols(512), cols(512), rows(256), rows(128), chunks(), rows(128), chunks())
    return pl.pallas_call(
        _in_proj_kernel,
        out_shape=out_shape,
        grid=(bsz, nj),
        in_specs=[rows(D_MODEL), rows(LANES), const((1, D_MODEL)), const((1, A_KV_RANK)),
                  const((D_MODEL, N_PROJ)), const((D_MODEL, N_PROJ)),
                  const((LANES, N_PROJ)), const((LANES, N_PROJ))],
        out_specs=out_specs,
        compiler_params=_cparams(("arbitrary", "arbitrary")),
        name="in_proj",
    )(x, tab, ln_g.reshape(1, -1), kv_g.reshape(1, -1), w_r, w_sw, ec, es)


def _s5_disc_kernel(lr_ref, li_ref, ls_ref, br_ref, bi_ref, ar_ref, ai_ref, bbr_ref, bbi_ref):
    lr, li = lr_ref[...], li_ref[...]
    step = jnp.exp(ls_ref[...])
    mag = jnp.exp(lr * step)
    ar = mag * jnp.cos(li * step)
    ai = mag * jnp.sin(li * step)
    den = lr * lr + li * li
    zr = ((ar - 1.0) * lr + ai * li) / den
    zi = (ai * lr - (ar - 1.0) * li) / den
    br, bi = br_ref[...], bi_ref[...]
    ar_ref[...] = ar
    ai_ref[...] = ai
    bbr_ref[...] = zr * br - zi * bi
    bbi_ref[...] = zr * bi + zi * br


def _s5_discretize(lam_re, lam_im, log_step, b_re, b_im):
    g, p, h = b_re.shape
    ex = lambda a: jnp.repeat(a.astype(F32), h, axis=1)
    ls = jnp.broadcast_to(log_step.astype(F32)[:, None], (g, p * h))
    sds = jax.ShapeDtypeStruct((g, p * h), F32)
    ar, ai, bbr, bbi = pl.pallas_call(
        _s5_disc_kernel, out_shape=(sds, sds, sds, sds), name="s5_discretize",
    )(ex(lam_re), ex(lam_im), ls, b_re.astype(F32).reshape(g, p * h), b_im.astype(F32).reshape(g, p * h))
    ar = ar.reshape(g, p, h)[:, :, 0]
    ai = ai.reshape(g, p, h)[:, :, 0]
    return ar, ai, bbr.reshape(g, p, h), bbi.reshape(g, p, h)


def _s5_kernel(u_ref, bmat_ref, a_ref, cre_ref, cim_ref, d_ref, gw_ref, gb_ref, gain_ref,
               o_ref, x_scr, h_scr, *, tc):
    ns = S5_NSTATE

    @pl.when(pl.program_id(0) == 0)
    def _():
        h_scr[...] = jnp.zeros_like(h_scr)

    u = u_ref[...]
    x_scr[...] = _dot(u, bmat_ref[...])
    ar = a_ref[0:SUBLANES, :]
    ai = a_ref[SUBLANES:2 * SUBLANES, :]

    def step(t, carry):
        hr, hi = carry
        r0 = pl.multiple_of(t * SUBLANES, SUBLANES)
        xr = x_scr[pl.ds(r0, SUBLANES), 0:ns]
        xi = x_scr[pl.ds(r0, SUBLANES), ns:2 * ns]
        nhr = ar * hr - ai * hi + xr
        nhi = ar * hi + ai * hr + xi
        x_scr[pl.ds(r0, SUBLANES), 0:ns] = nhr
        x_scr[pl.ds(r0, SUBLANES), ns:2 * ns] = nhi
        return nhr, nhi

    hr, hi = lax.fori_loop(0, tc, step, (h_scr[:, 0:ns], h_scr[:, ns:2 * ns]), unroll=8)
    h_scr[:, 0:ns] = hr
    h_scr[:, ns:2 * ns] = hi

    y = (_dot(x_scr[:, 0:ns], cre_ref[...]) - _dot(x_scr[:, ns:2 * ns], cim_ref[...])
         + d_ref[...] * u)
    y = jax.nn.gelu(y)
    y = y * jax.nn.sigmoid(_dot(y, gw_ref[...]) + gb_ref[...])
    o_ref[...] = (_rms(y, -1) * gain_ref[...]).astype(o_ref.dtype)


def _s5(u_tm, bsz, ar, ai, bbr, bbi, c_re, c_im, d, glu_w, glu_b, gain):
    assert bsz == SUBLANES, "the S5 scan keeps one batch row per sublane"
    seq = u_tm.shape[0]
    tc = min(S5_TC, seq)
    rows = tc * bsz
    eye = jnp.eye(S5_GROUPS, dtype=F32)
    bmat = jnp.concatenate(
        [jnp.einsum('gph,gk->ghkp', b, eye).reshape(B_WIDTH, S5_NSTATE) for b in (bbr, bbi)], axis=1)
    cre = jnp.einsum('ghp,gk->gpkh', c_re.astype(F32), eye).reshape(S5_NSTATE, B_WIDTH)
    cim = jnp.einsum('ghp,gk->gpkh', c_im.astype(F32), eye).reshape(S5_NSTATE, B_WIDTH)
    avec = jnp.concatenate([jnp.broadcast_to(a.reshape(1, S5_NSTATE), (SUBLANES, S5_NSTATE))
                            for a in (ar, ai)], axis=0)
    const = lambda shape: pl.BlockSpec(shape, lambda i: (0,) * len(shape))
    out = pl.pallas_call(
        functools.partial(_s5_kernel, tc=tc),
        out_shape=jax.ShapeDtypeStruct((seq * bsz, B_WIDTH), MXU_DTYPE),
        grid=(seq // tc,),
        in_specs=[pl.BlockSpec((rows, B_WIDTH), lambda i: (i, 0)),
                  const((B_WIDTH, 2 * S5_NSTATE)), const((2 * SUBLANES, S5_NSTATE)),
                  const((S5_NSTATE, B_WIDTH)), const((S5_NSTATE, B_WIDTH)),
                  const((1, B_WIDTH)), const((B_WIDTH, B_WIDTH)), const((1, B_WIDTH)),
                  const((1, B_WIDTH))],
        out_specs=pl.BlockSpec((rows, B_WIDTH), lambda i: (i, 0)),
        scratch_shapes=[pltpu.VMEM((rows, 2 * S5_NSTATE), F32),
                        pltpu.VMEM((SUBLANES, 2 * S5_NSTATE), F32)],
        compiler_params=_cparams(("arbitrary",)),
        name="s5_scan",
    )(u_tm.reshape(seq * bsz, B_WIDTH), bmat.astype(MXU_DTYPE), avec,
      cre.astype(MXU_DTYPE), cim.astype(MXU_DTYPE), d.astype(F32).reshape(1, B_WIDTH),
      glu_w.astype(MXU_DTYPE), glu_b.astype(F32).reshape(1, B_WIDTH), gain.reshape(1, B_WIDTH))
    return out.reshape(seq, bsz * B_WIDTH)


def _compress_kernel(seg_ref, pa_ref, pb_ref, wa_ref, wb_ref, w2_ref, cmp_ref, vT_ref):
    seg = seg_ref[...]
    xa = _dot(seg + pa_ref[...], wa_ref[...])
    xb = _dot(seg + pb_ref[...], wb_ref[...])
    nseg = seg.shape[0]
    pre = xa + pltpu.roll(xb, nseg - 1, 0)
    out = _dot(jax.nn.gelu(pre), w2_ref[...])
    cmp_ref[...] = out.astype(cmp_ref.dtype)
    vT_ref[...] = out[:, LANES:].T.astype(vT_ref.dtype)


def _compress(kvc, pos_k, pos_v, k_w1, k_w2, v_w1, v_w2):
    bsz, seq, _ = kvc.shape
    nseg = seq // CMP_STRIDE
    width = CMP_STRIDE * 256
    eye = jnp.eye(4, dtype=F32)
    w1 = jnp.stack([k_w1, k_w1, v_w1, v_w1]).astype(F32).reshape(4, CMP_BLOCK, HEAD_DIM, HEAD_DIM)
    pos = jnp.stack([pos_k, pos_k, pos_v, pos_v]).astype(F32)

    def half(lo):
        w = jnp.einsum('slde,st->lsdte', w1[:, lo:lo + CMP_STRIDE], eye).reshape(width, 256)
        p = jnp.transpose(pos[:, lo:lo + CMP_STRIDE], (1, 0, 2)).reshape(1, width)
        return w.astype(MXU_DTYPE), p

    wa, pa = half(0)
    wb, pb = half(CMP_STRIDE)
    w2 = jnp.einsum('sde,st->sdte', jnp.stack([k_w2, k_w2, v_w2, v_w2]).astype(F32), eye).reshape(256, 256)
    const = lambda shape: pl.BlockSpec(shape, lambda b: (0,) * len(shape))
    return pl.pallas_call(
        _compress_kernel,
        out_shape=(jax.ShapeDtypeStruct((bsz, nseg, 256), MXU_DTYPE),
                   jax.ShapeDtypeStruct((bsz, 128, nseg), MXU_DTYPE)),
        grid=(bsz,),
        in_specs=[pl.BlockSpec((None, nseg, width), lambda b: (b, 0, 0)),
                  const((1, width)), const((1, width)), const((width, 256)), const((width, 256)),
                  const((256, 256))],
        out_specs=(pl.BlockSpec((None, nseg, 256), lambda b: (b, 0, 0)),
                   pl.BlockSpec((None, 128, nseg), lambda b: (b, 0, 0))),
        compiler_params=_cparams(("arbitrary",)),
        name="nsa_compress",
    )(kvc.reshape(bsz, nseg, width), pa, pb, wa, wb, w2.astype(MXU_DTYPE))


def _online_softmax_step(s, mask, v_t, m_ref, l_ref, acc_ref):
    s = jnp.where(mask, s, NEG)
    m_old = m_ref[...]
    m_new = jnp.maximum(m_old, jnp.max(s, axis=0, keepdims=True))
    p = jnp.where(mask, jnp.exp(s - m_new), 0.0)
    alpha = jnp.exp(m_old - m_new)
    l_ref[...] = alpha * l_ref[...] + jnp.sum(p, axis=0, keepdims=True)
    acc_ref[...] = alpha * acc_ref[...] + _dot(v_t, p)
    m_ref[...] = m_new


def _tile4(x):
    return jnp.concatenate([x, x, x, x], axis=1)


INT_MIN = -2 ** 31


def _dsa_kernel(qT_ref, iqT_ref, miscT_ref, kv_ref, misc_ref, cT_ref, mq_ref, wuv_ref, gain_ref,
                o_ref, key_scr, m_scr, l_scr, acc_scr, *, k_top, cut_bits):
    i = pl.program_id(1)
    t0 = i * Q_BLOCK
    kc = KEY_CHUNK
    nch = (t0 + Q_BLOCK + kc - 1) // kc
    t_lane = t0 + lax.broadcasted_iota(I32, (1, Q_BLOCK), 1)
    row_iota = lax.broadcasted_iota(I32, (kc, Q_BLOCK), 0)

    qcat = _dot(mq_ref[...], qT_ref[...]).astype(MXU_DTYPE)
    qs_t = jnp.concatenate([qcat[h * 256:(h + 1) * 256] for h in range(A_HEADS)], axis=1)
    iq_t = iqT_ref[...]
    w_t = miscT_ref[MISC_IW:MISC_IW + IDX_HEADS, :] * (IDX_HEADS ** -0.5)

    def idx_body(c, _):
        k0 = pl.multiple_of(c * kc, kc)
        ik = misc_ref[pl.ds(k0, kc), MISC_IK:MISC_IK + IDX_DIM]
        score = jnp.zeros((kc, Q_BLOCK), F32)
        for h in range(IDX_HEADS):
            d = _dot(ik, iq_t[h * IDX_DIM:(h + 1) * IDX_DIM, :])
            score = score + jnp.maximum(d * (IDX_DIM ** -0.5), 0.0) * w_t[h:h + 1, :]
        score = jnp.where(row_iota + k0 <= t_lane, score, -jnp.inf)
        bits = pltpu.bitcast(score, I32)
        key_scr[pl.ds(k0, kc), :] = bits ^ ((bits >> 31) & 0x7FFFFFFF)
        return 0

    lax.fori_loop(0, nch, idx_body, 0)

    def count(pred):
        def body(c, acc):
            k0 = pl.multiple_of(c * kc, kc)
            hit = pred(key_scr[pl.ds(k0, kc), :], row_iota + k0)
            return acc + jnp.sum(hit.reshape(kc // SUBLANES, SUBLANES, Q_BLOCK), axis=0)
        acc = lax.fori_loop(0, nch, body, jnp.zeros((SUBLANES, Q_BLOCK), I32))
        return jnp.sum(acc, axis=0, keepdims=True)

    def thr_bit(b, prefix):
        cand = prefix | lax.shift_left(jnp.int32(1), 31 - b)
        cand_s = cand ^ INT_MIN
        cnt = count(lambda keys, idx: jnp.where(keys >= cand_s, 1, 0))
        return jnp.where(cnt >= k_top, cand, prefix)

    thr = lax.fori_loop(0, 32, thr_bit, jnp.zeros((1, Q_BLOCK), I32)) ^ INT_MIN
    n_gt = count(lambda keys, idx: jnp.where(keys > thr, 1, 0))
    n_eq = count(lambda keys, idx: jnp.where(keys == thr, 1, 0))
    need = k_top - n_gt

    def cut_bit(b, prefix):
        cand = prefix | lax.shift_left(jnp.int32(1), cut_bits - 1 - b)
        cnt = count(lambda keys, idx: jnp.where(keys == thr, jnp.where(idx < cand, 1, 0), 0))
        return jnp.where(cnt <= need, cand, prefix)

    tied = jnp.max(jnp.where(n_eq > need, 1, 0))
    cut = lax.cond(tied > 0,
                   lambda: lax.fori_loop(0, cut_bits, cut_bit, jnp.zeros((1, Q_BLOCK), I32)),
                   lambda: jnp.full((1, Q_BLOCK), 2 ** cut_bits, I32))

    m_scr[...] = jnp.full_like(m_scr, NEG)
    l_scr[...] = jnp.zeros_like(l_scr)
    acc_scr[...] = jnp.zeros_like(acc_scr)

    def att_body(c, _):
        k0 = pl.multiple_of(c * kc, kc)
        s = _dot(kv_ref[pl.ds(k0, kc), :], qs_t) * (HEAD_DIM ** -0.5)
        keys = key_scr[pl.ds(k0, kc), :]
        idx = row_iota + k0
        sel = jnp.where(keys > thr, 1, jnp.where(keys == thr, jnp.where(idx < cut, 1, 0), 0))
        sel = jnp.where(idx <= t_lane, sel, 0)
        v_t = jnp.concatenate([cT_ref[c * (kc // LANES) + j] for j in range(kc // LANES)], axis=1)
        _online_softmax_step(s, _tile4(sel) > 0, v_t, m_scr, l_scr, acc_scr)
        return 0

    lax.fori_loop(0, nch, att_body, 0)

    o_lat = (acc_scr[...] / l_scr[...]).astype(MXU_DTYPE)
    out_t = jnp.concatenate(
        [_dot(wuv_ref[h], o_lat[:, h * Q_BLOCK:(h + 1) * Q_BLOCK]) for h in range(A_HEADS)], axis=0)
    out_t = _rms(out_t, 0) * gain_ref[...]
    o_ref[...] = out_t.T.astype(o_ref.dtype)


def _dsa(aq_t, iq_t, misc_t, akv, misc, ac_t, w_uk, w_uv, gain):
    bsz, _, seq = aq_t.shape
    nq = seq // Q_BLOCK
    k_top = min(DSA_TOPK, seq // 4)
    seq_pad = -(-seq // KEY_CHUNK) * KEY_CHUNK
    mq = jnp.zeros((A_HEADS, 256, A_HEADS, HEAD_DIM), F32)
    for h in range(A_HEADS):
        mq = mq.at[h, :A_KV_RANK, h, ROPE_DIM:].set(w_uk[:, h, :].astype(F32))
        mq = mq.at[h, A_KV_RANK:A_KV_RANK + ROPE_DIM, h, :ROPE_DIM].set(jnp.eye(ROPE_DIM, dtype=F32))
    mq = mq.reshape(A_HEADS * 256, A_HEADS * HEAD_DIM).astype(MXU_DTYPE)
    wuv_t = jnp.transpose(w_uv, (1, 2, 0)).astype(MXU_DTYPE)
    per_q = lambda w: pl.BlockSpec((None, w, Q_BLOCK), lambda b, i: (b, 0, i))
    per_b = lambda *s: pl.BlockSpec((None,) + s, lambda b, i: (b,) + (0,) * len(s))
    const = lambda shape: pl.BlockSpec(shape, lambda b, i: (0,) * len(shape))
    return pl.pallas_call(
        functools.partial(_dsa_kernel, k_top=k_top, cut_bits=int(seq_pad).bit_length()),
        out_shape=jax.ShapeDtypeStruct((bsz, seq, A_HEADS * A_VDIM), MXU_DTYPE),
        grid=(bsz, nq),
        in_specs=[per_q(256), per_q(128), per_q(128), per_b(seq, 256), per_b(seq, 128),
                  per_b(seq // LANES, LANES, LANES), const(mq.shape), const(wuv_t.shape),
                  const((A_HEADS * A_VDIM, 1))],
        out_specs=pl.BlockSpec((None, Q_BLOCK, A_HEADS * A_VDIM), lambda b, i: (b, i, 0)),
        scratch_shapes=[pltpu.VMEM((seq_pad, Q_BLOCK), I32),
                        pltpu.VMEM((1, A_HEADS * Q_BLOCK), F32),
                        pltpu.VMEM((1, A_HEADS * Q_BLOCK), F32),
                        pltpu.VMEM((A_KV_RANK, A_HEADS * Q_BLOCK), F32)],
        compiler_params=_cparams(("arbitrary", "arbitrary")),
        name="dsa_attention",
    )(aq_t, iq_t, misc_t, akv, misc, ac_t, mq, wuv_t, gain.reshape(-1, 1))


def _nsa_kernel(qT_ref, qrT_ref, miscT_ref, cmp_ref, vcT_ref, ks_ref, vsT_ref, kw_ref, vwT_ref,
                ov_ref, gain_ref, o_ref, sel_scr, m_scr, l_scr, acc_scr, *, n_top, n_blk):
    i = pl.program_id(1)
    t0 = i * Q_BLOCK
    kc = KEY_CHUNK
    nch = (t0 + Q_BLOCK + kc - 1) // kc
    scale = HEAD_DIM ** -0.5
    t_lane = t0 + lax.broadcasted_iota(I32, (1, Q_BLOCK), 1)
    gates = jax.nn.sigmoid(miscT_ref[MISC_GATE:MISC_GATE + 3 * C_HEADS, :])
    n_cmp = cmp_ref.shape[0]
    cmp_iota = lax.broadcasted_iota(I32, (n_cmp, Q_BLOCK), 0)
    blk_iota = lax.broadcasted_iota(I32, (n_blk, Q_BLOCK), 0)
    row_iota = lax.broadcasted_iota(I32, (kc, Q_BLOCK), 0)
    win_iota = lax.broadcasted_iota(I32, (Q_BLOCK, Q_BLOCK), 0)
    slabs = []

    for g in range(C_KV_HEADS):
        heads = range(g * C_GROUP, (g + 1) * C_GROUP)
        q_t = jnp.concatenate([qT_ref[h * HEAD_DIM:(h + 1) * HEAD_DIM, :] for h in heads], axis=1)
        qr_t = jnp.concatenate([qrT_ref[h * HEAD_DIM:(h + 1) * HEAD_DIM, :] for h in heads], axis=1)
        gsl = slice(g * HEAD_DIM, (g + 1) * HEAD_DIM)

        s_c = _dot(cmp_ref[:, gsl], q_t) * scale
        vis = _tile4(jnp.where(cmp_iota * CMP_STRIDE + (CMP_BLOCK - 1) <= t_lane, 1, 0)) > 0
        s_c = jnp.where(vis, s_c, NEG)
        p_c = jnp.where(vis, jnp.exp(s_c - jnp.max(s_c, axis=0, keepdims=True)), 0.0)
        l_c = jnp.sum(p_c, axis=0, keepdims=True)
        p_c = p_c * (1.0 / jnp.maximum(l_c, 1e-30))
        o_c = _dot(vcT_ref[gsl, :], p_c)

        p_sum = (p_c[:, 0:Q_BLOCK] + p_c[:, Q_BLOCK:2 * Q_BLOCK]
                 + p_c[:, 2 * Q_BLOCK:3 * Q_BLOCK] + p_c[:, 3 * Q_BLOCK:4 * Q_BLOCK])
        p_hi = p_sum.astype(MXU_DTYPE)
        p_lo = p_sum - p_hi.astype(F32)
        imp = _dot(ov_ref[...], p_hi) + _dot(ov_ref[...], p_lo)
        cur = t_lane // SEL_BLOCK
        forced = jnp.where(blk_iota == 0, 1, jnp.where(blk_iota == cur, 1,
                           jnp.where(blk_iota == cur - 1, 1, 0)))
        imp = jnp.where(forced > 0, SEL_FORCE, imp)
        imp = jnp.where(blk_iota * SEL_BLOCK <= t_lane, imp, -jnp.inf)
        rank = jnp.zeros((n_blk, Q_BLOCK), I32)
        for mp in range(n_blk):
            row = imp[mp:mp + 1, :]
            rank = rank + jnp.where(row > imp, 1,
                                    jnp.where(row == imp, jnp.where(blk_iota > mp, 1, 0), 0))
        sel_scr[...] = jnp.where(rank < n_top, 1.0, 0.0)

        m_scr[...] = jnp.full_like(m_scr, NEG)
        l_scr[...] = jnp.zeros_like(l_scr)
        acc_scr[...] = jnp.zeros_like(acc_scr)
        bpc = kc // SEL_BLOCK

        def sel_body(c, _):
            k0 = pl.multiple_of(c * kc, kc)
            s = _dot(ks_ref[pl.ds(k0, kc), gsl], qr_t) * scale
            sel8 = sel_scr[pl.ds(pl.multiple_of(c * bpc, bpc), bpc), :]
            sel = jnp.concatenate(
                [jnp.broadcast_to(sel8[j:j + 1, :], (SEL_BLOCK, Q_BLOCK)) for j in range(bpc)], axis=0)
            sel = jnp.where(row_iota + k0 <= t_lane, sel, 0.0)
            v_t = jnp.concatenate([vsT_ref[c * (kc // LANES) + j][gsl, :]
                                   for j in range(kc // LANES)], axis=1)
            _online_softmax_step(s, _tile4(sel) > 0, v_t, m_scr, l_scr, acc_scr)
            return 0

        lax.fori_loop(0, nch, sel_body, 0)
        o_s = acc_scr[...] / l_scr[...]

        k_parts, v_parts, m_parts = [], [], []
        for j in range(WINDOW // Q_BLOCK + 1):
            cj = i - WINDOW // Q_BLOCK + j
            cjc = jnp.maximum(cj, 0)
            k0 = pl.multiple_of(cjc * Q_BLOCK, Q_BLOCK)
            k_parts.append(kw_ref[pl.ds(k0, Q_BLOCK), gsl])
            v_parts.append(vwT_ref[cjc][gsl, :])
            kidx = win_iota + k0
            inside = jnp.where(kidx <= t_lane, jnp.where(kidx > t_lane - WINDOW, 1, 0), 0)
            m_parts.append(jnp.where(cj >= 0, inside, 0))
        s_w = _dot(jnp.concatenate(k_parts, axis=0), qr_t) * scale
        win = _tile4(jnp.concatenate(m_parts, axis=0)) > 0
        s_w = jnp.where(win, s_w, NEG)
        p_w = jnp.where(win, jnp.exp(s_w - jnp.max(s_w, axis=0, keepdims=True)), 0.0)
        l_w = jnp.sum(p_w, axis=0, keepdims=True)
        o_w = _dot(jnp.concatenate(v_parts, axis=1), p_w) / l_w

        for hh in range(C_GROUP):
            sl = slice(hh * Q_BLOCK, (hh + 1) * Q_BLOCK)
            r = (g * C_GROUP + hh) * 3
            slabs.append(gates[r:r + 1, :] * o_c[:, sl] + gates[r + 1:r + 2, :] * o_s[:, sl]
                         + gates[r + 2:r + 3, :] * o_w[:, sl])

    out_t = jnp.concatenate(slabs, axis=0)
    out_t = _rms(out_t, 0) * gain_ref[...]
    o_ref[...] = out_t.T.astype(o_ref.dtype)


def _sel_overlap_t(n_cmp_rows, n_blk):
    cs = np.arange(n_cmp_rows)[None, :] * CMP_STRIDE
    ss = np.arange(n_blk)[:, None] * SEL_BLOCK
    ov = np.minimum(cs + CMP_BLOCK, ss + SEL_BLOCK) - np.maximum(cs, ss)
    return np.clip(ov, 0, None).astype(np.float32) / CMP_BLOCK


def _nsa(cq_t, cqr_t, misc_t, cmp, vcmp_t, ks, vs_t, kw, vw_t, gain):
    bsz, _, seq = cq_t.shape
    nq = seq // Q_BLOCK
    nseg = cmp.shape[1]
    n_blk = seq // SEL_BLOCK
    n_top = min(SEL_TOPN, n_blk)
    seq_pad = -(-seq // KEY_CHUNK) * KEY_CHUNK
    ov_np = _sel_overlap_t(nseg, n_blk)
    ov_np[:, (seq - CMP_BLOCK) // CMP_STRIDE + 1:] = 0.0
    ov = jnp.asarray(ov_np, MXU_DTYPE)
    per_q = lambda w: pl.BlockSpec((None, w, Q_BLOCK), lambda b, i: (b, 0, i))
    per_b = lambda *s: pl.BlockSpec((None,) + s, lambda b, i: (b,) + (0,) * len(s))
    const = lambda shape: pl.BlockSpec(shape, lambda b, i: (0,) * len(shape))
    return pl.pallas_call(
        functools.partial(_nsa_kernel, n_top=n_top, n_blk=n_blk),
        out_shape=jax.ShapeDtypeStruct((bsz, seq, C_HEADS * HEAD_DIM), MXU_DTYPE),
        grid=(bsz, nq),
        in_specs=[per_q(512), per_q(512), per_q(128), per_b(nseg, 256), per_b(128, nseg),
                  per_b(seq, 128), per_b(seq // LANES, LANES, LANES),
                  per_b(seq, 128), per_b(seq // LANES, LANES, LANES),
                  const(ov.shape), const((C_HEADS * HEAD_DIM, 1))],
        out_specs=pl.BlockSpec((None, Q_BLOCK, C_HEADS * HEAD_DIM), lambda b, i: (b, i, 0)),
        scratch_shapes=[pltpu.VMEM((max(n_blk, seq_pad // SEL_BLOCK), Q_BLOCK), F32),
                        pltpu.VMEM((1, C_GROUP * Q_BLOCK), F32),
                        pltpu.VMEM((1, C_GROUP * Q_BLOCK), F32),
                        pltpu.VMEM((HEAD_DIM, C_GROUP * Q_BLOCK), F32)],
        compiler_params=_cparams(("arbitrary", "arbitrary")),
        name="nsa_attention",
    )(cq_t, cqr_t, misc_t, cmp, vcmp_t, ks, vs_t, kw, vw_t, ov, gain.reshape(-1, 1))


def _out_mlp_kernel(x_ref, a_ref, b_ref, c_ref, wo_ref, g2_ref, wu_ref, wd_ref, fg_ref,
                    o_ref, x1_scr, n2_scr, acc_scr, *, final_norm):
    j = pl.program_id(1)

    @pl.when(j == 0)
    def _():
        mixed = (jnp.dot(a_ref[...], wo_ref[0:256, :], preferred_element_type=F32)
                 + jnp.dot(b_ref[...], wo_ref[256:512, :], preferred_element_type=F32)
                 + jnp.dot(c_ref[...], wo_ref[512:1024, :], preferred_element_type=F32))
        x1 = x_ref[...] + mixed
        x1_scr[...] = x1
        n2_scr[...] = (_rms(x1, -1) * g2_ref[...]).astype(n2_scr.dtype)
        acc_scr[...] = jnp.zeros_like(acc_scr)

    z = jnp.dot(n2_scr[...], wu_ref[...], preferred_element_type=F32)
    acc_scr[...] += _dot(jnp.square(jnp.maximum(z, 0.0)), wd_ref[...])

    @pl.when(j == pl.num_programs(1) - 1)
    def _():
        x2 = x1_scr[...] + acc_scr[...]
        if final_norm:
            x2 = _rms(x2, -1) * fg_ref[...]
        o_ref[...] = x2


def _out_mlp(x, a_n, b_tm, c_n, w_out, ln2_g, w_up, w_down, final_g, final_norm):
    bsz, seq, _ = x.shape
    tm = min(MLP_TM, seq)
    nj = seq // tm
    nf = D_FF // MLP_TF
    rows = lambda w: pl.BlockSpec((None, tm, w), lambda r, f: (r // nj, r % nj, 0))
    const = lambda shape: pl.BlockSpec(shape, lambda r, f: (0,) * len(shape))
    return pl.pallas_call(
        functools.partial(_out_mlp_kernel, final_norm=final_norm),
        out_shape=jax.ShapeDtypeStruct(x.shape, F32),
        grid=(bsz * nj, nf),
        in_specs=[rows(D_MODEL), rows(256),
                  pl.BlockSpec((tm, B_WIDTH), lambda r, f: (r % nj, r // nj)),
                  rows(512), const((D_MODEL, D_MODEL)), const((1, D_MODEL)),
                  pl.BlockSpec((D_MODEL, MLP_TF), lambda r, f: (0, f)),
                  pl.BlockSpec((MLP_TF, D_MODEL), lambda r, f: (f, 0)),
                  const((1, D_MODEL))],
        out_specs=rows(D_MODEL),
        scratch_shapes=[pltpu.VMEM((tm, D_MODEL), F32), pltpu.VMEM((tm, D_MODEL), MXU_DTYPE),
                        pltpu.VMEM((tm, D_MODEL), F32)],
        compiler_params=_cparams(("arbitrary", "arbitrary")),
        name="out_proj_mlp",
    )(x, a_n, b_tm, c_n, w_out.astype(MXU_DTYPE), ln2_g.reshape(1, -1), w_up.astype(MXU_DTYPE),
      w_down.astype(MXU_DTYPE), final_g.reshape(1, -1))


def kernel(x, positions, ln1_g, w_in, kv_norm_g, w_uk, w_uv, s5_lambda_re, s5_lambda_im, s5_log_step, s5_b_re, s5_b_im, s5_c_re, s5_c_im, s5_d, s5_glu_w, s5_glu_b, cmp_pos_k, cmp_pos_v, cmp_k_w1, cmp_k_w2, cmp_v_w1, cmp_v_w2, gain_a, gain_b, gain_c, w_out, ln2_g, w_up, w_down, final_g):
    bsz, seq, _ = x.shape
    depth = w_in.shape[0]
    src, partner, ec_np, es_np = _proj_layout()
    ec = jnp.asarray(ec_np, MXU_DTYPE)
    es = jnp.asarray(es_np, MXU_DTYPE)
    tab = _rope_table(positions)

    def regather(w, idx):
        cols = jnp.take(w, jnp.asarray(np.maximum(idx, 0)), axis=1)
        return jnp.where(jnp.asarray(idx >= 0)[None, :], cols, 0.0).astype(MXU_DTYPE)

    for layer in range(depth):
        w_r = regather(w_in[layer], src)
        w_sw = regather(w_in[layer], partner)
        (aq_t, akv, ac_t, iq_t, misc, misc_t, u_tm, cq_t, cqr_t, kvc, ks, vs_t, kw, vw_t) = _in_proj(
            x, tab, ln1_g[layer], kv_norm_g[layer], w_r, w_sw, ec, es)

        ar, ai, bbr, bbi = _s5_discretize(s5_lambda_re[layer], s5_lambda_im[layer], s5_log_step[layer],
                                          s5_b_re[layer], s5_b_im[layer])
        b_tm = _s5(u_tm, bsz, ar, ai, bbr, bbi, s5_c_re[layer], s5_c_im[layer], s5_d[layer].reshape(-1),
                   s5_glu_w[layer], s5_glu_b[layer], gain_b[layer])

        cmp, vcmp_t = _compress(kvc, cmp_pos_k[layer], cmp_pos_v[layer], cmp_k_w1[layer],
                                cmp_k_w2[layer], cmp_v_w1[layer], cmp_v_w2[layer])
        a_n = _dsa(aq_t, iq_t, misc_t, akv, misc, ac_t, w_uk[layer], w_uv[layer], gain_a[layer])
        c_n = _nsa(cq_t, cqr_t, misc_t, cmp, vcmp_t, ks, vs_t, kw, vw_t, gain_c[layer])

        x = _out_mlp(x, a_n, b_tm, c_n, w_out[layer], ln2_g[layer], w_up[layer], w_down[layer],
                     final_g, final_norm=(layer == depth - 1))
    return x
```

```python
import functools
import math

import numpy as np
import jax
import jax.numpy as jnp
from jax import lax
from jax.experimental import pallas as pl
from jax.experimental.pallas import tpu as pltpu

F32 = jnp.float32
I32 = jnp.int32
MXU_DTYPE = jnp.bfloat16

D_MODEL = 1024
HEAD_DIM = 64
ROPE_THETA = 500000.0
ROPE_DIM = HEAD_DIM // 4
NORM_EPS = 1e-6
Q_BLOCK = 128
NEG = -1e30
LOG2E = math.log2(math.e)
D_FF = 4 * D_MODEL

A_HEADS = 4
A_NOPE = HEAD_DIM - ROPE_DIM
A_VDIM = HEAD_DIM
A_KV_RANK = 128
IDX_HEADS = 4
IDX_DIM = 32
IDX_ROPE = IDX_DIM // 4
DSA_TOPK = 256

B_WIDTH = 256
S5_GROUP = 16
S5_GROUPS = B_WIDTH // S5_GROUP
S5_STATE = 64
S5_NSTATE = S5_GROUPS * S5_STATE

C_HEADS = 8
C_KV_HEADS = 2
C_GROUP = C_HEADS // C_KV_HEADS
CMP_BLOCK = 32
CMP_STRIDE = 16
SEL_BLOCK = 64
SEL_TOPN = 16
SEL_FORCE = 1e9
WINDOW = 512

IN_SIZES = (A_HEADS * HEAD_DIM, A_KV_RANK, ROPE_DIM, IDX_HEADS * IDX_DIM, IDX_DIM, IDX_HEADS,
            B_WIDTH, C_HEADS * HEAD_DIM, 6 * C_KV_HEADS * HEAD_DIM, 3 * C_HEADS)

LANES = 128
SUBLANES = 8
VMEM_LIMIT = 56 * 1024 * 1024

SEG_AQ, SEG_AKV, SEG_IQ, SEG_MISC, SEG_BU, SEG_CQ, SEG_KVC, SEG_KS, SEG_VS, SEG_KW, SEG_VW = (
    0, 256, 512, 640, 768, 1024, 1536, 1792, 1920, 2048, 2176)
N_PROJ = 2304
MISC_IK, MISC_IW, MISC_GATE = 0, 32, 36
TAB_COS, TAB_SIN, TAB_ICOS, TAB_ISIN, TAB_ONE = 0, 8, 16, 20, 24
TAB_PART = 32

PROJ_TM = 512
KEY_CHUNK = 512
S5_TC = 64
MLP_TM = 1024
MLP_TF = 1024


def _cparams(sem):
    return pltpu.CompilerParams(dimension_semantics=sem, vmem_limit_bytes=VMEM_LIMIT)


def _dot(a, b):
    return jnp.dot(a.astype(MXU_DTYPE), b.astype(MXU_DTYPE), preferred_element_type=F32)


def _rms(x, axis):
    return x * lax.rsqrt(jnp.mean(x * x, axis=axis, keepdims=True) + NORM_EPS)


def _rope_tab_kernel(pos_ref, freq_ref, out_ref):
    ang = pos_ref[...].astype(F32) * freq_ref[...]
    lane = lax.broadcasted_iota(I32, ang.shape, 1)
    l32 = lane % TAB_PART
    is_cos = (l32 < TAB_SIN) | ((l32 >= TAB_ICOS) & (l32 < TAB_ISIN))
    is_sin = ((l32 >= TAB_SIN) & (l32 < TAB_ICOS)) | ((l32 >= TAB_ISIN) & (l32 < TAB_ONE))
    val = jnp.where(is_cos, jnp.cos(ang),
                    jnp.where(is_sin, jnp.sin(ang), jnp.where(l32 == TAB_ONE, 1.0, 0.0)))
    hi = val.astype(jnp.bfloat16).astype(F32)
    r1 = val - hi
    mid = r1.astype(jnp.bfloat16).astype(F32)
    lo = r1 - mid
    part = lane // TAB_PART
    out = jnp.where(part == 0, hi, jnp.where(part == 1, mid, jnp.where(part == 2, lo, 0.0)))
    out_ref[...] = out.astype(jnp.bfloat16)


def _rope_table(positions):
    bsz, seq = positions.shape
    t = bsz * seq
    inv_r = (np.float32(ROPE_THETA) ** (-np.arange(0, ROPE_DIM, 2, dtype=np.float32) / ROPE_DIM))
    inv_i = (np.float32(ROPE_THETA) ** (-np.arange(0, IDX_ROPE, 2, dtype=np.float32) / IDX_ROPE))
    f32 = np.zeros(TAB_PART, np.float32)
    f32[TAB_COS:TAB_COS + 8] = inv_r
    f32[TAB_SIN:TAB_SIN + 8] = inv_r
    f32[TAB_ICOS:TAB_ICOS + 4] = inv_i
    f32[TAB_ISIN:TAB_ISIN + 4] = inv_i
    freq = jnp.asarray(np.tile(f32, LANES // TAB_PART)[None, :])
    tq = min(1024, t)
    return pl.pallas_call(
        _rope_tab_kernel,
        out_shape=jax.ShapeDtypeStruct((t, LANES), jnp.bfloat16),
        grid=(t // tq,),
        in_specs=[pl.BlockSpec((tq, 1), lambda i: (i, 0)),
                  pl.BlockSpec((1, LANES), lambda i: (0, 0))],
        out_specs=pl.BlockSpec((tq, LANES), lambda i: (i, 0)),
        compiler_params=_cparams(("arbitrary",)),
        name="rope_table",
    )(positions.reshape(t, 1), freq).reshape(bsz, seq, LANES)


def _proj_layout():
    offs = np.concatenate([[0], np.cumsum(IN_SIZES)])
    o_aq, o_ckv, o_kr, o_iq, o_ik, o_iw, o_bu, o_cq, o_kv, o_gate = offs[:10]
    src = -np.ones(N_PROJ, np.int64)
    partner = -np.ones(N_PROJ, np.int64)
    clane = np.full(N_PROJ, TAB_ONE, np.int64)
    slane = -np.ones(N_PROJ, np.int64)
    ssign = np.zeros(N_PROJ, np.float32)

    def plain(c0, o0, w):
        src[c0:c0 + w] = np.arange(o0, o0 + w)

    def rope(c0, o0, half, cos_lane, sin_lane):
        for j in range(half):
            partner[c0 + j] = o0 + half + j
            partner[c0 + half + j] = o0 + j
            clane[c0 + j] = clane[c0 + half + j] = cos_lane + j
            slane[c0 + j] = slane[c0 + half + j] = sin_lane + j
            ssign[c0 + j] = -1.0
            ssign[c0 + half + j] = 1.0

    plain(SEG_AQ, o_aq, A_HEADS * HEAD_DIM)
    for h in range(A_HEADS):
        rope(SEG_AQ + h * HEAD_DIM, o_aq + h * HEAD_DIM, ROPE_DIM // 2, TAB_COS, TAB_SIN)
    plain(SEG_AKV, o_ckv, A_KV_RANK)
    plain(SEG_AKV + A_KV_RANK, o_kr, ROPE_DIM)
    rope(SEG_AKV + A_KV_RANK, o_kr, ROPE_DIM // 2, TAB_COS, TAB_SIN)
    plain(SEG_IQ, o_iq, IDX_HEADS * IDX_DIM)
    for h in range(IDX_HEADS):
        rope(SEG_IQ + h * IDX_DIM, o_iq + h * IDX_DIM, IDX_ROPE // 2, TAB_ICOS, TAB_ISIN)
    plain(SEG_MISC + MISC_IK, o_ik, IDX_DIM)
    rope(SEG_MISC + MISC_IK, o_ik, IDX_ROPE // 2, TAB_ICOS, TAB_ISIN)
    plain(SEG_MISC + MISC_IW, o_iw, IDX_HEADS)
    plain(SEG_MISC + MISC_GATE, o_gate, 3 * C_HEADS)
    plain(SEG_BU, o_bu, B_WIDTH)
    plain(SEG_CQ, o_cq, C_HEADS * HEAD_DIM)
    for h in range(C_HEADS):
        rope(SEG_CQ + h * HEAD_DIM, o_cq + h * HEAD_DIM, ROPE_DIM // 2, TAB_COS, TAB_SIN)
    plain(SEG_KVC, o_kv, 6 * C_KV_HEADS * HEAD_DIM)
    for seg, sub in ((SEG_KS, 2), (SEG_KW, 4)):
        for g in range(C_KV_HEADS):
            rope(seg + g * HEAD_DIM, o_kv + sub * C_KV_HEADS * HEAD_DIM + g * HEAD_DIM,
                 ROPE_DIM // 2, TAB_COS, TAB_SIN)

    k = np.arange(LANES)[:, None]
    live = k < 3 * TAB_PART
    ec = (live & ((k % TAB_PART) == clane[None, :])).astype(np.float32)
    es = (live & ((k % TAB_PART) == slane[None, :])).astype(np.float32) * ssign[None, :]
    return src, partner, ec, es


def _in_proj_kernel(x_ref, tab_ref, g_ref, kvg_ref, w_ref, wsw_ref, ec_ref, es_ref,
                    aqT_ref, akv_ref, acT_ref, iqT_ref, misc_ref, miscT_ref, u_ref,
                    cqT_ref, cqrT_ref, kvc_ref, ks_ref, vsT_ref, kw_ref, vwT_ref):
    x = x_ref[...]
    n = (_rms(x, -1) * g_ref[...]).astype(MXU_DTYPE)
    tab = tab_ref[...]

    def plain(c0, w):
        return jnp.dot(n, w_ref[:, c0:c0 + w], preferred_element_type=F32)

    def roped(c0, w):
        p = plain(c0, w)
        psw = jnp.dot(n, wsw_ref[:, c0:c0 + w], preferred_element_type=F32)
        cos = jnp.dot(tab, ec_ref[:, c0:c0 + w], preferred_element_type=F32)
        sin = jnp.dot(tab, es_ref[:, c0:c0 + w], preferred_element_type=F32)
        return p * cos + psw * sin

    def chunked_t(val, ref):
        vt = val.T.astype(ref.dtype)
        for c in range(ref.shape[0]):
            ref[c] = vt[:, c * LANES:(c + 1) * LANES]

    aqT_ref[...] = roped(SEG_AQ, 256).T.astype(aqT_ref.dtype)

    akv = roped(SEG_AKV, 256)
    lat = _rms(akv[:, :A_KV_RANK], -1) * kvg_ref[...]
    akv_ref[...] = jnp.concatenate([lat, akv[:, A_KV_RANK:]], axis=1).astype(akv_ref.dtype)
    chunked_t(lat, acT_ref)

    iqT_ref[...] = roped(SEG_IQ, 128).T.astype(iqT_ref.dtype)
    misc = roped(SEG_MISC, 128)
    misc_ref[...] = misc
    miscT_ref[...] = misc.T
    u_ref[...] = plain(SEG_BU, 256)
    qk_scale = HEAD_DIM ** -0.5 * LOG2E
    cqT_ref[...] = (plain(SEG_CQ, 512) * qk_scale).T.astype(cqT_ref.dtype)
    cqrT_ref[...] = (roped(SEG_CQ, 512) * qk_scale).T.astype(cqrT_ref.dtype)
    kvc_ref[...] = plain(SEG_KVC, 256)
    ks_ref[...] = roped(SEG_KS, 128).astype(ks_ref.dtype)
    chunked_t(plain(SEG_VS, 128), vsT_ref)
    kw_ref[...] = roped(SEG_KW, 128).astype(kw_ref.dtype)
    chunked_t(plain(SEG_VW, 128), vwT_ref)


def _in_proj(x, tab, ln_g, kv_g, w_r, w_sw, ec, es):
    bsz, seq, _ = x.shape
    tm = min(PROJ_TM, seq)
    nj = seq // tm
    nck = tm // LANES
    bf = MXU_DTYPE

    def rows(w):
        return pl.BlockSpec((None, tm, w), lambda b, j: (b, j, 0))

    def cols(w):
        return pl.BlockSpec((None, w, tm), lambda b, j: (b, 0, j))

    def chunks():
        return pl.BlockSpec((None, nck, LANES, LANES), lambda b, j: (b, j, 0, 0))

    def const(shape):
        return pl.BlockSpec(shape, lambda b, j: (0,) * len(shape))

    sds = jax.ShapeDtypeStruct
    out_shape = (
        sds((bsz, 256, seq), bf),
        sds((bsz, seq, 256), bf),
        sds((bsz, seq // LANES, LANES, LANES), bf),
        sds((bsz, 128, seq), bf),
        sds((bsz, seq, 128), F32),
        sds((bsz, 128, seq), F32),
        sds((seq, bsz * B_WIDTH), F32),
        sds((bsz, 512, seq), bf),
        sds((bsz, 512, seq), bf),
        sds((bsz, seq, 256), F32),
        sds((bsz, seq, 128), bf),
        sds((bsz, seq // LANES, LANES, LANES), bf),
        sds((bsz, seq, 128), bf),
        sds((bsz, seq // LANES, LANES, LANES), bf),
    )
    out_specs = (cols(256), rows(256), chunks(), cols(128), rows(128), cols(128),
                 pl.BlockSpec((tm, B_WIDTH), lambda b, j: (j, b)),
                 cols(512), cols(512), rows(256), rows(128), chunks(), rows(128), chunks())
    return pl.pallas_call(
        _in_proj_kernel,
        out_shape=out_shape,
        grid=(bsz, nj),
        in_specs=[rows(D_MODEL), rows(LANES), const((1, D_MODEL)), const((1, A_KV_RANK)),
                  const((D_MODEL, N_PROJ)), const((D_MODEL, N_PROJ)),
                  const((LANES, N_PROJ)), const((LANES, N_PROJ))],
        out_specs=out_specs,
        compiler_params=_cparams(("arbitrary", "arbitrary")),
        name="in_proj",
    )(x, tab, ln_g.reshape(1, -1), kv_g.reshape(1, -1), w_r, w_sw, ec, es)


def _s5_disc_kernel(lr_ref, li_ref, ls_ref, br_ref, bi_ref, ar_ref, ai_ref, bbr_ref, bbi_ref):
    lr, li = lr_ref[...], li_ref[...]
    step = jnp.exp(ls_ref[...])
    mag = jnp.exp(lr * step)
    ar = mag * jnp.cos(li * step)
    ai = mag * jnp.sin(li * step)
    den = lr * lr + li * li
    zr = ((ar - 1.0) * lr + ai * li) / den
    zi = (ai * lr - (ar - 1.0) * li) / den
    br, bi = br_ref[...], bi_ref[...]
    ar_ref[...] = ar
    ai_ref[...] = ai
    bbr_ref[...] = zr * br - zi * bi
    bbi_ref[...] = zr * bi + zi * br


def _s5_discretize(lam_re, lam_im, log_step, b_re, b_im):
    g, p, h = b_re.shape
    ex = lambda a: jnp.repeat(a.astype(F32), h, axis=1)
    ls = jnp.broadcast_to(log_step.astype(F32)[:, None], (g, p * h))
    sds = jax.ShapeDtypeStruct((g, p * h), F32)
    ar, ai, bbr, bbi = pl.pallas_call(
        _s5_disc_kernel, out_shape=(sds, sds, sds, sds), name="s5_discretize",
    )(ex(lam_re), ex(lam_im), ls, b_re.astype(F32).reshape(g, p * h), b_im.astype(F32).reshape(g, p * h))
    ar = ar.reshape(g, p, h)[:, :, 0]
    ai = ai.reshape(g, p, h)[:, :, 0]
    return ar, ai, bbr.reshape(g, p, h), bbi.reshape(g, p, h)


def _s5_kernel(u_ref, bmat_ref, a_ref, cre_ref, cim_ref, d_ref, gw_ref, gb_ref, gain_ref,
               o_ref, x_scr, h_scr, *, tc):
    ns = S5_NSTATE

    @pl.when(pl.program_id(0) == 0)
    def _():
        h_scr[...] = jnp.zeros_like(h_scr)

    u = u_ref[...]
    x_scr[...] = _dot(u, bmat_ref[...])
    ar = a_ref[0:SUBLANES, :]
    ai = a_ref[SUBLANES:2 * SUBLANES, :]

    def step(t, carry):
        hr, hi = carry
        r0 = pl.multiple_of(t * SUBLANES, SUBLANES)
        xr = x_scr[pl.ds(r0, SUBLANES), 0:ns]
        xi = x_scr[pl.ds(r0, SUBLANES), ns:2 * ns]
        nhr = ar * hr - ai * hi + xr
        nhi = ar * hi + ai * hr + xi
        x_scr[pl.ds(r0, SUBLANES), 0:ns] = nhr
        x_scr[pl.ds(r0, SUBLANES), ns:2 * ns] = nhi
        return nhr, nhi

    hr, hi = lax.fori_loop(0, tc, step, (h_scr[:, 0:ns], h_scr[:, ns:2 * ns]), unroll=8)
    h_scr[:, 0:ns] = hr
    h_scr[:, ns:2 * ns] = hi

    y = (_dot(x_scr[:, 0:ns], cre_ref[...]) - _dot(x_scr[:, ns:2 * ns], cim_ref[...])
         + d_ref[...] * u)
    y = jax.nn.gelu(y)
    y = y * jax.nn.sigmoid(_dot(y, gw_ref[...]) + gb_ref[...])
    o_ref[...] = (_rms(y, -1) * gain_ref[...]).astype(o_ref.dtype)


def _s5(u_tm, bsz, ar, ai, bbr, bbi, c_re, c_im, d, glu_w, glu_b, gain):
    assert bsz == SUBLANES, "the S5 scan keeps one batch row per sublane"
    seq = u_tm.shape[0]
    tc = min(S5_TC, seq)
    rows = tc * bsz
    eye = jnp.eye(S5_GROUPS, dtype=F32)
    bmat = jnp.concatenate(
        [jnp.einsum('gph,gk->ghkp', b, eye).reshape(B_WIDTH, S5_NSTATE) for b in (bbr, bbi)], axis=1)
    cre = jnp.einsum('ghp,gk->gpkh', c_re.astype(F32), eye).reshape(S5_NSTATE, B_WIDTH)
    cim = jnp.einsum('ghp,gk->gpkh', c_im.astype(F32), eye).reshape(S5_NSTATE, B_WIDTH)
    avec = jnp.concatenate([jnp.broadcast_to(a.reshape(1, S5_NSTATE), (SUBLANES, S5_NSTATE))
                            for a in (ar, ai)], axis=0)
    const = lambda shape: pl.BlockSpec(shape, lambda i: (0,) * len(shape))
    out = pl.pallas_call(
        functools.partial(_s5_kernel, tc=tc),
        out_shape=jax.ShapeDtypeStruct((seq * bsz, B_WIDTH), MXU_DTYPE),
        grid=(seq // tc,),
        in_specs=[pl.BlockSpec((rows, B_WIDTH), lambda i: (i, 0)),
                  const((B_WIDTH, 2 * S5_NSTATE)), const((2 * SUBLANES, S5_NSTATE)),
                  const((S5_NSTATE, B_WIDTH)), const((S5_NSTATE, B_WIDTH)),
                  const((1, B_WIDTH)), const((B_WIDTH, B_WIDTH)), const((1, B_WIDTH)),
                  const((1, B_WIDTH))],
        out_specs=pl.BlockSpec((rows, B_WIDTH), lambda i: (i, 0)),
        scratch_shapes=[pltpu.VMEM((rows, 2 * S5_NSTATE), F32),
                        pltpu.VMEM((SUBLANES, 2 * S5_NSTATE), F32)],
        compiler_params=_cparams(("arbitrary",)),
        name="s5_scan",
    )(u_tm.reshape(seq * bsz, B_WIDTH), bmat.astype(MXU_DTYPE), avec,
      cre.astype(MXU_DTYPE), cim.astype(MXU_DTYPE), d.astype(F32).reshape(1, B_WIDTH),
      glu_w.astype(MXU_DTYPE), glu_b.astype(F32).reshape(1, B_WIDTH), gain.reshape(1, B_WIDTH))
    return out.reshape(seq, bsz * B_WIDTH)


def _compress_kernel(seg_ref, pa_ref, pb_ref, wa_ref, wb_ref, w2_ref, cmp_ref, vT_ref):
    seg = seg_ref[...]
    xa = _dot(seg + pa_ref[...], wa_ref[...])
    xb = _dot(seg + pb_ref[...], wb_ref[...])
    nseg = seg.shape[0]
    pre = xa + pltpu.roll(xb, nseg - 1, 0)
    out = _dot(jax.nn.gelu(pre), w2_ref[...])
    cmp_ref[...] = out.astype(cmp_ref.dtype)
    vT_ref[...] = out[:, LANES:].T.astype(vT_ref.dtype)


def _compress(kvc, pos_k, pos_v, k_w1, k_w2, v_w1, v_w2):
    bsz, seq, _ = kvc.shape
    nseg = seq // CMP_STRIDE
    width = CMP_STRIDE * 256
    eye = jnp.eye(4, dtype=F32)
    w1 = jnp.stack([k_w1, k_w1, v_w1, v_w1]).astype(F32).reshape(4, CMP_BLOCK, HEAD_DIM, HEAD_DIM)
    pos = jnp.stack([pos_k, pos_k, pos_v, pos_v]).astype(F32)

    def half(lo):
        w = jnp.einsum('slde,st->lsdte', w1[:, lo:lo + CMP_STRIDE], eye).reshape(width, 256)
        p = jnp.transpose(pos[:, lo:lo + CMP_STRIDE], (1, 0, 2)).reshape(1, width)
        return w.astype(MXU_DTYPE), p

    wa, pa = half(0)
    wb, pb = half(CMP_STRIDE)
    w2 = jnp.einsum('sde,st->sdte', jnp.stack([k_w2, k_w2, v_w2, v_w2]).astype(F32), eye).reshape(256, 256)
    const = lambda shape: pl.BlockSpec(shape, lambda b: (0,) * len(shape))
    return pl.pallas_call(
        _compress_kernel,
        out_shape=(jax.ShapeDtypeStruct((bsz, nseg, 256), MXU_DTYPE),
                   jax.ShapeDtypeStruct((bsz, 128, nseg), MXU_DTYPE)),
        grid=(bsz,),
        in_specs=[pl.BlockSpec((None, nseg, width), lambda b: (b, 0, 0)),
                  const((1, width)), const((1, width)), const((width, 256)), const((width, 256)),
                  const((256, 256))],
        out_specs=(pl.BlockSpec((None, nseg, 256), lambda b: (b, 0, 0)),
                   pl.BlockSpec((None, 128, nseg), lambda b: (b, 0, 0))),
        compiler_params=_cparams(("arbitrary",)),
        name="nsa_compress",
    )(kvc.reshape(bsz, nseg, width), pa, pb, wa, wb, w2.astype(MXU_DTYPE))


def _online_softmax_step(s, bias, v_t, m_ref, l_ref, acc_ref):
    s = s + bias
    m_old = m_ref[...]
    m_new = jnp.maximum(m_old, jnp.max(s, axis=0, keepdims=True))
    p = jnp.exp2(s - m_new)
    alpha = jnp.exp2(m_old - m_new)
    l_ref[...] = alpha * l_ref[...] + jnp.sum(p, axis=0, keepdims=True)
    acc_ref[...] = alpha * acc_ref[...] + _dot(v_t, p)
    m_ref[...] = m_new


def _tile4(x):
    return jnp.concatenate([x, x, x, x], axis=1)


INT_MIN = -2 ** 31


def _dsa_kernel(qT_ref, iqT_ref, miscT_ref, kv_ref, misc_ref, cT_ref, mq_ref, wuv_ref, gain_ref,
                o_ref, key_scr, m_scr, l_scr, acc_scr, *, k_top, cut_bits):
    i = pl.program_id(1)
    t0 = i * Q_BLOCK
    kc = KEY_CHUNK
    nch = (t0 + Q_BLOCK + kc - 1) // kc
    t_lane = t0 + lax.broadcasted_iota(I32, (1, Q_BLOCK), 1)
    row_iota = lax.broadcasted_iota(I32, (kc, Q_BLOCK), 0)

    qcat = (_dot(mq_ref[...], qT_ref[...]) * (HEAD_DIM ** -0.5 * LOG2E)).astype(MXU_DTYPE)
    qs_t = jnp.concatenate([qcat[h * 256:(h + 1) * 256] for h in range(A_HEADS)], axis=1)
    iq_t = iqT_ref[...]
    iq_all = jnp.concatenate([iq_t[h * IDX_DIM:(h + 1) * IDX_DIM, :] for h in range(IDX_HEADS)], axis=1)
    w_t = miscT_ref[MISC_IW:MISC_IW + IDX_HEADS, :] * (IDX_HEADS ** -0.5 * IDX_DIM ** -0.5)

    def idx_body(c, _):
        k0 = pl.multiple_of(c * kc, kc)
        ik = misc_ref[pl.ds(k0, kc), MISC_IK:MISC_IK + IDX_DIM]
        d = _dot(ik, iq_all)
        score = jnp.zeros((kc, Q_BLOCK), F32)
        for h in range(IDX_HEADS):
            score = score + jnp.maximum(d[:, h * Q_BLOCK:(h + 1) * Q_BLOCK], 0.0) * w_t[h:h + 1, :]
        score = jnp.where(row_iota + k0 <= t_lane, score, -jnp.inf)
        bits = pltpu.bitcast(score, I32)
        key_scr[pl.ds(k0, kc), :] = bits ^ ((bits >> 31) & 0x7FFFFFFF)
        return 0

    lax.fori_loop(0, nch, idx_body, 0)

    def count(pred):
        def body(c, acc):
            k0 = pl.multiple_of(c * kc, kc)
            hit = pred(key_scr[pl.ds(k0, kc), :], row_iota + k0)
            return acc + jnp.sum(hit.reshape(kc // SUBLANES, SUBLANES, Q_BLOCK), axis=0)
        acc = lax.fori_loop(0, nch, body, jnp.zeros((SUBLANES, Q_BLOCK), I32))
        return jnp.sum(acc, axis=0, keepdims=True)

    def thr_bit(b, prefix):
        cand = prefix | lax.shift_left(jnp.int32(1), 31 - b)
        cand_s = cand ^ INT_MIN
        cnt = count(lambda keys, idx: jnp.where(keys >= cand_s, 1, 0))
        return jnp.where(cnt >= k_top, cand, prefix)

    thr = lax.fori_loop(0, 32, thr_bit, jnp.zeros((1, Q_BLOCK), I32)) ^ INT_MIN
    n_gt = count(lambda keys, idx: jnp.where(keys > thr, 1, 0))
    n_eq = count(lambda keys, idx: jnp.where(keys == thr, 1, 0))
    need = k_top - n_gt

    def cut_bit(b, prefix):
        cand = prefix | lax.shift_left(jnp.int32(1), cut_bits - 1 - b)
        cnt = count(lambda keys, idx: jnp.where(keys == thr, jnp.where(idx < cand, 1, 0), 0))
        return jnp.where(cnt <= need, cand, prefix)

    tied = jnp.max(jnp.where(n_eq > need, 1, 0))
    cut = lax.cond(tied > 0,
                   lambda: lax.fori_loop(0, cut_bits, cut_bit, jnp.zeros((1, Q_BLOCK), I32)),
                   lambda: jnp.full((1, Q_BLOCK), 2 ** cut_bits, I32))

    m_scr[...] = jnp.full_like(m_scr, NEG)
    l_scr[...] = jnp.zeros_like(l_scr)
    acc_scr[...] = jnp.zeros_like(acc_scr)

    cut = jnp.minimum(cut, t_lane + 1)

    def att_body(c, _):
        k0 = pl.multiple_of(c * kc, kc)
        s = _dot(kv_ref[pl.ds(k0, kc), :], qs_t)
        keys = key_scr[pl.ds(k0, kc), :]
        bias = jnp.where(keys > thr, 0.0,
                         jnp.where(keys == thr, jnp.where(row_iota + k0 < cut, 0.0, NEG), NEG))
        v_t = jnp.concatenate([cT_ref[c * (kc // LANES) + j] for j in range(kc // LANES)], axis=1)
        _online_softmax_step(s, _tile4(bias), v_t, m_scr, l_scr, acc_scr)
        return 0

    lax.fori_loop(0, nch, att_body, 0)

    o_lat = (acc_scr[...] / l_scr[...]).astype(MXU_DTYPE)
    out_t = jnp.concatenate(
        [_dot(wuv_ref[h], o_lat[:, h * Q_BLOCK:(h + 1) * Q_BLOCK]) for h in range(A_HEADS)], axis=0)
    out_t = _rms(out_t, 0) * gain_ref[...]
    o_ref[...] = out_t.T.astype(o_ref.dtype)


def _dsa(aq_t, iq_t, misc_t, akv, misc, ac_t, w_uk, w_uv, gain):
    bsz, _, seq = aq_t.shape
    nq = seq // Q_BLOCK
    k_top = min(DSA_TOPK, seq // 4)
    seq_pad = -(-seq // KEY_CHUNK) * KEY_CHUNK
    mq = jnp.zeros((A_HEADS, 256, A_HEADS, HEAD_DIM), F32)
    for h in range(A_HEADS):
        mq = mq.at[h, :A_KV_RANK, h, ROPE_DIM:].set(w_uk[:, h, :].astype(F32))
        mq = mq.at[h, A_KV_RANK:A_KV_RANK + ROPE_DIM, h, :ROPE_DIM].set(jnp.eye(ROPE_DIM, dtype=F32))
    mq = mq.reshape(A_HEADS * 256, A_HEADS * HEAD_DIM).astype(MXU_DTYPE)
    wuv_t = jnp.transpose(w_uv, (1, 2, 0)).astype(MXU_DTYPE)
    per_q = lambda w: pl.BlockSpec((None, w, Q_BLOCK), lambda b, i: (b, 0, i))
    per_b = lambda *s: pl.BlockSpec((None,) + s, lambda b, i: (b,) + (0,) * len(s))
    const = lambda shape: pl.BlockSpec(shape, lambda b, i: (0,) * len(shape))
    return pl.pallas_call(
        functools.partial(_dsa_kernel, k_top=k_top, cut_bits=int(seq_pad).bit_length()),
        out_shape=jax.ShapeDtypeStruct((bsz, seq, A_HEADS * A_VDIM), MXU_DTYPE),
        grid=(bsz, nq),
        in_specs=[per_q(256), per_q(128), per_q(128), per_b(seq, 256), per_b(seq, 128),
                  per_b(seq // LANES, LANES, LANES), const(mq.shape), const(wuv_t.shape),
                  const((A_HEADS * A_VDIM, 1))],
        out_specs=pl.BlockSpec((None, Q_BLOCK, A_HEADS * A_VDIM), lambda b, i: (b, i, 0)),
        scratch_shapes=[pltpu.VMEM((seq_pad, Q_BLOCK), I32),
                        pltpu.VMEM((1, A_HEADS * Q_BLOCK), F32),
                        pltpu.VMEM((1, A_HEADS * Q_BLOCK), F32),
                        pltpu.VMEM((A_KV_RANK, A_HEADS * Q_BLOCK), F32)],
        compiler_params=_cparams(("arbitrary", "arbitrary")),
        name="dsa_attention",
    )(aq_t, iq_t, misc_t, akv, misc, ac_t, mq, wuv_t, gain.reshape(-1, 1))


def _nsa_kernel(qT_ref, qrT_ref, miscT_ref, cmp_ref, vcT_ref, ks_ref, vsT_ref, kw_ref, vwT_ref,
                ov_ref, gain_ref, o_ref, sel_scr, m_scr, l_scr, acc_scr, *, n_top, n_blk):
    i = pl.program_id(1)
    t0 = i * Q_BLOCK
    kc = KEY_CHUNK
    nch = (t0 + Q_BLOCK + kc - 1) // kc
    t_lane = t0 + lax.broadcasted_iota(I32, (1, Q_BLOCK), 1)
    gates = jax.nn.sigmoid(miscT_ref[MISC_GATE:MISC_GATE + 3 * C_HEADS, :])
    n_cmp = cmp_ref.shape[0]
    cmp_iota = lax.broadcasted_iota(I32, (n_cmp, Q_BLOCK), 0)
    blk_iota = lax.broadcasted_iota(I32, (n_blk, Q_BLOCK), 0)
    row_iota = lax.broadcasted_iota(I32, (kc, Q_BLOCK), 0)
    win_iota = lax.broadcasted_iota(I32, (Q_BLOCK, Q_BLOCK), 0)
    slabs = []

    for g in range(C_KV_HEADS):
        heads = range(g * C_GROUP, (g + 1) * C_GROUP)
        q_t = jnp.concatenate([qT_ref[h * HEAD_DIM:(h + 1) * HEAD_DIM, :] for h in heads], axis=1)
        qr_t = jnp.concatenate([qrT_ref[h * HEAD_DIM:(h + 1) * HEAD_DIM, :] for h in heads], axis=1)
        gsl = slice(g * HEAD_DIM, (g + 1) * HEAD_DIM)

        s_c = _dot(cmp_ref[:, gsl], q_t)
        vis = _tile4(jnp.where(cmp_iota * CMP_STRIDE + (CMP_BLOCK - 1) <= t_lane, 1, 0)) > 0
        s_c = jnp.where(vis, s_c, NEG)
        p_c = jnp.where(vis, jnp.exp2(s_c - jnp.max(s_c, axis=0, keepdims=True)), 0.0)
        l_c = jnp.sum(p_c, axis=0, keepdims=True)
        p_c = p_c * (1.0 / jnp.maximum(l_c, 1e-30))
        o_c = _dot(vcT_ref[gsl, :], p_c)

        p_sum = (p_c[:, 0:Q_BLOCK] + p_c[:, Q_BLOCK:2 * Q_BLOCK]
                 + p_c[:, 2 * Q_BLOCK:3 * Q_BLOCK] + p_c[:, 3 * Q_BLOCK:4 * Q_BLOCK])
        p_hi = p_sum.astype(MXU_DTYPE)
        p_lo = p_sum - p_hi.astype(F32)
        imp = _dot(ov_ref[...], p_hi) + _dot(ov_ref[...], p_lo)
        cur = t_lane // SEL_BLOCK
        forced = jnp.where(blk_iota == 0, 1, jnp.where(blk_iota == cur, 1,
                           jnp.where(blk_iota == cur - 1, 1, 0)))
        imp = jnp.where(forced > 0, SEL_FORCE, imp)
        imp = jnp.where(blk_iota * SEL_BLOCK <= t_lane, imp, -jnp.inf)
        rank = jnp.zeros((n_blk, Q_BLOCK), I32)
        for mp in range(n_blk):
            row = imp[mp:mp + 1, :]
            rank = rank + jnp.where(row > imp, 1,
                                    jnp.where(row == imp, jnp.where(blk_iota > mp, 1, 0), 0))
        sel_scr[...] = jnp.where(rank < n_top, 0.0, NEG)

        m_scr[...] = jnp.full_like(m_scr, NEG)
        l_scr[...] = jnp.zeros_like(l_scr)
        acc_scr[...] = jnp.zeros_like(acc_scr)
        bpc = kc // SEL_BLOCK

        def sel_body(c, _):
            k0 = pl.multiple_of(c * kc, kc)
            s = _dot(ks_ref[pl.ds(k0, kc), gsl], qr_t)
            sel8 = sel_scr[pl.ds(pl.multiple_of(c * bpc, bpc), bpc), :]
            bias = jnp.concatenate(
                [jnp.broadcast_to(sel8[j:j + 1, :], (SEL_BLOCK, Q_BLOCK)) for j in range(bpc)], axis=0)
            bias = jnp.where(row_iota + k0 <= t_lane, bias, NEG)
            v_t = jnp.concatenate([vsT_ref[c * (kc // LANES) + j][gsl, :]
                                   for j in range(kc // LANES)], axis=1)
            _online_softmax_step(s, _tile4(bias), v_t, m_scr, l_scr, acc_scr)
            return 0

        lax.fori_loop(0, nch, sel_body, 0)
        o_s = acc_scr[...] / l_scr[...]

        k_parts, v_parts, m_parts = [], [], []
        for j in range(WINDOW // Q_BLOCK + 1):
            cj = i - WINDOW // Q_BLOCK + j
            cjc = jnp.maximum(cj, 0)
            k0 = pl.multiple_of(cjc * Q_BLOCK, Q_BLOCK)
            k_parts.append(kw_ref[pl.ds(k0, Q_BLOCK), gsl])
            v_parts.append(vwT_ref[cjc][gsl, :])
            kidx = win_iota + k0
            inside = jnp.where(kidx <= t_lane, jnp.where(kidx > t_lane - WINDOW, 0.0, NEG), NEG)
            m_parts.append(jnp.where(cj >= 0, inside, NEG))
        s_w = _dot(jnp.concatenate(k_parts, axis=0), qr_t) + _tile4(jnp.concatenate(m_parts, axis=0))
        p_w = jnp.exp2(s_w - jnp.max(s_w, axis=0, keepdims=True))
        l_w = jnp.sum(p_w, axis=0, keepdims=True)
        o_w = _dot(jnp.concatenate(v_parts, axis=1), p_w) / l_w

        for hh in range(C_GROUP):
            sl = slice(hh * Q_BLOCK, (hh + 1) * Q_BLOCK)
            r = (g * C_GROUP + hh) * 3
            slabs.append(gates[r:r + 1, :] * o_c[:, sl] + gates[r + 1:r + 2, :] * o_s[:, sl]
                         + gates[r + 2:r + 3, :] * o_w[:, sl])

    out_t = jnp.concatenate(slabs, axis=0)
    out_t = _rms(out_t, 0) * gain_ref[...]
    o_ref[...] = out_t.T.astype(o_ref.dtype)


def _sel_overlap_t(n_cmp_rows, n_blk):
    cs = np.arange(n_cmp_rows)[None, :] * CMP_STRIDE
    ss = np.arange(n_blk)[:, None] * SEL_BLOCK
    ov = np.minimum(cs + CMP_BLOCK, ss + SEL_BLOCK) - np.maximum(cs, ss)
    return np.clip(ov, 0, None).astype(np.float32) / CMP_BLOCK


def _nsa(cq_t, cqr_t, misc_t, cmp, vcmp_t, ks, vs_t, kw, vw_t, gain):
    bsz, _, seq = cq_t.shape
    nq = seq // Q_BLOCK
    nseg = cmp.shape[1]
    n_blk = seq // SEL_BLOCK
    n_top = min(SEL_TOPN, n_blk)
    seq_pad = -(-seq // KEY_CHUNK) * KEY_CHUNK
    ov_np = _sel_overlap_t(nseg, n_blk)
    ov_np[:, (seq - CMP_BLOCK) // CMP_STRIDE + 1:] = 0.0
    ov = jnp.asarray(ov_np, MXU_DTYPE)
    per_q = lambda w: pl.BlockSpec((None, w, Q_BLOCK), lambda b, i: (b, 0, i))
    per_b = lambda *s: pl.BlockSpec((None,) + s, lambda b, i: (b,) + (0,) * len(s))
    const = lambda shape: pl.BlockSpec(shape, lambda b, i: (0,) * len(shape))
    return pl.pallas_call(
        functools.partial(_nsa_kernel, n_top=n_top, n_blk=n_blk),
        out_shape=jax.ShapeDtypeStruct((bsz, seq, C_HEADS * HEAD_DIM), MXU_DTYPE),
        grid=(bsz, nq),
        in_specs=[per_q(512), per_q(512), per_q(128), per_b(nseg, 256), per_b(128, nseg),
                  per_b(seq, 128), per_b(seq // LANES, LANES, LANES),
                  per_b(seq, 128), per_b(seq // LANES, LANES, LANES),
                  const(ov.shape), const((C_HEADS * HEAD_DIM, 1))],
        out_specs=pl.BlockSpec((None, Q_BLOCK, C_HEADS * HEAD_DIM), lambda b, i: (b, i, 0)),
        scratch_shapes=[pltpu.VMEM((max(n_blk, seq_pad // SEL_BLOCK), Q_BLOCK), F32),
                        pltpu.VMEM((1, C_GROUP * Q_BLOCK), F32),
                        pltpu.VMEM((1, C_GROUP * Q_BLOCK), F32),
                        pltpu.VMEM((HEAD_DIM, C_GROUP * Q_BLOCK), F32)],
        compiler_params=_cparams(("arbitrary", "arbitrary")),
        name="nsa_attention",
    )(cq_t, cqr_t, misc_t, cmp, vcmp_t, ks, vs_t, kw, vw_t, ov, gain.reshape(-1, 1))


def _out_mlp_kernel(x_ref, a_ref, b_ref, c_ref, wo_ref, g2_ref, wu_ref, wd_ref, fg_ref,
                    o_ref, x1_scr, n2_scr, acc_scr, *, final_norm):
    j = pl.program_id(1)

    @pl.when(j == 0)
    def _():
        mixed = (jnp.dot(a_ref[...], wo_ref[0:256, :], preferred_element_type=F32)
                 + jnp.dot(b_ref[...], wo_ref[256:512, :], preferred_element_type=F32)
                 + jnp.dot(c_ref[...], wo_ref[512:1024, :], preferred_element_type=F32))
        x1 = x_ref[...] + mixed
        x1_scr[...] = x1
        n2_scr[...] = (_rms(x1, -1) * g2_ref[...]).astype(n2_scr.dtype)
        acc_scr[...] = jnp.zeros_like(acc_scr)

    z = jnp.dot(n2_scr[...], wu_ref[...], preferred_element_type=F32)
    acc_scr[...] += _dot(jnp.square(jnp.maximum(z, 0.0)), wd_ref[...])

    @pl.when(j == pl.num_programs(1) - 1)
    def _():
        x2 = x1_scr[...] + acc_scr[...]
        if final_norm:
            x2 = _rms(x2, -1) * fg_ref[...]
        o_ref[...] = x2


def _out_mlp(x, a_n, b_tm, c_n, w_out, ln2_g, w_up, w_down, final_g, final_norm):
    bsz, seq, _ = x.shape
    tm = min(MLP_TM, seq)
    nj = seq // tm
    nf = D_FF // MLP_TF
    rows = lambda w: pl.BlockSpec((None, tm, w), lambda r, f: (r // nj, r % nj, 0))
    const = lambda shape: pl.BlockSpec(shape, lambda r, f: (0,) * len(shape))
    return pl.pallas_call(
        functools.partial(_out_mlp_kernel, final_norm=final_norm),
        out_shape=jax.ShapeDtypeStruct(x.shape, F32),
        grid=(bsz * nj, nf),
        in_specs=[rows(D_MODEL), rows(256),
                  pl.BlockSpec((tm, B_WIDTH), lambda r, f: (r % nj, r // nj)),
                  rows(512), const((D_MODEL, D_MODEL)), const((1, D_MODEL)),
                  pl.BlockSpec((D_MODEL, MLP_TF), lambda r, f: (0, f)),
                  pl.BlockSpec((MLP_TF, D_MODEL), lambda r, f: (f, 0)),
                  const((1, D_MODEL))],
        out_specs=rows(D_MODEL),
        scratch_shapes=[pltpu.VMEM((tm, D_MODEL), F32), pltpu.VMEM((tm, D_MODEL), MXU_DTYPE),
                        pltpu.VMEM((tm, D_MODEL), F32)],
        compiler_params=_cparams(("arbitrary", "arbitrary")),
        name="out_proj_mlp",
    )(x, a_n, b_tm, c_n, w_out.astype(MXU_DTYPE), ln2_g.reshape(1, -1), w_up.astype(MXU_DTYPE),
      w_down.astype(MXU_DTYPE), final_g.reshape(1, -1))


def kernel(x, positions, ln1_g, w_in, kv_norm_g, w_uk, w_uv, s5_lambda_re, s5_lambda_im, s5_log_step, s5_b_re, s5_b_im, s5_c_re, s5_c_im, s5_d, s5_glu_w, s5_glu_b, cmp_pos_k, cmp_pos_v, cmp_k_w1, cmp_k_w2, cmp_v_w1, cmp_v_w2, gain_a, gain_b, gain_c, w_out, ln2_g, w_up, w_down, final_g):
    bsz, seq, _ = x.shape
    depth = w_in.shape[0]
    src, partner, ec_np, es_np = _proj_layout()
    ec = jnp.asarray(ec_np, MXU_DTYPE)
    es = jnp.asarray(es_np, MXU_DTYPE)
    tab = _rope_table(positions)

    def regather(w, idx):
        cols = jnp.take(w, jnp.asarray(np.maximum(idx, 0)), axis=1)
        return jnp.where(jnp.asarray(idx >= 0)[None, :], cols, 0.0).astype(MXU_DTYPE)

    for layer in range(depth):
        w_r = regather(w_in[layer], src)
        w_sw = regather(w_in[layer], partner)
        (aq_t, akv, ac_t, iq_t, misc, misc_t, u_tm, cq_t, cqr_t, kvc, ks, vs_t, kw, vw_t) = _in_proj(
            x, tab, ln1_g[layer], kv_norm_g[layer], w_r, w_sw, ec, es)

        ar, ai, bbr, bbi = _s5_discretize(s5_lambda_re[layer], s5_lambda_im[layer], s5_log_step[layer],
                                          s5_b_re[layer], s5_b_im[layer])
        b_tm = _s5(u_tm, bsz, ar, ai, bbr, bbi, s5_c_re[layer], s5_c_im[layer], s5_d[layer].reshape(-1),
                   s5_glu_w[layer], s5_glu_b[layer], gain_b[layer])

        cmp, vcmp_t = _compress(kvc, cmp_pos_k[layer], cmp_pos_v[layer], cmp_k_w1[layer],
                                cmp_k_w2[layer], cmp_v_w1[layer], cmp_v_w2[layer])
        a_n = _dsa(aq_t, iq_t, misc_t, akv, misc, ac_t, w_uk[layer], w_uv[layer], gain_a[layer])
        c_n = _nsa(cq_t, cqr_t, misc_t, cmp, vcmp_t, ks, vs_t, kw, vw_t, gain_c[layer])

        x = _out_mlp(x, a_n, b_tm, c_n, w_out[layer], ln2_g[layer], w_up[layer], w_down[layer],
                     final_g, final_norm=(layer == depth - 1))
    return x
```

```python
import functools
import math

import numpy as np
import jax
import jax.numpy as jnp
from jax import lax
from jax.experimental import pallas as pl
from jax.experimental.pallas import tpu as pltpu

F32 = jnp.float32
I32 = jnp.int32
MXU_DTYPE = jnp.bfloat16

D_MODEL = 1024
HEAD_DIM = 64
ROPE_THETA = 500000.0
ROPE_DIM = HEAD_DIM // 4
NORM_EPS = 1e-6
Q_BLOCK = 128
NEG = -1e30
LOG2E = math.log2(math.e)
D_FF = 4 * D_MODEL

A_HEADS = 4
A_NOPE = HEAD_DIM - ROPE_DIM
A_VDIM = HEAD_DIM
A_KV_RANK = 128
IDX_HEADS = 4
IDX_DIM = 32
IDX_ROPE = IDX_DIM // 4
DSA_TOPK = 256

B_WIDTH = 256
S5_GROUP = 16
S5_GROUPS = B_WIDTH // S5_GROUP
S5_STATE = 64
S5_NSTATE = S5_GROUPS * S5_STATE

C_HEADS = 8
C_KV_HEADS = 2
C_GROUP = C_HEADS // C_KV_HEADS
CMP_BLOCK = 32
CMP_STRIDE = 16
SEL_BLOCK = 64
SEL_TOPN = 16
SEL_FORCE = 1e9
WINDOW = 512

IN_SIZES = (A_HEADS * HEAD_DIM, A_KV_RANK, ROPE_DIM, IDX_HEADS * IDX_DIM, IDX_DIM, IDX_HEADS,
            B_WIDTH, C_HEADS * HEAD_DIM, 6 * C_KV_HEADS * HEAD_DIM, 3 * C_HEADS)

LANES = 128
SUBLANES = 8
VMEM_LIMIT = 56 * 1024 * 1024

SEG_AQ, SEG_AKV, SEG_IQ, SEG_MISC, SEG_BU, SEG_CQ, SEG_KVC, SEG_KS, SEG_VS, SEG_KW, SEG_VW = (
    0, 256, 512, 640, 768, 1024, 1536, 1792, 1920, 2048, 2176)
N_PROJ = 2304
MISC_IK, MISC_IW, MISC_GATE = 0, 32, 36
TAB_COS, TAB_SIN, TAB_ICOS, TAB_ISIN, TAB_ONE = 0, 8, 16, 20, 24
TAB_PART = 32

PROJ_TM = 512
KEY_CHUNK = 512
S5_TC = 64
MLP_TM = 1024
MLP_TF = 1024


def _cparams(sem):
    return pltpu.CompilerParams(dimension_semantics=sem, vmem_limit_bytes=VMEM_LIMIT)


def _dot(a, b):
    return jnp.dot(a.astype(MXU_DTYPE), b.astype(MXU_DTYPE), preferred_element_type=F32)


def _rms(x, axis):
    return x * lax.rsqrt(jnp.mean(x * x, axis=axis, keepdims=True) + NORM_EPS)


def _rope_tab_kernel(pos_ref, freq_ref, out_ref):
    ang = pos_ref[...].astype(F32) * freq_ref[...]
    lane = lax.broadcasted_iota(I32, ang.shape, 1)
    l32 = lane % TAB_PART
    is_cos = (l32 < TAB_SIN) | ((l32 >= TAB_ICOS) & (l32 < TAB_ISIN))
    is_sin = ((l32 >= TAB_SIN) & (l32 < TAB_ICOS)) | ((l32 >= TAB_ISIN) & (l32 < TAB_ONE))
    val = jnp.where(is_cos, jnp.cos(ang),
                    jnp.where(is_sin, jnp.sin(ang), jnp.where(l32 == TAB_ONE, 1.0, 0.0)))
    hi = val.astype(jnp.bfloat16).astype(F32)
    r1 = val - hi
    mid = r1.astype(jnp.bfloat16).astype(F32)
    lo = r1 - mid
    part = lane // TAB_PART
    out = jnp.where(part == 0, hi, jnp.where(part == 1, mid, jnp.where(part == 2, lo, 0.0)))
    out_ref[...] = out.astype(jnp.bfloat16)


def _rope_table(positions):
    bsz, seq = positions.shape
    t = bsz * seq
    inv_r = (np.float32(ROPE_THETA) ** (-np.arange(0, ROPE_DIM, 2, dtype=np.float32) / ROPE_DIM))
    inv_i = (np.float32(ROPE_THETA) ** (-np.arange(0, IDX_ROPE, 2, dtype=np.float32) / IDX_ROPE))
    f32 = np.zeros(TAB_PART, np.float32)
    f32[TAB_COS:TAB_COS + 8] = inv_r
    f32[TAB_SIN:TAB_SIN + 8] = inv_r
    f32[TAB_ICOS:TAB_ICOS + 4] = inv_i
    f32[TAB_ISIN:TAB_ISIN + 4] = inv_i
    freq = jnp.asarray(np.tile(f32, LANES // TAB_PART)[None, :])
    tq = min(1024, t)
    return pl.pallas_call(
        _rope_tab_kernel,
        out_shape=jax.ShapeDtypeStruct((t, LANES), jnp.bfloat16),
        grid=(t // tq,),
        in_specs=[pl.BlockSpec((tq, 1), lambda i: (i, 0)),
                  pl.BlockSpec((1, LANES), lambda i: (0, 0))],
        out_specs=pl.BlockSpec((tq, LANES), lambda i: (i, 0)),
        compiler_params=_cparams(("arbitrary",)),
        name="rope_table",
    )(positions.reshape(t, 1), freq).reshape(bsz, seq, LANES)


def _proj_layout():
    offs = np.concatenate([[0], np.cumsum(IN_SIZES)])
    o_aq, o_ckv, o_kr, o_iq, o_ik, o_iw, o_bu, o_cq, o_kv, o_gate = offs[:10]
    src = -np.ones(N_PROJ, np.int64)
    partner = -np.ones(N_PROJ, np.int64)
    clane = np.full(N_PROJ, TAB_ONE, np.int64)
    slane = -np.ones(N_PROJ, np.int64)
    ssign = np.zeros(N_PROJ, np.float32)

    def plain(c0, o0, w):
        src[c0:c0 + w] = np.arange(o0, o0 + w)

    def rope(c0, o0, half, cos_lane, sin_lane):
        for j in range(half):
            partner[c0 + j] = o0 + half + j
            partner[c0 + half + j] = o0 + j
            clane[c0 + j] = clane[c0 + half + j] = cos_lane + j
            slane[c0 + j] = slane[c0 + half + j] = sin_lane + j
            ssign[c0 + j] = -1.0
            ssign[c0 + half + j] = 1.0

    plain(SEG_AQ, o_aq, A_HEADS * HEAD_DIM)
    for h in range(A_HEADS):
        rope(SEG_AQ + h * HEAD_DIM, o_aq + h * HEAD_DIM, ROPE_DIM // 2, TAB_COS, TAB_SIN)
    plain(SEG_AKV, o_ckv, A_KV_RANK)
    plain(SEG_AKV + A_KV_RANK, o_kr, ROPE_DIM)
    rope(SEG_AKV + A_KV_RANK, o_kr, ROPE_DIM // 2, TAB_COS, TAB_SIN)
    plain(SEG_IQ, o_iq, IDX_HEADS * IDX_DIM)
    for h in range(IDX_HEADS):
        rope(SEG_IQ + h * IDX_DIM, o_iq + h * IDX_DIM, IDX_ROPE // 2, TAB_ICOS, TAB_ISIN)
    plain(SEG_MISC + MISC_IK, o_ik, IDX_DIM)
    rope(SEG_MISC + MISC_IK, o_ik, IDX_ROPE // 2, TAB_ICOS, TAB_ISIN)
    plain(SEG_MISC + MISC_IW, o_iw, IDX_HEADS)
    plain(SEG_MISC + MISC_GATE, o_gate, 3 * C_HEADS)
    plain(SEG_BU, o_bu, B_WIDTH)
    plain(SEG_CQ, o_cq, C_HEADS * HEAD_DIM)
    for h in range(C_HEADS):
        rope(SEG_CQ + h * HEAD_DIM, o_cq + h * HEAD_DIM, ROPE_DIM // 2, TAB_COS, TAB_SIN)
    plain(SEG_KVC, o_kv, 6 * C_KV_HEADS * HEAD_DIM)
    for seg, sub in ((SEG_KS, 2), (SEG_KW, 4)):
        for g in range(C_KV_HEADS):
            rope(seg + g * HEAD_DIM, o_kv + sub * C_KV_HEADS * HEAD_DIM + g * HEAD_DIM,
                 ROPE_DIM // 2, TAB_COS, TAB_SIN)

    k = np.arange(LANES)[:, None]
    live = k < 3 * TAB_PART
    ec = (live & ((k % TAB_PART) == clane[None, :])).astype(np.float32)
    es = (live & ((k % TAB_PART) == slane[None, :])).astype(np.float32) * ssign[None, :]
    return src, partner, ec, es


def _in_proj_kernel(x_ref, tab_ref, g_ref, kvg_ref, w_ref, wsw_ref, ec_ref, es_ref,
                    aqT_ref, akv_ref, acT_ref, iqT_ref, misc_ref, miscT_ref, u_ref,
                    cqT_ref, cqrT_ref, kvc_ref, ks_ref, vsT_ref, kw_ref, vwT_ref):
    x = x_ref[...]
    n = (_rms(x, -1) * g_ref[...]).astype(MXU_DTYPE)
    tab = tab_ref[...]

    def plain(c0, w):
        return jnp.dot(n, w_ref[:, c0:c0 + w], preferred_element_type=F32)

    def roped(c0, w):
        p = plain(c0, w)
        psw = jnp.dot(n, wsw_ref[:, c0:c0 + w], preferred_element_type=F32)
        cos = jnp.dot(tab, ec_ref[:, c0:c0 + w], preferred_element_type=F32)
        sin = jnp.dot(tab, es_ref[:, c0:c0 + w], preferred_element_type=F32)
        return p * cos + psw * sin

    def chunked_t(val, ref):
        vt = val.T.astype(ref.dtype)
        for c in range(ref.shape[0]):
            ref[c] = vt[:, c * LANES:(c + 1) * LANES]

    aqT_ref[...] = roped(SEG_AQ, 256).T.astype(aqT_ref.dtype)

    akv = roped(SEG_AKV, 256)
    lat = _rms(akv[:, :A_KV_RANK], -1) * kvg_ref[...]
    akv_ref[...] = jnp.concatenate([lat, akv[:, A_KV_RANK:]], axis=1).astype(akv_ref.dtype)
    chunked_t(lat, acT_ref)

    iqT_ref[...] = roped(SEG_IQ, 128).T.astype(iqT_ref.dtype)
    misc = roped(SEG_MISC, 128)
    misc_ref[...] = misc
    miscT_ref[...] = misc.T
    u_ref[...] = plain(SEG_BU, 256)
    qk_scale = HEAD_DIM ** -0.5 * LOG2E
    cqT_ref[...] = (plain(SEG_CQ, 512) * qk_scale).T.astype(cqT_ref.dtype)
    cqrT_ref[...] = (roped(SEG_CQ, 512) * qk_scale).T.astype(cqrT_ref.dtype)
    kvc_ref[...] = plain(SEG_KVC, 256)
    ks_ref[...] = roped(SEG_KS, 128).astype(ks_ref.dtype)
    chunked_t(plain(SEG_VS, 128), vsT_ref)
    kw_ref[...] = roped(SEG_KW, 128).astype(kw_ref.dtype)
    chunked_t(plain(SEG_VW, 128), vwT_ref)


def _in_proj(x, tab, ln_g, kv_g, w_r, w_sw, ec, es):
    bsz, seq, _ = x.shape
    tm = min(PROJ_TM, seq)
    nj = seq // tm
    nck = tm // LANES
    bf = MXU_DTYPE

    def rows(w):
        return pl.BlockSpec((None, tm, w), lambda b, j: (b, j, 0))

    def cols(w):
        return pl.BlockSpec((None, w, tm), lambda b, j: (b, 0, j))

    def chunks():
        return pl.BlockSpec((None, nck, LANES, LANES), lambda b, j: (b, j, 0, 0))

    def const(shape):
        return pl.BlockSpec(shape, lambda b, j: (0,) * len(shape))

    sds = jax.ShapeDtypeStruct
    out_shape = (
        sds((bsz, 256, seq), bf),
        sds((bsz, seq, 256), bf),
        sds((bsz, seq // LANES, LANES, LANES), bf),
        sds((bsz, 128, seq), bf),
        sds((bsz, seq, 128), F32),
        sds((bsz, 128, seq), F32),
        sds((seq, bsz * B_WIDTH), F32),
        sds((bsz, 512, seq), bf),
        sds((bsz, 512, seq), bf),
        sds((bsz, seq, 256), F32),
        sds((bsz, seq, 128), bf),
        sds((bsz, seq // LANES, LANES, LANES), bf),
        sds((bsz, seq, 128), bf),
        sds((bsz, seq // LANES, LANES, LANES), bf),
    )
    out_specs = (cols(256), rows(256), chunks(), cols(128), rows(128), cols(128),
                 pl.BlockSpec((tm, B_WIDTH), lambda b, j: (j, b)),
                 cols(512), cols(512), rows(256), rows(128), chunks(), rows(128), chunks())
    return pl.pallas_call(
        _in_proj_kernel,
        out_shape=out_shape,
        grid=(bsz, nj),
        in_specs=[rows(D_MODEL), rows(LANES), const((1, D_MODEL)), const((1, A_KV_RANK)),
                  const((D_MODEL, N_PROJ)), const((D_MODEL, N_PROJ)),
                  const((LANES, N_PROJ)), const((LANES, N_PROJ))],
        out_specs=out_specs,
        compiler_params=_cparams(("arbitrary", "arbitrary")),
        name="in_proj",
    )(x, tab, ln_g.reshape(1, -1), kv_g.reshape(1, -1), w_r, w_sw, ec, es)


def _s5_disc_kernel(lr_ref, li_ref, ls_ref, br_ref, bi_ref, ar_ref, ai_ref, bbr_ref, bbi_ref):
    lr, li = lr_ref[...], li_ref[...]
    step = jnp.exp(ls_ref[...])
    mag = jnp.exp(lr * step)
    ar = mag * jnp.cos(li * step)
    ai = mag * jnp.sin(li * step)
    den = lr * lr + li * li
    zr = ((ar - 1.0) * lr + ai * li) / den
    zi = (ai * lr - (ar - 1.0) * li) / den
    br, bi = br_ref[...], bi_ref[...]
    ar_ref[...] = ar
    ai_ref[...] = ai
    bbr_ref[...] = zr * br - zi * bi
    bbi_ref[...] = zr * bi + zi * br


def _s5_discretize(lam_re, lam_im, log_step, b_re, b_im):
    g, p, h = b_re.shape
    ex = lambda a: jnp.repeat(a.astype(F32), h, axis=1)
    ls = jnp.broadcast_to(log_step.astype(F32)[:, None], (g, p * h))
    sds = jax.ShapeDtypeStruct((g, p * h), F32)
    ar, ai, bbr, bbi = pl.pallas_call(
        _s5_disc_kernel, out_shape=(sds, sds, sds, sds), name="s5_discretize",
    )(ex(lam_re), ex(lam_im), ls, b_re.astype(F32).reshape(g, p * h), b_im.astype(F32).reshape(g, p * h))
    ar = ar.reshape(g, p, h)[:, :, 0]
    ai = ai.reshape(g, p, h)[:, :, 0]
    return ar, ai, bbr.reshape(g, p, h), bbi.reshape(g, p, h)


def _s5_kernel(u_ref, bmat_ref, a_ref, cre_ref, cim_ref, d_ref, gw_ref, gb_ref, gain_ref,
               o_ref, x_scr, h_scr, *, tc):
    ns = S5_NSTATE

    @pl.when(pl.program_id(0) == 0)
    def _():
        h_scr[...] = jnp.zeros_like(h_scr)

    u = u_ref[...]
    x_scr[...] = _dot(u, bmat_ref[...])
    ar = a_ref[0:SUBLANES, :]
    ai = a_ref[SUBLANES:2 * SUBLANES, :]

    def step(t, carry):
        hr, hi = carry
        r0 = pl.multiple_of(t * SUBLANES, SUBLANES)
        xr = x_scr[pl.ds(r0, SUBLANES), 0:ns]
        xi = x_scr[pl.ds(r0, SUBLANES), ns:2 * ns]
        nhr = ar * hr - ai * hi + xr
        nhi = ar * hi + ai * hr + xi
        x_scr[pl.ds(r0, SUBLANES), 0:ns] = nhr
        x_scr[pl.ds(r0, SUBLANES), ns:2 * ns] = nhi
        return nhr, nhi

    hr, hi = lax.fori_loop(0, tc, step, (h_scr[:, 0:ns], h_scr[:, ns:2 * ns]), unroll=8)
    h_scr[:, 0:ns] = hr
    h_scr[:, ns:2 * ns] = hi

    y = (_dot(x_scr[:, 0:ns], cre_ref[...]) - _dot(x_scr[:, ns:2 * ns], cim_ref[...])
         + d_ref[...] * u)
    y = jax.nn.gelu(y)
    y = y * jax.nn.sigmoid(_dot(y, gw_ref[...]) + gb_ref[...])
    o_ref[...] = (_rms(y, -1) * gain_ref[...]).astype(o_ref.dtype)


def _s5(u_tm, bsz, ar, ai, bbr, bbi, c_re, c_im, d, glu_w, glu_b, gain):
    assert bsz == SUBLANES, "the S5 scan keeps one batch row per sublane"
    seq = u_tm.shape[0]
    tc = min(S5_TC, seq)
    rows = tc * bsz
    eye = jnp.eye(S5_GROUPS, dtype=F32)
    bmat = jnp.concatenate(
        [jnp.einsum('gph,gk->ghkp', b, eye).reshape(B_WIDTH, S5_NSTATE) for b in (bbr, bbi)], axis=1)
    cre = jnp.einsum('ghp,gk->gpkh', c_re.astype(F32), eye).reshape(S5_NSTATE, B_WIDTH)
    cim = jnp.einsum('ghp,gk->gpkh', c_im.astype(F32), eye).reshape(S5_NSTATE, B_WIDTH)
    avec = jnp.concatenate([jnp.broadcast_to(a.reshape(1, S5_NSTATE), (SUBLANES, S5_NSTATE))
                            for a in (ar, ai)], axis=0)
    const = lambda shape: pl.BlockSpec(shape, lambda i: (0,) * len(shape))
    out = pl.pallas_call(
        functools.partial(_s5_kernel, tc=tc),
        out_shape=jax.ShapeDtypeStruct((seq * bsz, B_WIDTH), MXU_DTYPE),
        grid=(seq // tc,),
        in_specs=[pl.BlockSpec((rows, B_WIDTH), lambda i: (i, 0)),
                  const((B_WIDTH, 2 * S5_NSTATE)), const((2 * SUBLANES, S5_NSTATE)),
                  const((S5_NSTATE, B_WIDTH)), const((S5_NSTATE, B_WIDTH)),
                  const((1, B_WIDTH)), const((B_WIDTH, B_WIDTH)), const((1, B_WIDTH)),
                  const((1, B_WIDTH))],
        out_specs=pl.BlockSpec((rows, B_WIDTH), lambda i: (i, 0)),
        scratch_shapes=[pltpu.VMEM((rows, 2 * S5_NSTATE), F32),
                        pltpu.VMEM((SUBLANES, 2 * S5_NSTATE), F32)],
        compiler_params=_cparams(("arbitrary",)),
        name="s5_scan",
    )(u_tm.reshape(seq * bsz, B_WIDTH), bmat.astype(MXU_DTYPE), avec,
      cre.astype(MXU_DTYPE), cim.astype(MXU_DTYPE), d.astype(F32).reshape(1, B_WIDTH),
      glu_w.astype(MXU_DTYPE), glu_b.astype(F32).reshape(1, B_WIDTH), gain.reshape(1, B_WIDTH))
    return out.reshape(seq, bsz * B_WIDTH)


def _compress_kernel(seg_ref, pa_ref, pb_ref, wa_ref, wb_ref, w2_ref, cmp_ref, vT_ref):
    seg = seg_ref[...]
    xa = _dot(seg + pa_ref[...], wa_ref[...])
    xb = _dot(seg + pb_ref[...], wb_ref[...])
    nseg = seg.shape[0]
    pre = xa + pltpu.roll(xb, nseg - 1, 0)
    out = _dot(jax.nn.gelu(pre), w2_ref[...])
    cmp_ref[...] = out.astype(cmp_ref.dtype)
    vT_ref[...] = out[:, LANES:].T.astype(vT_ref.dtype)


def _compress(kvc, pos_k, pos_v, k_w1, k_w2, v_w1, v_w2):
    bsz, seq, _ = kvc.shape
    nseg = seq // CMP_STRIDE
    width = CMP_STRIDE * 256
    eye = jnp.eye(4, dtype=F32)
    w1 = jnp.stack([k_w1, k_w1, v_w1, v_w1]).astype(F32).reshape(4, CMP_BLOCK, HEAD_DIM, HEAD_DIM)
    pos = jnp.stack([pos_k, pos_k, pos_v, pos_v]).astype(F32)

    def half(lo):
        w = jnp.einsum('slde,st->lsdte', w1[:, lo:lo + CMP_STRIDE], eye).reshape(width, 256)
        p = jnp.transpose(pos[:, lo:lo + CMP_STRIDE], (1, 0, 2)).reshape(1, width)
        return w.astype(MXU_DTYPE), p

    wa, pa = half(0)
    wb, pb = half(CMP_STRIDE)
    w2 = jnp.einsum('sde,st->sdte', jnp.stack([k_w2, k_w2, v_w2, v_w2]).astype(F32), eye).reshape(256, 256)
    const = lambda shape: pl.BlockSpec(shape, lambda b: (0,) * len(shape))
    return pl.pallas_call(
        _compress_kernel,
        out_shape=(jax.ShapeDtypeStruct((bsz, nseg, 256), MXU_DTYPE),
                   jax.ShapeDtypeStruct((bsz, 128, nseg), MXU_DTYPE)),
        grid=(bsz,),
        in_specs=[pl.BlockSpec((None, nseg, width), lambda b: (b, 0, 0)),
                  const((1, width)), const((1, width)), const((width, 256)), const((width, 256)),
                  const((256, 256))],
        out_specs=(pl.BlockSpec((None, nseg, 256), lambda b: (b, 0, 0)),
                   pl.BlockSpec((None, 128, nseg), lambda b: (b, 0, 0))),
        compiler_params=_cparams(("arbitrary",)),
        name="nsa_compress",
    )(kvc.reshape(bsz, nseg, width), pa, pb, wa, wb, w2.astype(MXU_DTYPE))


ONES_ROWS = 16
NQ4 = 4 * Q_BLOCK


def _with_ones(v_t):
    return jnp.concatenate([v_t, jnp.ones((ONES_ROWS, v_t.shape[1]), v_t.dtype)], axis=0)


def _online_softmax_step(s, bias, v_ext, m_ref, acc_ref):
    s = s + bias
    m_old = m_ref[...]
    m_new = jnp.maximum(m_old, jnp.max(s, axis=0, keepdims=True))
    p = jnp.exp2(s - m_new)
    acc_ref[...] = jnp.exp2(m_old - m_new) * acc_ref[...] + _dot(v_ext, p)
    m_ref[...] = m_new


def _tile4(x):
    return jnp.concatenate([x, x, x, x], axis=1)


INT_MIN = -2 ** 31
NEG_INF_KEY = int(np.int32(np.uint32(0xFF800000 ^ 0x7FFFFFFF)))


def _dsa_kernel(qT_ref, iqT_ref, miscT_ref, kv_ref, misc_ref, cT_ref, mq_ref, wuv_ref, gain_ref, tri_ref,
                o_ref, key_scr, m_scr, acc_scr, *, k_top):
    i = pl.program_id(1)
    t0 = i * Q_BLOCK
    kc = KEY_CHUNK
    nch = (t0 + Q_BLOCK + kc - 1) // kc
    t_lane = t0 + lax.broadcasted_iota(I32, (1, Q_BLOCK), 1)
    row_iota = lax.broadcasted_iota(I32, (kc, Q_BLOCK), 0)

    qcat = (_dot(mq_ref[...], qT_ref[...]) * (HEAD_DIM ** -0.5 * LOG2E)).astype(MXU_DTYPE)
    qs_t = jnp.concatenate([qcat[h * 256:(h + 1) * 256] for h in range(A_HEADS)], axis=1)
    iq_t = iqT_ref[...]
    iq_all = jnp.concatenate([iq_t[h * IDX_DIM:(h + 1) * IDX_DIM, :] for h in range(IDX_HEADS)], axis=1)
    w_t = miscT_ref[MISC_IW:MISC_IW + IDX_HEADS, :] * (IDX_HEADS ** -0.5 * IDX_DIM ** -0.5)

    def idx_body(c, _):
        k0 = pl.multiple_of(c * kc, kc)
        ik = misc_ref[pl.ds(k0, kc), MISC_IK:MISC_IK + IDX_DIM]
        d = _dot(ik, iq_all)
        score = jnp.zeros((kc, Q_BLOCK), F32)
        for h in range(IDX_HEADS):
            score = score + jnp.maximum(d[:, h * Q_BLOCK:(h + 1) * Q_BLOCK], 0.0) * w_t[h:h + 1, :]
        score = jnp.where(row_iota + k0 <= t_lane, score, -jnp.inf)
        bits = pltpu.bitcast(score, I32)
        key_scr[pl.ds(k0, kc), :] = bits ^ ((bits >> 31) & 0x7FFFFFFF)
        return 0

    lax.fori_loop(0, nch, idx_body, 0)

    def count(pred):
        def body(c, acc):
            k0 = pl.multiple_of(c * kc, kc)
            hit = pred(key_scr[pl.ds(k0, kc), :])
            return acc + jnp.sum(hit.reshape(kc // SUBLANES, SUBLANES, Q_BLOCK), axis=0)
        acc = lax.fori_loop(0, nch, body, jnp.zeros((SUBLANES, Q_BLOCK), I32))
        return jnp.sum(acc, axis=0, keepdims=True)

    def thr_bit(b, prefix):
        cand = prefix | lax.shift_left(jnp.int32(1), 31 - b)
        cand_s = cand ^ INT_MIN
        cnt = count(lambda keys: jnp.where(keys >= cand_s, 1, 0))
        return jnp.where(cnt >= k_top, cand, prefix)

    thr = lax.fori_loop(0, 32, thr_bit, jnp.zeros((1, Q_BLOCK), I32)) ^ INT_MIN
    n_gt = count(lambda keys: jnp.where(keys > thr, 1, 0))
    need = jnp.where(thr == NEG_INF_KEY, 0, k_top - n_gt).astype(F32)

    m_scr[...] = jnp.full_like(m_scr, NEG)
    acc_scr[...] = jnp.zeros_like(acc_scr)

    def att_body(c, n_tied):
        k0 = pl.multiple_of(c * kc, kc)
        s = _dot(kv_ref[pl.ds(k0, kc), :], qs_t)
        keys = key_scr[pl.ds(k0, kc), :]
        tied = _dot(tri_ref[...], jnp.where(keys == thr, 1.0, 0.0)) + n_tied
        bias = jnp.where(keys > thr, 0.0,
                         jnp.where(keys == thr, jnp.where(tied <= need, 0.0, NEG), NEG))
        v_t = jnp.concatenate([cT_ref[c * (kc // LANES) + j] for j in range(kc // LANES)], axis=1)
        _online_softmax_step(s, _tile4(bias), _with_ones(v_t), m_scr, acc_scr)
        return tied[kc - 1:kc, :]

    lax.fori_loop(0, nch, att_body, jnp.zeros((1, Q_BLOCK), F32))

    acc = acc_scr[...]
    o_lat = (acc[:A_KV_RANK] / acc[A_KV_RANK:A_KV_RANK + 1]).astype(MXU_DTYPE)
    out_t = jnp.concatenate(
        [_dot(wuv_ref[h], o_lat[:, h * Q_BLOCK:(h + 1) * Q_BLOCK]) for h in range(A_HEADS)], axis=0)
    out_t = _rms(out_t, 0) * gain_ref[...]
    o_ref[...] = out_t.T.astype(o_ref.dtype)


def _dsa(aq_t, iq_t, misc_t, akv, misc, ac_t, w_uk, w_uv, gain):
    bsz, _, seq = aq_t.shape
    nq = seq // Q_BLOCK
    k_top = min(DSA_TOPK, seq // 4)
    seq_pad = -(-seq // KEY_CHUNK) * KEY_CHUNK
    tri = jnp.asarray(np.tril(np.ones((KEY_CHUNK, KEY_CHUNK), np.float32)), MXU_DTYPE)
    mq = jnp.zeros((A_HEADS, 256, A_HEADS, HEAD_DIM), F32)
    for h in range(A_HEADS):
        mq = mq.at[h, :A_KV_RANK, h, ROPE_DIM:].set(w_uk[:, h, :].astype(F32))
        mq = mq.at[h, A_KV_RANK:A_KV_RANK + ROPE_DIM, h, :ROPE_DIM].set(jnp.eye(ROPE_DIM, dtype=F32))
    mq = mq.reshape(A_HEADS * 256, A_HEADS * HEAD_DIM).astype(MXU_DTYPE)
    wuv_t = jnp.transpose(w_uv, (1, 2, 0)).astype(MXU_DTYPE)
    per_q = lambda w: pl.BlockSpec((None, w, Q_BLOCK), lambda b, i: (b, 0, i))
    per_b = lambda *s: pl.BlockSpec((None,) + s, lambda b, i: (b,) + (0,) * len(s))
    const = lambda shape: pl.BlockSpec(shape, lambda b, i: (0,) * len(shape))
    return pl.pallas_call(
        functools.partial(_dsa_kernel, k_top=k_top),
        out_shape=jax.ShapeDtypeStruct((bsz, seq, A_HEADS * A_VDIM), MXU_DTYPE),
        grid=(bsz, nq),
        in_specs=[per_q(256), per_q(128), per_q(128), per_b(seq, 256), per_b(seq, 128),
                  per_b(seq // LANES, LANES, LANES), const(mq.shape), const(wuv_t.shape),
                  const((A_HEADS * A_VDIM, 1)), const(tri.shape)],
        out_specs=pl.BlockSpec((None, Q_BLOCK, A_HEADS * A_VDIM), lambda b, i: (b, i, 0)),
        scratch_shapes=[pltpu.VMEM((seq_pad, Q_BLOCK), I32),
                        pltpu.VMEM((1, NQ4), F32),
                        pltpu.VMEM((A_KV_RANK + ONES_ROWS, NQ4), F32)],
        compiler_params=_cparams(("arbitrary", "arbitrary")),
        name="dsa_attention",
    )(aq_t, iq_t, misc_t, akv, misc, ac_t, mq, wuv_t, gain.reshape(-1, 1), tri)


def _nsa_kernel(qT_ref, qrT_ref, miscT_ref, cmp_ref, vcT_ref, ks_ref, vsT_ref, kw_ref, vwT_ref,
                ov_ref, gain_ref, o_ref, sel_scr, m_scr, acc_scr, *, n_top, n_blk):
    i = pl.program_id(1)
    t0 = i * Q_BLOCK
    kc = KEY_CHUNK
    nch = (t0 + Q_BLOCK + kc - 1) // kc
    t_lane = t0 + lax.broadcasted_iota(I32, (1, Q_BLOCK), 1)
    gates = jax.nn.sigmoid(miscT_ref[MISC_GATE:MISC_GATE + 3 * C_HEADS, :])
    n_cmp = cmp_ref.shape[0]
    cmp_iota = lax.broadcasted_iota(I32, (n_cmp, Q_BLOCK), 0)
    blk_iota = lax.broadcasted_iota(I32, (n_blk, Q_BLOCK), 0)
    row_iota = lax.broadcasted_iota(I32, (kc, Q_BLOCK), 0)
    win_iota = lax.broadcasted_iota(I32, (Q_BLOCK, Q_BLOCK), 0)
    slabs = []

    for g in range(C_KV_HEADS):
        heads = range(g * C_GROUP, (g + 1) * C_GROUP)
        q_t = jnp.concatenate([qT_ref[h * HEAD_DIM:(h + 1) * HEAD_DIM, :] for h in heads], axis=1)
        qr_t = jnp.concatenate([qrT_ref[h * HEAD_DIM:(h + 1) * HEAD_DIM, :] for h in heads], axis=1)
        gsl = slice(g * HEAD_DIM, (g + 1) * HEAD_DIM)

        s_c = _dot(cmp_ref[:, gsl], q_t)
        vis = _tile4(jnp.where(cmp_iota * CMP_STRIDE + (CMP_BLOCK - 1) <= t_lane, 1, 0)) > 0
        s_c = jnp.where(vis, s_c, NEG)
        p_c = jnp.where(vis, jnp.exp2(s_c - jnp.max(s_c, axis=0, keepdims=True)), 0.0)
        l_c = jnp.sum(p_c, axis=0, keepdims=True)
        p_c = p_c * (1.0 / jnp.maximum(l_c, 1e-30))
        o_c = _dot(vcT_ref[gsl, :], p_c)

        p_sum = (p_c[:, 0:Q_BLOCK] + p_c[:, Q_BLOCK:2 * Q_BLOCK]
                 + p_c[:, 2 * Q_BLOCK:3 * Q_BLOCK] + p_c[:, 3 * Q_BLOCK:4 * Q_BLOCK])
        p_hi = p_sum.astype(MXU_DTYPE)
        p_lo = p_sum - p_hi.astype(F32)
        imp = _dot(ov_ref[...], p_hi) + _dot(ov_ref[...], p_lo)
        cur = t_lane // SEL_BLOCK
        forced = jnp.where(blk_iota == 0, 1, jnp.where(blk_iota == cur, 1,
                           jnp.where(blk_iota == cur - 1, 1, 0)))
        imp = jnp.where(forced > 0, SEL_FORCE, imp)
        imp = jnp.where(blk_iota * SEL_BLOCK <= t_lane, imp, -jnp.inf)
        rank = jnp.zeros((n_blk, Q_BLOCK), I32)
        for mp in range(n_blk):
            row = imp[mp:mp + 1, :]
            rank = rank + jnp.where(row > imp, 1,
                                    jnp.where(row == imp, jnp.where(blk_iota > mp, 1, 0), 0))
        sel_scr[...] = jnp.where(rank < n_top, 0.0, NEG)

        m_scr[...] = jnp.full_like(m_scr, NEG)
        acc_scr[...] = jnp.zeros_like(acc_scr)
        bpc = kc // SEL_BLOCK

        def sel_body(c, _):
            k0 = pl.multiple_of(c * kc, kc)
            s = _dot(ks_ref[pl.ds(k0, kc), gsl], qr_t)
            sel8 = sel_scr[pl.ds(pl.multiple_of(c * bpc, bpc), bpc), :]
            bias = jnp.concatenate(
                [jnp.broadcast_to(sel8[j:j + 1, :], (SEL_BLOCK, Q_BLOCK)) for j in range(bpc)], axis=0)
            bias = jnp.where(row_iota + k0 <= t_lane, bias, NEG)
            v_t = jnp.concatenate([vsT_ref[c * (kc // LANES) + j][gsl, :]
                                   for j in range(kc // LANES)], axis=1)
            _online_softmax_step(s, _tile4(bias), _with_ones(v_t), m_scr, acc_scr)
            return 0

        lax.fori_loop(0, nch, sel_body, 0)
        acc = acc_scr[...]
        o_s = acc[:HEAD_DIM] / acc[HEAD_DIM:HEAD_DIM + 1]

        k_parts, v_parts, m_parts = [], [], []
        for j in range(WINDOW // Q_BLOCK + 1):
            cj = i - WINDOW // Q_BLOCK + j
            cjc = jnp.maximum(cj, 0)
            k0 = pl.multiple_of(cjc * Q_BLOCK, Q_BLOCK)
            k_parts.append(kw_ref[pl.ds(k0, Q_BLOCK), gsl])
            v_parts.append(vwT_ref[cjc][gsl, :])
            kidx = win_iota + k0
            inside = jnp.where(kidx <= t_lane, jnp.where(kidx > t_lane - WINDOW, 0.0, NEG), NEG)
            m_parts.append(jnp.where(cj >= 0, inside, NEG))
        s_w = _dot(jnp.concatenate(k_parts, axis=0), qr_t) + _tile4(jnp.concatenate(m_parts, axis=0))
        p_w = jnp.exp2(s_w - jnp.max(s_w, axis=0, keepdims=True))
        acc = _dot(_with_ones(jnp.concatenate(v_parts, axis=1)), p_w)
        o_w = acc[:HEAD_DIM] / acc[HEAD_DIM:HEAD_DIM + 1]

        for hh in range(C_GROUP):
            sl = slice(hh * Q_BLOCK, (hh + 1) * Q_BLOCK)
            r = (g * C_GROUP + hh) * 3
            slabs.append(gates[r:r + 1, :] * o_c[:, sl] + gates[r + 1:r + 2, :] * o_s[:, sl]
                         + gates[r + 2:r + 3, :] * o_w[:, sl])

    out_t = jnp.concatenate(slabs, axis=0)
    out_t = _rms(out_t, 0) * gain_ref[...]
    o_ref[...] = out_t.T.astype(o_ref.dtype)


def _sel_overlap_t(n_cmp_rows, n_blk):
    cs = np.arange(n_cmp_rows)[None, :] * CMP_STRIDE
    ss = np.arange(n_blk)[:, None] * SEL_BLOCK
    ov = np.minimum(cs + CMP_BLOCK, ss + SEL_BLOCK) - np.maximum(cs, ss)
    return np.clip(ov, 0, None).astype(np.float32) / CMP_BLOCK


def _nsa(cq_t, cqr_t, misc_t, cmp, vcmp_t, ks, vs_t, kw, vw_t, gain):
    bsz, _, seq = cq_t.shape
    nq = seq // Q_BLOCK
    nseg = cmp.shape[1]
    n_blk = seq // SEL_BLOCK
    n_top = min(SEL_TOPN, n_blk)
    seq_pad = -(-seq // KEY_CHUNK) * KEY_CHUNK
    ov_np = _sel_overlap_t(nseg, n_blk)
    ov_np[:, (seq - CMP_BLOCK) // CMP_STRIDE + 1:] = 0.0
    ov = jnp.asarray(ov_np, MXU_DTYPE)
    per_q = lambda w: pl.BlockSpec((None, w, Q_BLOCK), lambda b, i: (b, 0, i))
    per_b = lambda *s: pl.BlockSpec((None,) + s, lambda b, i: (b,) + (0,) * len(s))
    const = lambda shape: pl.BlockSpec(shape, lambda b, i: (0,) * len(shape))
    return pl.pallas_call(
        functools.partial(_nsa_kernel, n_top=n_top, n_blk=n_blk),
        out_shape=jax.ShapeDtypeStruct((bsz, seq, C_HEADS * HEAD_DIM), MXU_DTYPE),
        grid=(bsz, nq),
        in_specs=[per_q(512), per_q(512), per_q(128), per_b(nseg, 256), per_b(128, nseg),
                  per_b(seq, 128), per_b(seq // LANES, LANES, LANES),
                  per_b(seq, 128), per_b(seq // LANES, LANES, LANES),
                  const(ov.shape), const((C_HEADS * HEAD_DIM, 1))],
        out_specs=pl.BlockSpec((None, Q_BLOCK, C_HEADS * HEAD_DIM), lambda b, i: (b, i, 0)),
        scratch_shapes=[pltpu.VMEM((max(n_blk, seq_pad // SEL_BLOCK), Q_BLOCK), F32),
                        pltpu.VMEM((1, NQ4), F32),
                        pltpu.VMEM((HEAD_DIM + ONES_ROWS, NQ4), F32)],
        compiler_params=_cparams(("arbitrary", "arbitrary")),
        name="nsa_attention",
    )(cq_t, cqr_t, misc_t, cmp, vcmp_t, ks, vs_t, kw, vw_t, ov, gain.reshape(-1, 1))


def _out_mlp_kernel(x_ref, a_ref, b_ref, c_ref, wo_ref, g2_ref, wu_ref, wd_ref, fg_ref,
                    o_ref, x1_scr, n2_scr, acc_scr, *, final_norm):
    j = pl.program_id(1)

    @pl.when(j == 0)
    def _():
        mixed = (jnp.dot(a_ref[...], wo_ref[0:256, :], preferred_element_type=F32)
                 + jnp.dot(b_ref[...], wo_ref[256:512, :], preferred_element_type=F32)
                 + jnp.dot(c_ref[...], wo_ref[512:1024, :], preferred_element_type=F32))
        x1 = x_ref[...] + mixed
        x1_scr[...] = x1
        n2_scr[...] = (_rms(x1, -1) * g2_ref[...]).astype(n2_scr.dtype)
        acc_scr[...] = jnp.zeros_like(acc_scr)

    z = jnp.dot(n2_scr[...], wu_ref[...], preferred_element_type=F32)
    acc_scr[...] += _dot(jnp.square(jnp.maximum(z, 0.0)), wd_ref[...])

    @pl.when(j == pl.num_programs(1) - 1)
    def _():
        x2 = x1_scr[...] + acc_scr[...]
        if final_norm:
            x2 = _rms(x2, -1) * fg_ref[...]
        o_ref[...] = x2


def _out_mlp(x, a_n, b_tm, c_n, w_out, ln2_g, w_up, w_down, final_g, final_norm):
    bsz, seq, _ = x.shape
    tm = min(MLP_TM, seq)
    nj = seq // tm
    nf = D_FF // MLP_TF
    rows = lambda w: pl.BlockSpec((None, tm, w), lambda r, f: (r // nj, r % nj, 0))
    const = lambda shape: pl.BlockSpec(shape, lambda r, f: (0,) * len(shape))
    return pl.pallas_call(
        functools.partial(_out_mlp_kernel, final_norm=final_norm),
        out_shape=jax.ShapeDtypeStruct(x.shape, F32),
        grid=(bsz * nj, nf),
        in_specs=[rows(D_MODEL), rows(256),
                  pl.BlockSpec((tm, B_WIDTH), lambda r, f: (r % nj, r // nj)),
                  rows(512), const((D_MODEL, D_MODEL)), const((1, D_MODEL)),
                  pl.BlockSpec((D_MODEL, MLP_TF), lambda r, f: (0, f)),
                  pl.BlockSpec((MLP_TF, D_MODEL), lambda r, f: (f, 0)),
                  const((1, D_MODEL))],
        out_specs=rows(D_MODEL),
        scratch_shapes=[pltpu.VMEM((tm, D_MODEL), F32), pltpu.VMEM((tm, D_MODEL), MXU_DTYPE),
                        pltpu.VMEM((tm, D_MODEL), F32)],
        compiler_params=_cparams(("arbitrary", "arbitrary")),
        name="out_proj_mlp",
    )(x, a_n, b_tm, c_n, w_out.astype(MXU_DTYPE), ln2_g.reshape(1, -1), w_up.astype(MXU_DTYPE),
      w_down.astype(MXU_DTYPE), final_g.reshape(1, -1))


def kernel(x, positions, ln1_g, w_in, kv_norm_g, w_uk, w_uv, s5_lambda_re, s5_lambda_im, s5_log_step, s5_b_re, s5_b_im, s5_c_re, s5_c_im, s5_d, s5_glu_w, s5_glu_b, cmp_pos_k, cmp_pos_v, cmp_k_w1, cmp_k_w2, cmp_v_w1, cmp_v_w2, gain_a, gain_b, gain_c, w_out, ln2_g, w_up, w_down, final_g):
    bsz, seq, _ = x.shape
    depth = w_in.shape[0]
    src, partner, ec_np, es_np = _proj_layout()
    ec = jnp.asarray(ec_np, MXU_DTYPE)
    es = jnp.asarray(es_np, MXU_DTYPE)
    tab = _rope_table(positions)

    def regather(w, idx):
        cols = jnp.take(w, jnp.asarray(np.maximum(idx, 0)), axis=1)
        return jnp.where(jnp.asarray(idx >= 0)[None, :], cols, 0.0).astype(MXU_DTYPE)

    for layer in range(depth):
        w_r = regather(w_in[layer], src)
        w_sw = regather(w_in[layer], partner)
        (aq_t, akv, ac_t, iq_t, misc, misc_t, u_tm, cq_t, cqr_t, kvc, ks, vs_t, kw, vw_t) = _in_proj(
            x, tab, ln1_g[layer], kv_norm_g[layer], w_r, w_sw, ec, es)

        ar, ai, bbr, bbi = _s5_discretize(s5_lambda_re[layer], s5_lambda_im[layer], s5_log_step[layer],
                                          s5_b_re[layer], s5_b_im[layer])
        b_tm = _s5(u_tm, bsz, ar, ai, bbr, bbi, s5_c_re[layer], s5_c_im[layer], s5_d[layer].reshape(-1),
                   s5_glu_w[layer], s5_glu_b[layer], gain_b[layer])

        cmp, vcmp_t = _compress(kvc, cmp_pos_k[layer], cmp_pos_v[layer], cmp_k_w1[layer],
                                cmp_k_w2[layer], cmp_v_w1[layer], cmp_v_w2[layer])
        a_n = _dsa(aq_t, iq_t, misc_t, akv, misc, ac_t, w_uk[layer], w_uv[layer], gain_a[layer])
        c_n = _nsa(cq_t, cqr_t, misc_t, cmp, vcmp_t, ks, vs_t, kw, vw_t, gain_c[layer])

        x = _out_mlp(x, a_n, b_tm, c_n, w_out[layer], ln2_g[layer], w_up[layer], w_down[layer],
                     final_g, final_norm=(layer == depth - 1))
    return x
```

```python
import functools
import math

import numpy as np
import jax
import jax.numpy as jnp
from jax import lax
from jax.experimental import pallas as pl
from jax.experimental.pallas import tpu as pltpu

F32 = jnp.float32
I32 = jnp.int32
MXU_DTYPE = jnp.bfloat16

D_MODEL = 1024
HEAD_DIM = 64
ROPE_THETA = 500000.0
ROPE_DIM = HEAD_DIM // 4
NORM_EPS = 1e-6
Q_BLOCK = 128
NEG = -1e30
LOG2E = math.log2(math.e)
D_FF = 4 * D_MODEL

A_HEADS = 4
A_NOPE = HEAD_DIM - ROPE_DIM
A_VDIM = HEAD_DIM
A_KV_RANK = 128
IDX_HEADS = 4
IDX_DIM = 32
IDX_ROPE = IDX_DIM // 4
DSA_TOPK = 256

B_WIDTH = 256
S5_GROUP = 16
S5_GROUPS = B_WIDTH // S5_GROUP
S5_STATE = 64
S5_NSTATE = S5_GROUPS * S5_STATE

C_HEADS = 8
C_KV_HEADS = 2
C_GROUP = C_HEADS // C_KV_HEADS
CMP_BLOCK = 32
CMP_STRIDE = 16
SEL_BLOCK = 64
SEL_TOPN = 16
SEL_FORCE = 1e9
WINDOW = 512

IN_SIZES = (A_HEADS * HEAD_DIM, A_KV_RANK, ROPE_DIM, IDX_HEADS * IDX_DIM, IDX_DIM, IDX_HEADS,
            B_WIDTH, C_HEADS * HEAD_DIM, 6 * C_KV_HEADS * HEAD_DIM, 3 * C_HEADS)

LANES = 128
SUBLANES = 8
VMEM_LIMIT = 56 * 1024 * 1024

SEG_AQ, SEG_AKV, SEG_IQ, SEG_MISC, SEG_BU, SEG_CQ, SEG_KVC, SEG_KS, SEG_VS, SEG_KW, SEG_VW = (
    0, 256, 512, 640, 768, 1024, 1536, 1792, 1920, 2048, 2176)
N_PROJ = 2304
MISC_IK, MISC_IW, MISC_GATE = 0, 32, 36
TAB_COS, TAB_SIN, TAB_ICOS, TAB_ISIN, TAB_ONE = 0, 8, 16, 20, 24
TAB_PART = 32

PROJ_TM = 512
DSA_KEY_CHUNK = 512
NSA_KEY_CHUNK = 1024
S5_TC = 64
MLP_TM = 1024
MLP_TF = 1024


def _cparams(sem):
    return pltpu.CompilerParams(dimension_semantics=sem, vmem_limit_bytes=VMEM_LIMIT)


def _dot(a, b):
    return jnp.dot(a.astype(MXU_DTYPE), b.astype(MXU_DTYPE), preferred_element_type=F32)


def _rms(x, axis):
    return x * lax.rsqrt(jnp.mean(x * x, axis=axis, keepdims=True) + NORM_EPS)


def _rope_tab_kernel(pos_ref, freq_ref, out_ref):
    ang = pos_ref[...].astype(F32) * freq_ref[...]
    lane = lax.broadcasted_iota(I32, ang.shape, 1)
    l32 = lane % TAB_PART
    is_cos = (l32 < TAB_SIN) | ((l32 >= TAB_ICOS) & (l32 < TAB_ISIN))
    is_sin = ((l32 >= TAB_SIN) & (l32 < TAB_ICOS)) | ((l32 >= TAB_ISIN) & (l32 < TAB_ONE))
    val = jnp.where(is_cos, jnp.cos(ang),
                    jnp.where(is_sin, jnp.sin(ang), jnp.where(l32 == TAB_ONE, 1.0, 0.0)))
    hi = val.astype(jnp.bfloat16).astype(F32)
    r1 = val - hi
    mid = r1.astype(jnp.bfloat16).astype(F32)
    lo = r1 - mid
    part = lane // TAB_PART
    out = jnp.where(part == 0, hi, jnp.where(part == 1, mid, jnp.where(part == 2, lo, 0.0)))
    out_ref[...] = out.astype(jnp.bfloat16)


def _rope_table(positions):
    bsz, seq = positions.shape
    t = bsz * seq
    inv_r = (np.float32(ROPE_THETA) ** (-np.arange(0, ROPE_DIM, 2, dtype=np.float32) / ROPE_DIM))
    inv_i = (np.float32(ROPE_THETA) ** (-np.arange(0, IDX_ROPE, 2, dtype=np.float32) / IDX_ROPE))
    f32 = np.zeros(TAB_PART, np.float32)
    f32[TAB_COS:TAB_COS + 8] = inv_r
    f32[TAB_SIN:TAB_SIN + 8] = inv_r
    f32[TAB_ICOS:TAB_ICOS + 4] = inv_i
    f32[TAB_ISIN:TAB_ISIN + 4] = inv_i
    freq = jnp.asarray(np.tile(f32, LANES // TAB_PART)[None, :])
    tq = min(1024, t)
    return pl.pallas_call(
        _rope_tab_kernel,
        out_shape=jax.ShapeDtypeStruct((t, LANES), jnp.bfloat16),
        grid=(t // tq,),
        in_specs=[pl.BlockSpec((tq, 1), lambda i: (i, 0)),
                  pl.BlockSpec((1, LANES), lambda i: (0, 0))],
        out_specs=pl.BlockSpec((tq, LANES), lambda i: (i, 0)),
        compiler_params=_cparams(("arbitrary",)),
        name="rope_table",
    )(positions.reshape(t, 1), freq).reshape(bsz, seq, LANES)


def _proj_layout():
    offs = np.concatenate([[0], np.cumsum(IN_SIZES)])
    o_aq, o_ckv, o_kr, o_iq, o_ik, o_iw, o_bu, o_cq, o_kv, o_gate = offs[:10]
    src = -np.ones(N_PROJ, np.int64)
    partner = -np.ones(N_PROJ, np.int64)
    clane = np.full(N_PROJ, TAB_ONE, np.int64)
    slane = -np.ones(N_PROJ, np.int64)
    ssign = np.zeros(N_PROJ, np.float32)

    def plain(c0, o0, w):
        src[c0:c0 + w] = np.arange(o0, o0 + w)

    def rope(c0, o0, half, cos_lane, sin_lane):
        for j in range(half):
            partner[c0 + j] = o0 + half + j
            partner[c0 + half + j] = o0 + j
            clane[c0 + j] = clane[c0 + half + j] = cos_lane + j
            slane[c0 + j] = slane[c0 + half + j] = sin_lane + j
            ssign[c0 + j] = -1.0
            ssign[c0 + half + j] = 1.0

    plain(SEG_AQ, o_aq, A_HEADS * HEAD_DIM)
    for h in range(A_HEADS):
        rope(SEG_AQ + h * HEAD_DIM, o_aq + h * HEAD_DIM, ROPE_DIM // 2, TAB_COS, TAB_SIN)
    plain(SEG_AKV, o_ckv, A_KV_RANK)
    plain(SEG_AKV + A_KV_RANK, o_kr, ROPE_DIM)
    rope(SEG_AKV + A_KV_RANK, o_kr, ROPE_DIM // 2, TAB_COS, TAB_SIN)
    plain(SEG_IQ, o_iq, IDX_HEADS * IDX_DIM)
    for h in range(IDX_HEADS):
        rope(SEG_IQ + h * IDX_DIM, o_iq + h * IDX_DIM, IDX_ROPE // 2, TAB_ICOS, TAB_ISIN)
    plain(SEG_MISC + MISC_IK, o_ik, IDX_DIM)
    rope(SEG_MISC + MISC_IK, o_ik, IDX_ROPE // 2, TAB_ICOS, TAB_ISIN)
    plain(SEG_MISC + MISC_IW, o_iw, IDX_HEADS)
    plain(SEG_MISC + MISC_GATE, o_gate, 3 * C_HEADS)
    plain(SEG_BU, o_bu, B_WIDTH)
    plain(SEG_CQ, o_cq, C_HEADS * HEAD_DIM)
    for h in range(C_HEADS):
        rope(SEG_CQ + h * HEAD_DIM, o_cq + h * HEAD_DIM, ROPE_DIM // 2, TAB_COS, TAB_SIN)
    plain(SEG_KVC, o_kv, 6 * C_KV_HEADS * HEAD_DIM)
    for seg, sub in ((SEG_KS, 2), (SEG_KW, 4)):
        for g in range(C_KV_HEADS):
            rope(seg + g * HEAD_DIM, o_kv + sub * C_KV_HEADS * HEAD_DIM + g * HEAD_DIM,
                 ROPE_DIM // 2, TAB_COS, TAB_SIN)

    k = np.arange(LANES)[:, None]
    live = k < 3 * TAB_PART
    ec = (live & ((k % TAB_PART) == clane[None, :])).astype(np.float32)
    es = (live & ((k % TAB_PART) == slane[None, :])).astype(np.float32) * ssign[None, :]
    return src, partner, ec, es


def _in_proj_kernel(x_ref, tab_ref, g_ref, kvg_ref, w_ref, wsw_ref, ec_ref, es_ref,
                    aqT_ref, akv_ref, acT_ref, iqT_ref, misc_ref, miscT_ref, u_ref,
                    cqT_ref, cqrT_ref, kvc_ref, ks_ref, vsT_ref, kw_ref, vwT_ref):
    x = x_ref[...]
    n = (_rms(x, -1) * g_ref[...]).astype(MXU_DTYPE)
    tab = tab_ref[...]

    def plain(c0, w):
        return jnp.dot(n, w_ref[:, c0:c0 + w], preferred_element_type=F32)

    def roped(c0, w):
        p = plain(c0, w)
        psw = jnp.dot(n, wsw_ref[:, c0:c0 + w], preferred_element_type=F32)
        cos = jnp.dot(tab, ec_ref[:, c0:c0 + w], preferred_element_type=F32)
        sin = jnp.dot(tab, es_ref[:, c0:c0 + w], preferred_element_type=F32)
        return p * cos + psw * sin

    def chunked_t(val, ref):
        vt = val.T.astype(ref.dtype)
        for c in range(ref.shape[0]):
            ref[c] = vt[:, c * LANES:(c + 1) * LANES]

    aqT_ref[...] = roped(SEG_AQ, 256).T.astype(aqT_ref.dtype)

    akv = roped(SEG_AKV, 256)
    lat = _rms(akv[:, :A_KV_RANK], -1) * kvg_ref[...]
    akv_ref[...] = jnp.concatenate([lat, akv[:, A_KV_RANK:]], axis=1).astype(akv_ref.dtype)
    chunked_t(lat, acT_ref)

    iqT_ref[...] = roped(SEG_IQ, 128).T.astype(iqT_ref.dtype)
    misc = roped(SEG_MISC, 128)
    misc_ref[...] = misc
    miscT_ref[...] = misc.T
    u_ref[...] = plain(SEG_BU, 256)
    qk_scale = HEAD_DIM ** -0.5 * LOG2E
    cqT_ref[...] = (plain(SEG_CQ, 512) * qk_scale).T.astype(cqT_ref.dtype)
    cqrT_ref[...] = (roped(SEG_CQ, 512) * qk_scale).T.astype(cqrT_ref.dtype)
    kvc_ref[...] = plain(SEG_KVC, 256)
    ks_ref[...] = roped(SEG_KS, 128).astype(ks_ref.dtype)
    chunked_t(plain(SEG_VS, 128), vsT_ref)
    kw_ref[...] = roped(SEG_KW, 128).astype(kw_ref.dtype)
    chunked_t(plain(SEG_VW, 128), vwT_ref)


def _in_proj(x, tab, ln_g, kv_g, w_r, w_sw, ec, es):
    bsz, seq, _ = x.shape
    tm = min(PROJ_TM, seq)
    nj = seq // tm
    nck = tm // LANES
    bf = MXU_DTYPE

    def rows(w):
        return pl.BlockSpec((None, tm, w), lambda b, j: (b, j, 0))

    def cols(w):
        return pl.BlockSpec((None, w, tm), lambda b, j: (b, 0, j))

    def chunks():
        return pl.BlockSpec((None, nck, LANES, LANES), lambda b, j: (b, j, 0, 0))

    def const(shape):
        return pl.BlockSpec(shape, lambda b, j: (0,) * len(shape))

    sds = jax.ShapeDtypeStruct
    out_shape = (
        sds((bsz, 256, seq), bf),
        sds((bsz, seq, 256), bf),
        sds((bsz, seq // LANES, LANES, LANES), bf),
        sds((bsz, 128, seq), bf),
        sds((bsz, seq, 128), F32),
        sds((bsz, 128, seq), F32),
        sds((seq, bsz * B_WIDTH), F32),
        sds((bsz, 512, seq), bf),
        sds((bsz, 512, seq), bf),
        sds((bsz, seq, 256), F32),
        sds((bsz, seq, 128), bf),
        sds((bsz, seq // LANES, LANES, LANES), bf),
        sds((bsz, seq, 128), bf),
        sds((bsz, seq // LANES, LANES, LANES), bf),
    )
    out_specs = (cols(256), rows(256), chunks(), cols(128), rows(128), cols(128),
                 pl.BlockSpec((tm, B_WIDTH), lambda b, j: (j, b)),
                 cols(512), cols(512), rows(256), rows(128), chunks(), rows(128), chunks())
    return pl.pallas_call(
        _in_proj_kernel,
        out_shape=out_shape,
        grid=(bsz, nj),
        in_specs=[rows(D_MODEL), rows(LANES), const((1, D_MODEL)), const((1, A_KV_RANK)),
                  const((D_MODEL, N_PROJ)), const((D_MODEL, N_PROJ)),
                  const((LANES, N_PROJ)), const((LANES, N_PROJ))],
        out_specs=out_specs,
        compiler_params=_cparams(("arbitrary", "arbitrary")),
        name="in_proj",
    )(x, tab, ln_g.reshape(1, -1), kv_g.reshape(1, -1), w_r, w_sw, ec, es)


def _s5_disc_kernel(lr_ref, li_ref, ls_ref, br_ref, bi_ref, ar_ref, ai_ref, bbr_ref, bbi_ref):
    lr, li = lr_ref[...], li_ref[...]
    step = jnp.exp(ls_ref[...])
    mag = jnp.exp(lr * step)
    ar = mag * jnp.cos(li * step)
    ai = mag * jnp.sin(li * step)
    den = lr * lr + li * li
    zr = ((ar - 1.0) * lr + ai * li) / den
    zi = (ai * lr - (ar - 1.0) * li) / den
    br, bi = br_ref[...], bi_ref[...]
    ar_ref[...] = ar
    ai_ref[...] = ai
    bbr_ref[...] = zr * br - zi * bi
    bbi_ref[...] = zr * bi + zi * br


def _s5_discretize(lam_re, lam_im, log_step, b_re, b_im):
    g, p, h = b_re.shape
    ex = lambda a: jnp.repeat(a.astype(F32), h, axis=1)
    ls = jnp.broadcast_to(log_step.astype(F32)[:, None], (g, p * h))
    sds = jax.ShapeDtypeStruct((g, p * h), F32)
    ar, ai, bbr, bbi = pl.pallas_call(
        _s5_disc_kernel, out_shape=(sds, sds, sds, sds), name="s5_discretize",
    )(ex(lam_re), ex(lam_im), ls, b_re.astype(F32).reshape(g, p * h), b_im.astype(F32).reshape(g, p * h))
    ar = ar.reshape(g, p, h)[:, :, 0]
    ai = ai.reshape(g, p, h)[:, :, 0]
    return ar, ai, bbr.reshape(g, p, h), bbi.reshape(g, p, h)


def _s5_kernel(u_ref, bmat_ref, a_ref, cre_ref, cim_ref, d_ref, gw_ref, gb_ref, gain_ref,
               o_ref, x_scr, h_scr, *, tc):
    ns = S5_NSTATE

    @pl.when(pl.program_id(0) == 0)
    def _():
        h_scr[...] = jnp.zeros_like(h_scr)

    u = u_ref[...]
    x_scr[...] = _dot(u, bmat_ref[...])
    ar = a_ref[0:SUBLANES, :]
    ai = a_ref[SUBLANES:2 * SUBLANES, :]

    def step(t, carry):
        hr, hi = carry
        r0 = pl.multiple_of(t * SUBLANES, SUBLANES)
        xr = x_scr[pl.ds(r0, SUBLANES), 0:ns]
        xi = x_scr[pl.ds(r0, SUBLANES), ns:2 * ns]
        nhr = ar * hr - ai * hi + xr
        nhi = ar * hi + ai * hr + xi
        x_scr[pl.ds(r0, SUBLANES), 0:ns] = nhr
        x_scr[pl.ds(r0, SUBLANES), ns:2 * ns] = nhi
        return nhr, nhi

    hr, hi = lax.fori_loop(0, tc, step, (h_scr[:, 0:ns], h_scr[:, ns:2 * ns]), unroll=8)
    h_scr[:, 0:ns] = hr
    h_scr[:, ns:2 * ns] = hi

    y = (_dot(x_scr[:, 0:ns], cre_ref[...]) - _dot(x_scr[:, ns:2 * ns], cim_ref[...])
         + d_ref[...] * u)
    y = jax.nn.gelu(y)
    y = y * jax.nn.sigmoid(_dot(y, gw_ref[...]) + gb_ref[...])
    o_ref[...] = (_rms(y, -1) * gain_ref[...]).astype(o_ref.dtype)


def _s5(u_tm, bsz, ar, ai, bbr, bbi, c_re, c_im, d, glu_w, glu_b, gain):
    assert bsz == SUBLANES, "the S5 scan keeps one batch row per sublane"
    seq = u_tm.shape[0]
    tc = min(S5_TC, seq)
    rows = tc * bsz
    eye = jnp.eye(S5_GROUPS, dtype=F32)
    bmat = jnp.concatenate(
        [jnp.einsum('gph,gk->ghkp', b, eye).reshape(B_WIDTH, S5_NSTATE) for b in (bbr, bbi)], axis=1)
    cre = jnp.einsum('ghp,gk->gpkh', c_re.astype(F32), eye).reshape(S5_NSTATE, B_WIDTH)
    cim = jnp.einsum('ghp,gk->gpkh', c_im.astype(F32), eye).reshape(S5_NSTATE, B_WIDTH)
    avec = jnp.concatenate([jnp.broadcast_to(a.reshape(1, S5_NSTATE), (SUBLANES, S5_NSTATE))
                            for a in (ar, ai)], axis=0)
    const = lambda shape: pl.BlockSpec(shape, lambda i: (0,) * len(shape))
    out = pl.pallas_call(
        functools.partial(_s5_kernel, tc=tc),
        out_shape=jax.ShapeDtypeStruct((seq * bsz, B_WIDTH), MXU_DTYPE),
        grid=(seq // tc,),
        in_specs=[pl.BlockSpec((rows, B_WIDTH), lambda i: (i, 0)),
                  const((B_WIDTH, 2 * S5_NSTATE)), const((2 * SUBLANES, S5_NSTATE)),
                  const((S5_NSTATE, B_WIDTH)), const((S5_NSTATE, B_WIDTH)),
                  const((1, B_WIDTH)), const((B_WIDTH, B_WIDTH)), const((1, B_WIDTH)),
                  const((1, B_WIDTH))],
        out_specs=pl.BlockSpec((rows, B_WIDTH), lambda i: (i, 0)),
        scratch_shapes=[pltpu.VMEM((rows, 2 * S5_NSTATE), F32),
                        pltpu.VMEM((SUBLANES, 2 * S5_NSTATE), F32)],
        compiler_params=_cparams(("arbitrary",)),
        name="s5_scan",
    )(u_tm.reshape(seq * bsz, B_WIDTH), bmat.astype(MXU_DTYPE), avec,
      cre.astype(MXU_DTYPE), cim.astype(MXU_DTYPE), d.astype(F32).reshape(1, B_WIDTH),
      glu_w.astype(MXU_DTYPE), glu_b.astype(F32).reshape(1, B_WIDTH), gain.reshape(1, B_WIDTH))
    return out.reshape(seq, bsz * B_WIDTH)


def _compress_kernel(seg_ref, pa_ref, pb_ref, wa_ref, wb_ref, w2_ref, cmp_ref, vT_ref):
    seg = seg_ref[...]
    xa = _dot(seg + pa_ref[...], wa_ref[...])
    xb = _dot(seg + pb_ref[...], wb_ref[...])
    nseg = seg.shape[0]
    pre = xa + pltpu.roll(xb, nseg - 1, 0)
    out = _dot(jax.nn.gelu(pre), w2_ref[...])
    cmp_ref[...] = out.astype(cmp_ref.dtype)
    vT_ref[...] = out[:, LANES:].T.astype(vT_ref.dtype)


def _compress(kvc, pos_k, pos_v, k_w1, k_w2, v_w1, v_w2):
    bsz, seq, _ = kvc.shape
    nseg = seq // CMP_STRIDE
    width = CMP_STRIDE * 256
    eye = jnp.eye(4, dtype=F32)
    w1 = jnp.stack([k_w1, k_w1, v_w1, v_w1]).astype(F32).reshape(4, CMP_BLOCK, HEAD_DIM, HEAD_DIM)
    pos = jnp.stack([pos_k, pos_k, pos_v, pos_v]).astype(F32)

    def half(lo):
        w = jnp.einsum('slde,st->lsdte', w1[:, lo:lo + CMP_STRIDE], eye).reshape(width, 256)
        p = jnp.transpose(pos[:, lo:lo + CMP_STRIDE], (1, 0, 2)).reshape(1, width)
        return w.astype(MXU_DTYPE), p

    wa, pa = half(0)
    wb, pb = half(CMP_STRIDE)
    w2 = jnp.einsum('sde,st->sdte', jnp.stack([k_w2, k_w2, v_w2, v_w2]).astype(F32), eye).reshape(256, 256)
    const = lambda shape: pl.BlockSpec(shape, lambda b: (0,) * len(shape))
    return pl.pallas_call(
        _compress_kernel,
        out_shape=(jax.ShapeDtypeStruct((bsz, nseg, 256), MXU_DTYPE),
                   jax.ShapeDtypeStruct((bsz, 128, nseg), MXU_DTYPE)),
        grid=(bsz,),
        in_specs=[pl.BlockSpec((None, nseg, width), lambda b: (b, 0, 0)),
                  const((1, width)), const((1, width)), const((width, 256)), const((width, 256)),
                  const((256, 256))],
        out_specs=(pl.BlockSpec((None, nseg, 256), lambda b: (b, 0, 0)),
                   pl.BlockSpec((None, 128, nseg), lambda b: (b, 0, 0))),
        compiler_params=_cparams(("arbitrary",)),
        name="nsa_compress",
    )(kvc.reshape(bsz, nseg, width), pa, pb, wa, wb, w2.astype(MXU_DTYPE))


ONES_ROWS = 16
NQ4 = 4 * Q_BLOCK


def _with_ones(v_t):
    return jnp.concatenate([v_t, jnp.ones((ONES_ROWS, v_t.shape[1]), v_t.dtype)], axis=0)


def _online_softmax_step(s, bias, v_ext, m_ref, acc_ref):
    s = s + bias
    m_old = m_ref[...]
    m_new = jnp.maximum(m_old, jnp.max(s, axis=0, keepdims=True))
    p = jnp.exp2(s - m_new)
    acc_ref[...] = jnp.exp2(m_old - m_new) * acc_ref[...] + _dot(v_ext, p)
    m_ref[...] = m_new


def _tile4(x):
    return jnp.concatenate([x, x, x, x], axis=1)


INT_MIN = -2 ** 31
NEG_INF_KEY = int(np.int32(np.uint32(0xFF800000 ^ 0x7FFFFFFF)))


def _dsa_kernel(qT_ref, iqT_ref, miscT_ref, kv_ref, misc_ref, cT_ref, mq_ref, wuv_ref, gain_ref, tri_ref,
                o_ref, key_scr, m_scr, acc_scr, *, k_top):
    i = pl.program_id(1)
    t0 = i * Q_BLOCK
    kc = DSA_KEY_CHUNK
    nch = (t0 + Q_BLOCK + kc - 1) // kc
    t_lane = t0 + lax.broadcasted_iota(I32, (1, Q_BLOCK), 1)
    row_iota = lax.broadcasted_iota(I32, (kc, Q_BLOCK), 0)

    qcat = (_dot(mq_ref[...], qT_ref[...]) * (HEAD_DIM ** -0.5 * LOG2E)).astype(MXU_DTYPE)
    qs_t = jnp.concatenate([qcat[h * 256:(h + 1) * 256] for h in range(A_HEADS)], axis=1)
    iq_t = iqT_ref[...]
    iq_all = jnp.concatenate([iq_t[h * IDX_DIM:(h + 1) * IDX_DIM, :] for h in range(IDX_HEADS)], axis=1)
    w_t = miscT_ref[MISC_IW:MISC_IW + IDX_HEADS, :] * (IDX_HEADS ** -0.5 * IDX_DIM ** -0.5)

    def idx_body(c, _):
        k0 = pl.multiple_of(c * kc, kc)
        ik = misc_ref[pl.ds(k0, kc), MISC_IK:MISC_IK + IDX_DIM]
        d = _dot(ik, iq_all)
        score = jnp.zeros((kc, Q_BLOCK), F32)
        for h in range(IDX_HEADS):
            score = score + jnp.maximum(d[:, h * Q_BLOCK:(h + 1) * Q_BLOCK], 0.0) * w_t[h:h + 1, :]
        score = jnp.where(row_iota + k0 <= t_lane, score, -jnp.inf)
        bits = pltpu.bitcast(score, I32)
        key_scr[pl.ds(k0, kc), :] = bits ^ ((bits >> 31) & 0x7FFFFFFF)
        return 0

    lax.fori_loop(0, nch, idx_body, 0)

    def count(pred):
        def body(c, acc):
            k0 = pl.multiple_of(c * kc, kc)
            hit = pred(key_scr[pl.ds(k0, kc), :])
            return acc + jnp.sum(hit.reshape(kc // SUBLANES, SUBLANES, Q_BLOCK), axis=0)
        acc = lax.fori_loop(0, nch, body, jnp.zeros((SUBLANES, Q_BLOCK), I32))
        return jnp.sum(acc, axis=0, keepdims=True)

    def thr_bit(b, prefix):
        cand = prefix | lax.shift_left(jnp.int32(1), 31 - b)
        cand_s = cand ^ INT_MIN
        cnt = count(lambda keys: jnp.where(keys >= cand_s, 1, 0))
        return jnp.where(cnt >= k_top, cand, prefix)

    thr = lax.fori_loop(0, 32, thr_bit, jnp.zeros((1, Q_BLOCK), I32)) ^ INT_MIN
    n_gt = count(lambda keys: jnp.where(keys > thr, 1, 0))
    need = jnp.where(thr == NEG_INF_KEY, 0, k_top - n_gt).astype(F32)

    m_scr[...] = jnp.full_like(m_scr, NEG)
    acc_scr[...] = jnp.zeros_like(acc_scr)

    def att_body(c, n_tied):
        k0 = pl.multiple_of(c * kc, kc)
        s = _dot(kv_ref[pl.ds(k0, kc), :], qs_t)
        keys = key_scr[pl.ds(k0, kc), :]
        tied = _dot(tri_ref[...], jnp.where(keys == thr, 1.0, 0.0)) + n_tied
        bias = jnp.where(keys > thr, 0.0,
                         jnp.where(keys == thr, jnp.where(tied <= need, 0.0, NEG), NEG))
        v_t = jnp.concatenate([cT_ref[c * (kc // LANES) + j] for j in range(kc // LANES)], axis=1)
        _online_softmax_step(s, _tile4(bias), _with_ones(v_t), m_scr, acc_scr)
        return tied[kc - 1:kc, :]

    lax.fori_loop(0, nch, att_body, jnp.zeros((1, Q_BLOCK), F32))

    acc = acc_scr[...]
    o_lat = (acc[:A_KV_RANK] / acc[A_KV_RANK:A_KV_RANK + 1]).astype(MXU_DTYPE)
    out_t = jnp.concatenate(
        [_dot(wuv_ref[h], o_lat[:, h * Q_BLOCK:(h + 1) * Q_BLOCK]) for h in range(A_HEADS)], axis=0)
    out_t = _rms(out_t, 0) * gain_ref[...]
    o_ref[...] = out_t.T.astype(o_ref.dtype)


def _dsa(aq_t, iq_t, misc_t, akv, misc, ac_t, w_uk, w_uv, gain):
    bsz, _, seq = aq_t.shape
    nq = seq // Q_BLOCK
    k_top = min(DSA_TOPK, seq // 4)
    seq_pad = -(-seq // DSA_KEY_CHUNK) * DSA_KEY_CHUNK
    tri = jnp.asarray(np.tril(np.ones((DSA_KEY_CHUNK, DSA_KEY_CHUNK), np.float32)), MXU_DTYPE)
    mq = jnp.zeros((A_HEADS, 256, A_HEADS, HEAD_DIM), F32)
    for h in range(A_HEADS):
        mq = mq.at[h, :A_KV_RANK, h, ROPE_DIM:].set(w_uk[:, h, :].astype(F32))
        mq = mq.at[h, A_KV_RANK:A_KV_RANK + ROPE_DIM, h, :ROPE_DIM].set(jnp.eye(ROPE_DIM, dtype=F32))
    mq = mq.reshape(A_HEADS * 256, A_HEADS * HEAD_DIM).astype(MXU_DTYPE)
    wuv_t = jnp.transpose(w_uv, (1, 2, 0)).astype(MXU_DTYPE)
    per_q = lambda w: pl.BlockSpec((None, w, Q_BLOCK), lambda b, i: (b, 0, i))
    per_b = lambda *s: pl.BlockSpec((None,) + s, lambda b, i: (b,) + (0,) * len(s))
    const = lambda shape: pl.BlockSpec(shape, lambda b, i: (0,) * len(shape))
    return pl.pallas_call(
        functools.partial(_dsa_kernel, k_top=k_top),
        out_shape=jax.ShapeDtypeStruct((bsz, seq, A_HEADS * A_VDIM), MXU_DTYPE),
        grid=(bsz, nq),
        in_specs=[per_q(256), per_q(128), per_q(128), per_b(seq, 256), per_b(seq, 128),
                  per_b(seq // LANES, LANES, LANES), const(mq.shape), const(wuv_t.shape),
                  const((A_HEADS * A_VDIM, 1)), const(tri.shape)],
        out_specs=pl.BlockSpec((None, Q_BLOCK, A_HEADS * A_VDIM), lambda b, i: (b, i, 0)),
        scratch_shapes=[pltpu.VMEM((seq_pad, Q_BLOCK), I32),
                        pltpu.VMEM((1, NQ4), F32),
                        pltpu.VMEM((A_KV_RANK + ONES_ROWS, NQ4), F32)],
        compiler_params=_cparams(("arbitrary", "arbitrary")),
        name="dsa_attention",
    )(aq_t, iq_t, misc_t, akv, misc, ac_t, mq, wuv_t, gain.reshape(-1, 1), tri)


def _nsa_kernel(qT_ref, qrT_ref, miscT_ref, cmp_ref, vcT_ref, ks_ref, vsT_ref, kw_ref, vwT_ref,
                ov_ref, gain_ref, o_ref, sel_scr, m_scr, acc_scr, *, n_top, n_blk):
    i = pl.program_id(1)
    t0 = i * Q_BLOCK
    kc = NSA_KEY_CHUNK
    nch = (t0 + Q_BLOCK + kc - 1) // kc
    t_lane = t0 + lax.broadcasted_iota(I32, (1, Q_BLOCK), 1)
    gates = jax.nn.sigmoid(miscT_ref[MISC_GATE:MISC_GATE + 3 * C_HEADS, :])
    n_cmp = cmp_ref.shape[0]
    cmp_iota = lax.broadcasted_iota(I32, (n_cmp, Q_BLOCK), 0)
    blk_iota = lax.broadcasted_iota(I32, (n_blk, Q_BLOCK), 0)
    row_iota = lax.broadcasted_iota(I32, (kc, Q_BLOCK), 0)
    win_iota = lax.broadcasted_iota(I32, (Q_BLOCK, Q_BLOCK), 0)
    groups = range(C_KV_HEADS)
    gsl = [slice(g * HEAD_DIM, (g + 1) * HEAD_DIM) for g in groups]

    def heads_t(ref, g):
        return jnp.concatenate([ref[h * HEAD_DIM:(h + 1) * HEAD_DIM, :]
                                for h in range(g * C_GROUP, (g + 1) * C_GROUP)], axis=1)

    o_c = []
    for g in groups:
        s_c = _dot(cmp_ref[:, gsl[g]], heads_t(qT_ref, g))
        vis = _tile4(jnp.where(cmp_iota * CMP_STRIDE + (CMP_BLOCK - 1) <= t_lane, 1, 0)) > 0
        s_c = jnp.where(vis, s_c, NEG)
        p_c = jnp.where(vis, jnp.exp2(s_c - jnp.max(s_c, axis=0, keepdims=True)), 0.0)
        l_c = jnp.sum(p_c, axis=0, keepdims=True)
        p_c = p_c * (1.0 / jnp.maximum(l_c, 1e-30))
        o_c.append(_dot(vcT_ref[gsl[g], :], p_c))

        p_sum = (p_c[:, 0:Q_BLOCK] + p_c[:, Q_BLOCK:2 * Q_BLOCK]
                 + p_c[:, 2 * Q_BLOCK:3 * Q_BLOCK] + p_c[:, 3 * Q_BLOCK:4 * Q_BLOCK])
        p_hi = p_sum.astype(MXU_DTYPE)
        p_lo = p_sum - p_hi.astype(F32)
        imp = _dot(ov_ref[...], p_hi) + _dot(ov_ref[...], p_lo)
        cur = t_lane // SEL_BLOCK
        forced = jnp.where(blk_iota == 0, 1, jnp.where(blk_iota == cur, 1,
                           jnp.where(blk_iota == cur - 1, 1, 0)))
        imp = jnp.where(forced > 0, SEL_FORCE, imp)
        imp = jnp.where(blk_iota * SEL_BLOCK <= t_lane, imp, -jnp.inf)
        rank = jnp.zeros((n_blk, Q_BLOCK), I32)
        for mp in range(n_blk):
            row = imp[mp:mp + 1, :]
            rank = rank + jnp.where(row > imp, 1,
                                    jnp.where(row == imp, jnp.where(blk_iota > mp, 1, 0), 0))
        sel_scr[g, 0:n_blk, :] = jnp.where(rank < n_top, 0.0, NEG)

    m_scr[...] = jnp.full_like(m_scr, NEG)
    acc_scr[...] = jnp.zeros_like(acc_scr)
    bpc = kc // SEL_BLOCK

    def sel_body(c, _):
        k0 = pl.multiple_of(c * kc, kc)
        causal = jnp.where(row_iota + k0 <= t_lane, 0.0, NEG)
        for g in groups:
            s = _dot(ks_ref[pl.ds(k0, kc), gsl[g]], heads_t(qrT_ref, g))
            sel8 = sel_scr[g, pl.ds(pl.multiple_of(c * bpc, bpc), bpc), :]
            bias = causal + jnp.concatenate(
                [jnp.broadcast_to(sel8[j:j + 1, :], (SEL_BLOCK, Q_BLOCK)) for j in range(bpc)], axis=0)
            v_t = jnp.concatenate([vsT_ref[c * (kc // LANES) + j][gsl[g], :]
                                   for j in range(kc // LANES)], axis=1)
            _online_softmax_step(s, _tile4(bias), _with_ones(v_t), m_scr.at[g], acc_scr.at[g])
        return 0

    lax.fori_loop(0, nch, sel_body, 0)

    slabs = []
    for g in groups:
        acc = acc_scr[g]
        o_s = acc[:HEAD_DIM] / acc[HEAD_DIM:HEAD_DIM + 1]

        k_parts, v_parts, m_parts = [], [], []
        for j in range(WINDOW // Q_BLOCK + 1):
            cj = i - WINDOW // Q_BLOCK + j
            cjc = jnp.maximum(cj, 0)
            k0 = pl.multiple_of(cjc * Q_BLOCK, Q_BLOCK)
            k_parts.append(kw_ref[pl.ds(k0, Q_BLOCK), gsl[g]])
            v_parts.append(vwT_ref[cjc][gsl[g], :])
            kidx = win_iota + k0
            inside = jnp.where(kidx <= t_lane, jnp.where(kidx > t_lane - WINDOW, 0.0, NEG), NEG)
            m_parts.append(jnp.where(cj >= 0, inside, NEG))
        s_w = (_dot(jnp.concatenate(k_parts, axis=0), heads_t(qrT_ref, g))
               + _tile4(jnp.concatenate(m_parts, axis=0)))
        p_w = jnp.exp2(s_w - jnp.max(s_w, axis=0, keepdims=True))
        acc = _dot(_with_ones(jnp.concatenate(v_parts, axis=1)), p_w)
        o_w = acc[:HEAD_DIM] / acc[HEAD_DIM:HEAD_DIM + 1]

        for hh in range(C_GROUP):
            sl = slice(hh * Q_BLOCK, (hh + 1) * Q_BLOCK)
            r = (g * C_GROUP + hh) * 3
            slabs.append(gates[r:r + 1, :] * o_c[g][:, sl] + gates[r + 1:r + 2, :] * o_s[:, sl]
                         + gates[r + 2:r + 3, :] * o_w[:, sl])

    out_t = jnp.concatenate(slabs, axis=0)
    out_t = _rms(out_t, 0) * gain_ref[...]
    o_ref[...] = out_t.T.astype(o_ref.dtype)


def _sel_overlap_t(n_cmp_rows, n_blk):
    cs = np.arange(n_cmp_rows)[None, :] * CMP_STRIDE
    ss = np.arange(n_blk)[:, None] * SEL_BLOCK
    ov = np.minimum(cs + CMP_BLOCK, ss + SEL_BLOCK) - np.maximum(cs, ss)
    return np.clip(ov, 0, None).astype(np.float32) / CMP_BLOCK


def _nsa(cq_t, cqr_t, misc_t, cmp, vcmp_t, ks, vs_t, kw, vw_t, gain):
    bsz, _, seq = cq_t.shape
    nq = seq // Q_BLOCK
    nseg = cmp.shape[1]
    n_blk = seq // SEL_BLOCK
    n_top = min(SEL_TOPN, n_blk)
    seq_pad = -(-seq // NSA_KEY_CHUNK) * NSA_KEY_CHUNK
    ov_np = _sel_overlap_t(nseg, n_blk)
    ov_np[:, (seq - CMP_BLOCK) // CMP_STRIDE + 1:] = 0.0
    ov = jnp.asarray(ov_np, MXU_DTYPE)
    per_q = lambda w: pl.BlockSpec((None, w, Q_BLOCK), lambda b, i: (b, 0, i))
    per_b = lambda *s: pl.BlockSpec((None,) + s, lambda b, i: (b,) + (0,) * len(s))
    const = lambda shape: pl.BlockSpec(shape, lambda b, i: (0,) * len(shape))
    return pl.pallas_call(
        functools.partial(_nsa_kernel, n_top=n_top, n_blk=n_blk),
        out_shape=jax.ShapeDtypeStruct((bsz, seq, C_HEADS * HEAD_DIM), MXU_DTYPE),
        grid=(bsz, nq),
        in_specs=[per_q(512), per_q(512), per_q(128), per_b(nseg, 256), per_b(128, nseg),
                  per_b(seq, 128), per_b(seq // LANES, LANES, LANES),
                  per_b(seq, 128), per_b(seq // LANES, LANES, LANES),
                  const(ov.shape), const((C_HEADS * HEAD_DIM, 1))],
        out_specs=pl.BlockSpec((None, Q_BLOCK, C_HEADS * HEAD_DIM), lambda b, i: (b, i, 0)),
        scratch_shapes=[pltpu.VMEM((C_KV_HEADS, max(n_blk, seq_pad // SEL_BLOCK), Q_BLOCK), F32),
                        pltpu.VMEM((C_KV_HEADS, 1, NQ4), F32),
                        pltpu.VMEM((C_KV_HEADS, HEAD_DIM + ONES_ROWS, NQ4), F32)],
        compiler_params=_cparams(("arbitrary", "arbitrary")),
        name="nsa_attention",
    )(cq_t, cqr_t, misc_t, cmp, vcmp_t, ks, vs_t, kw, vw_t, ov, gain.reshape(-1, 1))


def _out_mlp_kernel(x_ref, a_ref, b_ref, c_ref, wo_ref, g2_ref, wu_ref, wd_ref, fg_ref,
                    o_ref, x1_scr, n2_scr, acc_scr, *, final_norm):
    j = pl.program_id(1)

    @pl.when(j == 0)
    def _():
        mixed = (jnp.dot(a_ref[...], wo_ref[0:256, :], preferred_element_type=F32)
                 + jnp.dot(b_ref[...], wo_ref[256:512, :], preferred_element_type=F32)
                 + jnp.dot(c_ref[...], wo_ref[512:1024, :], preferred_element_type=F32))
        x1 = x_ref[...] + mixed
        x1_scr[...] = x1
        n2_scr[...] = (_rms(x1, -1) * g2_ref[...]).astype(n2_scr.dtype)
        acc_scr[...] = jnp.zeros_like(acc_scr)

    z = jnp.dot(n2_scr[...], wu_ref[...], preferred_element_type=F32)
    acc_scr[...] += _dot(jnp.square(jnp.maximum(z, 0.0)), wd_ref[...])

    @pl.when(j == pl.num_programs(1) - 1)
    def _():
        x2 = x1_scr[...] + acc_scr[...]
        if final_norm:
            x2 = _rms(x2, -1) * fg_ref[...]
        o_ref[...] = x2


def _out_mlp(x, a_n, b_tm, c_n, w_out, ln2_g, w_up, w_down, final_g, final_norm):
    bsz, seq, _ = x.shape
    tm = min(MLP_TM, seq)
    nj = seq // tm
    nf = D_FF // MLP_TF
    rows = lambda w: pl.BlockSpec((None, tm, w), lambda r, f: (r // nj, r % nj, 0))
    const = lambda shape: pl.BlockSpec(shape, lambda r, f: (0,) * len(shape))
    return pl.pallas_call(
        functools.partial(_out_mlp_kernel, final_norm=final_norm),
        out_shape=jax.ShapeDtypeStruct(x.shape, F32),
        grid=(bsz * nj, nf),
        in_specs=[rows(D_MODEL), rows(256),
                  pl.BlockSpec((tm, B_WIDTH), lambda r, f: (r % nj, r // nj)),
                  rows(512), const((D_MODEL, D_MODEL)), const((1, D_MODEL)),
                  pl.BlockSpec((D_MODEL, MLP_TF), lambda r, f: (0, f)),
                  pl.BlockSpec((MLP_TF, D_MODEL), lambda r, f: (f, 0)),
                  const((1, D_MODEL))],
        out_specs=rows(D_MODEL),
        scratch_shapes=[pltpu.VMEM((tm, D_MODEL), F32), pltpu.VMEM((tm, D_MODEL), MXU_DTYPE),
                        pltpu.VMEM((tm, D_MODEL), F32)],
        compiler_params=_cparams(("arbitrary", "arbitrary")),
        name="out_proj_mlp",
    )(x, a_n, b_tm, c_n, w_out.astype(MXU_DTYPE), ln2_g.reshape(1, -1), w_up.astype(MXU_DTYPE),
      w_down.astype(MXU_DTYPE), final_g.reshape(1, -1))


def kernel(x, positions, ln1_g, w_in, kv_norm_g, w_uk, w_uv, s5_lambda_re, s5_lambda_im, s5_log_step, s5_b_re, s5_b_im, s5_c_re, s5_c_im, s5_d, s5_glu_w, s5_glu_b, cmp_pos_k, cmp_pos_v, cmp_k_w1, cmp_k_w2, cmp_v_w1, cmp_v_w2, gain_a, gain_b, gain_c, w_out, ln2_g, w_up, w_down, final_g):
    bsz, seq, _ = x.shape
    depth = w_in.shape[0]
    src, partner, ec_np, es_np = _proj_layout()
    ec = jnp.asarray(ec_np, MXU_DTYPE)
    es = jnp.asarray(es_np, MXU_DTYPE)
    tab = _rope_table(positions)

    def regather(w, idx):
        cols = jnp.take(w, jnp.asarray(np.maximum(idx, 0)), axis=1)
        return jnp.where(jnp.asarray(idx >= 0)[None, :], cols, 0.0).astype(MXU_DTYPE)

    for layer in range(depth):
        w_r = regather(w_in[layer], src)
        w_sw = regather(w_in[layer], partner)
        (aq_t, akv, ac_t, iq_t, misc, misc_t, u_tm, cq_t, cqr_t, kvc, ks, vs_t, kw, vw_t) = _in_proj(
            x, tab, ln1_g[layer], kv_norm_g[layer], w_r, w_sw, ec, es)

        ar, ai, bbr, bbi = _s5_discretize(s5_lambda_re[layer], s5_lambda_im[layer], s5_log_step[layer],
                                          s5_b_re[layer], s5_b_im[layer])
        b_tm = _s5(u_tm, bsz, ar, ai, bbr, bbi, s5_c_re[layer], s5_c_im[layer], s5_d[layer].reshape(-1),
                   s5_glu_w[layer], s5_glu_b[layer], gain_b[layer])

        cmp, vcmp_t = _compress(kvc, cmp_pos_k[layer], cmp_pos_v[layer], cmp_k_w1[layer],
                                cmp_k_w2[layer], cmp_v_w1[layer], cmp_v_w2[layer])
        a_n = _dsa(aq_t, iq_t, misc_t, akv, misc, ac_t, w_uk[layer], w_uv[layer], gain_a[layer])
        c_n = _nsa(cq_t, cqr_t, misc_t, cmp, vcmp_t, ks, vs_t, kw, vw_t, gain_c[layer])

        x = _out_mlp(x, a_n, b_tm, c_n, w_out[layer], ln2_g[layer], w_up[layer], w_down[layer],
                     final_g, final_norm=(layer == depth - 1))
    return x
```

```python
import functools
import math

import numpy as np
import jax
import jax.numpy as jnp
from jax import lax
from jax.experimental import pallas as pl
from jax.experimental.pallas import tpu as pltpu

F32 = jnp.float32
I32 = jnp.int32
MXU_DTYPE = jnp.bfloat16

D_MODEL = 1024
HEAD_DIM = 64
ROPE_THETA = 500000.0
ROPE_DIM = HEAD_DIM // 4
NORM_EPS = 1e-6
Q_BLOCK = 128
NEG = -1e30
LOG2E = math.log2(math.e)
D_FF = 4 * D_MODEL

A_HEADS = 4
A_NOPE = HEAD_DIM - ROPE_DIM
A_VDIM = HEAD_DIM
A_KV_RANK = 128
IDX_HEADS = 4
IDX_DIM = 32
IDX_ROPE = IDX_DIM // 4
DSA_TOPK = 256

B_WIDTH = 256
S5_GROUP = 16
S5_GROUPS = B_WIDTH // S5_GROUP
S5_STATE = 64
S5_NSTATE = S5_GROUPS * S5_STATE

C_HEADS = 8
C_KV_HEADS = 2
C_GROUP = C_HEADS // C_KV_HEADS
CMP_BLOCK = 32
CMP_STRIDE = 16
SEL_BLOCK = 64
SEL_TOPN = 16
SEL_FORCE = 1e9
WINDOW = 512

IN_SIZES = (A_HEADS * HEAD_DIM, A_KV_RANK, ROPE_DIM, IDX_HEADS * IDX_DIM, IDX_DIM, IDX_HEADS,
            B_WIDTH, C_HEADS * HEAD_DIM, 6 * C_KV_HEADS * HEAD_DIM, 3 * C_HEADS)

LANES = 128
SUBLANES = 8
VMEM_LIMIT = 56 * 1024 * 1024

SEG_AQ, SEG_AKV, SEG_IQ, SEG_MISC, SEG_BU, SEG_CQ, SEG_KVC, SEG_KS, SEG_VS, SEG_KW, SEG_VW = (
    0, 256, 512, 640, 768, 1024, 1536, 1792, 1920, 2048, 2176)
N_PROJ = 2304
MISC_IK, MISC_IW, MISC_GATE = 0, 32, 36
TAB_COS, TAB_SIN, TAB_ICOS, TAB_ISIN, TAB_ONE = 0, 8, 16, 20, 24
TAB_PART = 32

PROJ_TM = 512
DSA_KEY_CHUNK = 512
NSA_KEY_CHUNK = 1024
S5_TC = 64
MLP_TM = 1024
MLP_TF = 1024


def _cparams(sem):
    return pltpu.CompilerParams(dimension_semantics=sem, vmem_limit_bytes=VMEM_LIMIT)


def _dot(a, b):
    return jnp.dot(a.astype(MXU_DTYPE), b.astype(MXU_DTYPE), preferred_element_type=F32)


def _rms(x, axis):
    return x * lax.rsqrt(jnp.mean(x * x, axis=axis, keepdims=True) + NORM_EPS)


def _rope_tab_kernel(pos_ref, freq_ref, out_ref):
    ang = pos_ref[...].astype(F32) * freq_ref[...]
    lane = lax.broadcasted_iota(I32, ang.shape, 1)
    l32 = lane % TAB_PART
    is_cos = (l32 < TAB_SIN) | ((l32 >= TAB_ICOS) & (l32 < TAB_ISIN))
    is_sin = ((l32 >= TAB_SIN) & (l32 < TAB_ICOS)) | ((l32 >= TAB_ISIN) & (l32 < TAB_ONE))
    val = jnp.where(is_cos, jnp.cos(ang),
                    jnp.where(is_sin, jnp.sin(ang), jnp.where(l32 == TAB_ONE, 1.0, 0.0)))
    hi = val.astype(jnp.bfloat16).astype(F32)
    r1 = val - hi
    mid = r1.astype(jnp.bfloat16).astype(F32)
    lo = r1 - mid
    part = lane // TAB_PART
    out = jnp.where(part == 0, hi, jnp.where(part == 1, mid, jnp.where(part == 2, lo, 0.0)))
    out_ref[...] = out.astype(jnp.bfloat16)


def _rope_table(positions):
    bsz, seq = positions.shape
    t = bsz * seq
    inv_r = (np.float32(ROPE_THETA) ** (-np.arange(0, ROPE_DIM, 2, dtype=np.float32) / ROPE_DIM))
    inv_i = (np.float32(ROPE_THETA) ** (-np.arange(0, IDX_ROPE, 2, dtype=np.float32) / IDX_ROPE))
    f32 = np.zeros(TAB_PART, np.float32)
    f32[TAB_COS:TAB_COS + 8] = inv_r
    f32[TAB_SIN:TAB_SIN + 8] = inv_r
    f32[TAB_ICOS:TAB_ICOS + 4] = inv_i
    f32[TAB_ISIN:TAB_ISIN + 4] = inv_i
    freq = jnp.asarray(np.tile(f32, LANES // TAB_PART)[None, :])
    tq = min(1024, t)
    return pl.pallas_call(
        _rope_tab_kernel,
        out_shape=jax.ShapeDtypeStruct((t, LANES), jnp.bfloat16),
        grid=(t // tq,),
        in_specs=[pl.BlockSpec((tq, 1), lambda i: (i, 0)),
                  pl.BlockSpec((1, LANES), lambda i: (0, 0))],
        out_specs=pl.BlockSpec((tq, LANES), lambda i: (i, 0)),
        compiler_params=_cparams(("arbitrary",)),
        name="rope_table",
    )(positions.reshape(t, 1), freq).reshape(bsz, seq, LANES)


PAT_HEAD, PAT_KROPE, PAT_IDX, PAT_MISC = range(4)
N_PAT = 4


def _proj_layout():
    offs = np.concatenate([[0], np.cumsum(IN_SIZES)])
    o_aq, o_ckv, o_kr, o_iq, o_ik, o_iw, o_bu, o_cq, o_kv, o_gate = offs[:10]
    src = -np.ones(N_PROJ, np.int64)
    partner = -np.ones(N_PROJ, np.int64)
    clane = np.full(N_PROJ, TAB_ONE, np.int64)
    slane = -np.ones(N_PROJ, np.int64)
    ssign = np.zeros(N_PROJ, np.float32)

    def plain(c0, o0, w):
        src[c0:c0 + w] = np.arange(o0, o0 + w)

    def rope(c0, o0, half, cos_lane, sin_lane):
        for j in range(half):
            partner[c0 + j] = o0 + half + j
            partner[c0 + half + j] = o0 + j
            clane[c0 + j] = clane[c0 + half + j] = cos_lane + j
            slane[c0 + j] = slane[c0 + half + j] = sin_lane + j
            ssign[c0 + j] = -1.0
            ssign[c0 + half + j] = 1.0

    plain(SEG_AQ, o_aq, A_HEADS * HEAD_DIM)
    for h in range(A_HEADS):
        rope(SEG_AQ + h * HEAD_DIM, o_aq + h * HEAD_DIM, ROPE_DIM // 2, TAB_COS, TAB_SIN)
    plain(SEG_AKV, o_ckv, A_KV_RANK)
    plain(SEG_AKV + A_KV_RANK, o_kr, ROPE_DIM)
    rope(SEG_AKV + A_KV_RANK, o_kr, ROPE_DIM // 2, TAB_COS, TAB_SIN)
    plain(SEG_IQ, o_iq, IDX_HEADS * IDX_DIM)
    for h in range(IDX_HEADS):
        rope(SEG_IQ + h * IDX_DIM, o_iq + h * IDX_DIM, IDX_ROPE // 2, TAB_ICOS, TAB_ISIN)
    plain(SEG_MISC + MISC_IK, o_ik, IDX_DIM)
    rope(SEG_MISC + MISC_IK, o_ik, IDX_ROPE // 2, TAB_ICOS, TAB_ISIN)
    plain(SEG_MISC + MISC_IW, o_iw, IDX_HEADS)
    plain(SEG_MISC + MISC_GATE, o_gate, 3 * C_HEADS)
    plain(SEG_BU, o_bu, B_WIDTH)
    plain(SEG_CQ, o_cq, C_HEADS * HEAD_DIM)
    for h in range(C_HEADS):
        rope(SEG_CQ + h * HEAD_DIM, o_cq + h * HEAD_DIM, ROPE_DIM // 2, TAB_COS, TAB_SIN)
    plain(SEG_KVC, o_kv, 6 * C_KV_HEADS * HEAD_DIM)
    for seg, sub in ((SEG_KS, 2), (SEG_KW, 4)):
        for g in range(C_KV_HEADS):
            rope(seg + g * HEAD_DIM, o_kv + sub * C_KV_HEADS * HEAD_DIM + g * HEAD_DIM,
                 ROPE_DIM // 2, TAB_COS, TAB_SIN)

    k = np.arange(LANES)[:, None]
    live = k < 3 * TAB_PART
    ec = (live & ((k % TAB_PART) == clane[None, :])).astype(np.float32)
    es = (live & ((k % TAB_PART) == slane[None, :])).astype(np.float32) * ssign[None, :]
    starts = {PAT_HEAD: SEG_AQ, PAT_KROPE: SEG_AKV + A_KV_RANK, PAT_IDX: SEG_IQ, PAT_MISC: SEG_MISC}
    blocks = [m[:, starts[p]:starts[p] + LANES] for m in (ec, es) for p in range(N_PAT)]
    return src, partner, np.concatenate(blocks, axis=1)


def _in_proj_kernel(x_ref, tab_ref, g_ref, kvg_ref, w_ref, wsw_ref, pat_ref,
                    aqT_ref, akv_ref, acT_ref, iqT_ref, misc_ref, miscT_ref, u_ref,
                    cqT_ref, cqrT_ref, kvc_ref, ks_ref, vsT_ref, kw_ref, vwT_ref):
    x = x_ref[...]
    n = (_rms(x, -1) * g_ref[...]).astype(MXU_DTYPE)
    cs = jnp.dot(tab_ref[...], pat_ref[...], preferred_element_type=F32)

    def plain(c0, w):
        return jnp.dot(n, w_ref[:, c0:c0 + w], preferred_element_type=F32)

    def rotate(p, c0, pat):
        nb = p.shape[1] // LANES
        psw = jnp.dot(n, wsw_ref[:, c0:c0 + p.shape[1]], preferred_element_type=F32)
        cos = cs[:, pat * LANES:(pat + 1) * LANES]
        sin = cs[:, (N_PAT + pat) * LANES:(N_PAT + pat + 1) * LANES]
        return p * jnp.concatenate([cos] * nb, axis=1) + psw * jnp.concatenate([sin] * nb, axis=1)

    def chunked_t(val, ref):
        vt = val.T.astype(ref.dtype)
        for c in range(ref.shape[0]):
            ref[c] = vt[:, c * LANES:(c + 1) * LANES]

    aqT_ref[...] = rotate(plain(SEG_AQ, 256), SEG_AQ, PAT_HEAD).T.astype(aqT_ref.dtype)

    akv = plain(SEG_AKV, 256)
    lat = _rms(akv[:, :A_KV_RANK], -1) * kvg_ref[...]
    k_rope = rotate(akv[:, A_KV_RANK:], SEG_AKV + A_KV_RANK, PAT_KROPE)
    akv_ref[...] = jnp.concatenate([lat, k_rope], axis=1).astype(akv_ref.dtype)
    chunked_t(lat, acT_ref)

    iqT_ref[...] = rotate(plain(SEG_IQ, 128), SEG_IQ, PAT_IDX).T.astype(iqT_ref.dtype)
    misc = rotate(plain(SEG_MISC, 128), SEG_MISC, PAT_MISC)
    misc_ref[...] = misc
    miscT_ref[...] = misc.T
    u_ref[...] = plain(SEG_BU, 256)
    qk_scale = HEAD_DIM ** -0.5 * LOG2E
    cq = plain(SEG_CQ, 512)
    cqT_ref[...] = (cq * qk_scale).T.astype(cqT_ref.dtype)
    cqrT_ref[...] = (rotate(cq, SEG_CQ, PAT_HEAD) * qk_scale).T.astype(cqrT_ref.dtype)
    kvc_ref[...] = plain(SEG_KVC, 256)
    ks_ref[...] = rotate(plain(SEG_KS, 128), SEG_KS, PAT_HEAD).astype(ks_ref.dtype)
    chunked_t(plain(SEG_VS, 128), vsT_ref)
    kw_ref[...] = rotate(plain(SEG_KW, 128), SEG_KW, PAT_HEAD).astype(kw_ref.dtype)
    chunked_t(plain(SEG_VW, 128), vwT_ref)


def _in_proj(x, tab, ln_g, kv_g, w_r, w_sw, pat):
    bsz, seq, _ = x.shape
    tm = min(PROJ_TM, seq)
    nj = seq // tm
    nck = tm // LANES
    bf = MXU_DTYPE

    def rows(w):
        return pl.BlockSpec((None, tm, w), lambda b, j: (b, j, 0))

    def cols(w):
        return pl.BlockSpec((None, w, tm), lambda b, j: (b, 0, j))

    def chunks():
        return pl.BlockSpec((None, nck, LANES, LANES), lambda b, j: (b, j, 0, 0))

    def const(shape):
        return pl.BlockSpec(shape, lambda b, j: (0,) * len(shape))

    sds = jax.ShapeDtypeStruct
    out_shape = (
        sds((bsz, 256, seq), bf),
        sds((bsz, seq, 256), bf),
        sds((bsz, seq // LANES, LANES, LANES), bf),
        sds((bsz, 128, seq), bf),
        sds((bsz, seq, 128), F32),
        sds((bsz, 128, seq), F32),
        sds((seq, bsz * B_WIDTH), F32),
        sds((bsz, 512, seq), bf),
        sds((bsz, 512, seq), bf),
        sds((bsz, seq, 256), F32),
        sds((bsz, seq, 128), bf),
        sds((bsz, seq // LANES, LANES, LANES), bf),
        sds((bsz, seq, 128), bf),
        sds((bsz, seq // LANES, LANES, LANES), bf),
    )
    out_specs = (cols(256), rows(256), chunks(), cols(128), rows(128), cols(128),
                 pl.BlockSpec((tm, B_WIDTH), lambda b, j: (j, b)),
                 cols(512), cols(512), rows(256), rows(128), chunks(), rows(128), chunks())
    return pl.pallas_call(
        _in_proj_kernel,
        out_shape=out_shape,
        grid=(bsz, nj),
        in_specs=[rows(D_MODEL), rows(LANES), const((1, D_MODEL)), const((1, A_KV_RANK)),
                  const((D_MODEL, N_PROJ)), const((D_MODEL, N_PROJ)),
                  const((LANES, 2 * N_PAT * LANES))],
        out_specs=out_specs,
        compiler_params=_cparams(("arbitrary", "arbitrary")),
        name="in_proj",
    )(x, tab, ln_g.reshape(1, -1), kv_g.reshape(1, -1), w_r, w_sw, pat)


def _s5_disc_kernel(lr_ref, li_ref, ls_ref, br_ref, bi_ref, ar_ref, ai_ref, bbr_ref, bbi_ref):
    lr, li = lr_ref[...], li_ref[...]
    step = jnp.exp(ls_ref[...])
    mag = jnp.exp(lr * step)
    ar = mag * jnp.cos(li * step)
    ai = mag * jnp.sin(li * step)
    den = lr * lr + li * li
    zr = ((ar - 1.0) * lr + ai * li) / den
    zi = (ai * lr - (ar - 1.0) * li) / den
    br, bi = br_ref[...], bi_ref[...]
    ar_ref[...] = ar
    ai_ref[...] = ai
    bbr_ref[...] = zr * br - zi * bi
    bbi_ref[...] = zr * bi + zi * br


def _s5_discretize(lam_re, lam_im, log_step, b_re, b_im):
    g, p, h = b_re.shape
    ex = lambda a: jnp.repeat(a.astype(F32), h, axis=1)
    ls = jnp.broadcast_to(log_step.astype(F32)[:, None], (g, p * h))
    sds = jax.ShapeDtypeStruct((g, p * h), F32)
    ar, ai, bbr, bbi = pl.pallas_call(
        _s5_disc_kernel, out_shape=(sds, sds, sds, sds), name="s5_discretize",
    )(ex(lam_re), ex(lam_im), ls, b_re.astype(F32).reshape(g, p * h), b_im.astype(F32).reshape(g, p * h))
    ar = ar.reshape(g, p, h)[:, :, 0]
    ai = ai.reshape(g, p, h)[:, :, 0]
    return ar, ai, bbr.reshape(g, p, h), bbi.reshape(g, p, h)


def _s5_kernel(u_ref, bmat_ref, a_ref, cre_ref, cim_ref, d_ref, gw_ref, gb_ref, gain_ref,
               o_ref, u_scr, x_scr, h_scr, *, tc):
    ns = S5_NSTATE

    @pl.when(pl.program_id(0) == 0)
    def _():
        h_scr[...] = jnp.zeros_like(h_scr)

    halves = range(B_WIDTH // LANES)
    for b in range(SUBLANES):
        for hf in halves:
            c0 = b * B_WIDTH + hf * LANES
            u_scr[hf, pl.ds(b, tc, stride=SUBLANES), :] = u_ref[:, c0:c0 + LANES]
    u = jnp.concatenate([u_scr[hf] for hf in halves], axis=1)
    x_scr[...] = _dot(u, bmat_ref[...])
    ar = a_ref[0:SUBLANES, :]
    ai = a_ref[SUBLANES:2 * SUBLANES, :]

    def step(t, carry):
        hr, hi = carry
        r0 = pl.multiple_of(t * SUBLANES, SUBLANES)
        xr = x_scr[pl.ds(r0, SUBLANES), 0:ns]
        xi = x_scr[pl.ds(r0, SUBLANES), ns:2 * ns]
        nhr = ar * hr - ai * hi + xr
        nhi = ar * hi + ai * hr + xi
        x_scr[pl.ds(r0, SUBLANES), 0:ns] = nhr
        x_scr[pl.ds(r0, SUBLANES), ns:2 * ns] = nhi
        return nhr, nhi

    hr, hi = lax.fori_loop(0, tc, step, (h_scr[:, 0:ns], h_scr[:, ns:2 * ns]), unroll=8)
    h_scr[:, 0:ns] = hr
    h_scr[:, ns:2 * ns] = hi

    y = (_dot(x_scr[:, 0:ns], cre_ref[...]) - _dot(x_scr[:, ns:2 * ns], cim_ref[...])
         + d_ref[...] * u)
    y = jax.nn.gelu(y)
    y = y * jax.nn.sigmoid(_dot(y, gw_ref[...]) + gb_ref[...])
    y = _rms(y, -1) * gain_ref[...]
    for hf in halves:
        u_scr[hf] = y[:, hf * LANES:(hf + 1) * LANES]
    for b in range(SUBLANES):
        for hf in halves:
            c0 = b * B_WIDTH + hf * LANES
            o_ref[:, c0:c0 + LANES] = u_scr[hf, pl.ds(b, tc, stride=SUBLANES), :].astype(o_ref.dtype)


def _s5(u_tm, bsz, ar, ai, bbr, bbi, c_re, c_im, d, glu_w, glu_b, gain):
    assert bsz == SUBLANES, "the S5 scan keeps one batch row per sublane"
    seq = u_tm.shape[0]
    tc = min(S5_TC, seq)
    rows = tc * bsz
    eye = jnp.eye(S5_GROUPS, dtype=F32)
    bmat = jnp.concatenate(
        [jnp.einsum('gph,gk->ghkp', b, eye).reshape(B_WIDTH, S5_NSTATE) for b in (bbr, bbi)], axis=1)
    cre = jnp.einsum('ghp,gk->gpkh', c_re.astype(F32), eye).reshape(S5_NSTATE, B_WIDTH)
    cim = jnp.einsum('ghp,gk->gpkh', c_im.astype(F32), eye).reshape(S5_NSTATE, B_WIDTH)
    avec = jnp.concatenate([jnp.broadcast_to(a.reshape(1, S5_NSTATE), (SUBLANES, S5_NSTATE))
                            for a in (ar, ai)], axis=0)
    const = lambda shape: pl.BlockSpec(shape, lambda i: (0,) * len(shape))
    out = pl.pallas_call(
        functools.partial(_s5_kernel, tc=tc),
        out_shape=jax.ShapeDtypeStruct((seq, bsz * B_WIDTH), MXU_DTYPE),
        grid=(seq // tc,),
        in_specs=[pl.BlockSpec((tc, bsz * B_WIDTH), lambda i: (i, 0)),
                  const((B_WIDTH, 2 * S5_NSTATE)), const((2 * SUBLANES, S5_NSTATE)),
                  const((S5_NSTATE, B_WIDTH)), const((S5_NSTATE, B_WIDTH)),
                  const((1, B_WIDTH)), const((B_WIDTH, B_WIDTH)), const((1, B_WIDTH)),
                  const((1, B_WIDTH))],
        out_specs=pl.BlockSpec((tc, bsz * B_WIDTH), lambda i: (i, 0)),
        scratch_shapes=[pltpu.VMEM((B_WIDTH // LANES, rows, LANES), F32),
                        pltpu.VMEM((rows, 2 * S5_NSTATE), F32),
                        pltpu.VMEM((SUBLANES, 2 * S5_NSTATE), F32)],
        compiler_params=_cparams(("arbitrary",)),
        name="s5_scan",
    )(u_tm, bmat.astype(MXU_DTYPE), avec,
      cre.astype(MXU_DTYPE), cim.astype(MXU_DTYPE), d.astype(F32).reshape(1, B_WIDTH),
      glu_w.astype(MXU_DTYPE), glu_b.astype(F32).reshape(1, B_WIDTH), gain.reshape(1, B_WIDTH))
    return out


def _compress_kernel(k_ref, v_ref, pa_ref, pb_ref, wa_ref, wb_ref, w2_ref, cmp_ref, vT_ref):
    nseg = cmp_ref.shape[0]
    cw = k_ref.shape[1] + v_ref.shape[1]
    xa = jnp.zeros((nseg, cw), F32)
    xb = jnp.zeros((nseg, cw), F32)
    for l in range(CMP_STRIDE):
        rows = pl.ds(l, nseg, stride=CMP_STRIDE)
        tok = jnp.concatenate([k_ref[rows, :], v_ref[rows, :]], axis=1)
        cols = slice(l * cw, (l + 1) * cw)
        xa = xa + _dot(tok + pa_ref[:, cols], wa_ref[cols, :])
        xb = xb + _dot(tok + pb_ref[:, cols], wb_ref[cols, :])
    pre = xa + pltpu.roll(xb, nseg - 1, 0)
    out = _dot(jax.nn.gelu(pre), w2_ref[...])
    cmp_ref[...] = out.astype(cmp_ref.dtype)
    vT_ref[...] = out[:, LANES:].T.astype(vT_ref.dtype)


def _compress(kvc, pos_k, pos_v, k_w1, k_w2, v_w1, v_w2):
    bsz, seq, _ = kvc.shape
    nseg = seq // CMP_STRIDE
    width = CMP_STRIDE * 256
    eye = jnp.eye(4, dtype=F32)
    w1 = jnp.stack([k_w1, k_w1, v_w1, v_w1]).astype(F32).reshape(4, CMP_BLOCK, HEAD_DIM, HEAD_DIM)
    pos = jnp.stack([pos_k, pos_k, pos_v, pos_v]).astype(F32)

    def half(lo):
        w = jnp.einsum('slde,st->lsdte', w1[:, lo:lo + CMP_STRIDE], eye).reshape(width, 256)
        p = jnp.transpose(pos[:, lo:lo + CMP_STRIDE], (1, 0, 2)).reshape(1, width)
        return w.astype(MXU_DTYPE), p

    wa, pa = half(0)
    wb, pb = half(CMP_STRIDE)
    w2 = jnp.einsum('sde,st->sdte', jnp.stack([k_w2, k_w2, v_w2, v_w2]).astype(F32), eye).reshape(256, 256)
    const = lambda shape: pl.BlockSpec(shape, lambda b: (0,) * len(shape))
    return pl.pallas_call(
        _compress_kernel,
        out_shape=(jax.ShapeDtypeStruct((bsz, nseg, 256), MXU_DTYPE),
                   jax.ShapeDtypeStruct((bsz, 128, nseg), MXU_DTYPE)),
        grid=(bsz,),
        in_specs=[pl.BlockSpec((None, seq, LANES), lambda b: (b, 0, 0)),
                  pl.BlockSpec((None, seq, LANES), lambda b: (b, 0, 1)),
                  const((1, width)), const((1, width)), const((width, 256)), const((width, 256)),
                  const((256, 256))],
        out_specs=(pl.BlockSpec((None, nseg, 256), lambda b: (b, 0, 0)),
                   pl.BlockSpec((None, 128, nseg), lambda b: (b, 0, 0))),
        compiler_params=_cparams(("arbitrary",)),
        name="nsa_compress",
    )(kvc, kvc, pa, pb, wa, wb, w2.astype(MXU_DTYPE))


ONES_ROWS = 16
NQ4 = 4 * Q_BLOCK


def _with_ones(v_t):
    return jnp.concatenate([v_t, jnp.ones((ONES_ROWS, v_t.shape[1]), v_t.dtype)], axis=0)


def _online_softmax_step(s, bias, v_ext, m_ref, acc_ref):
    s = s + bias
    m_old = m_ref[...]
    m_new = jnp.maximum(m_old, jnp.max(s, axis=0, keepdims=True))
    p = jnp.exp2(s - m_new)
    acc_ref[...] = jnp.exp2(m_old - m_new) * acc_ref[...] + _dot(v_ext, p)
    m_ref[...] = m_new


def _tile4(x):
    return jnp.concatenate([x, x, x, x], axis=1)


INT_MIN = -2 ** 31
NEG_INF_KEY = int(np.int32(np.uint32(0xFF800000 ^ 0x7FFFFFFF)))


def _dsa_kernel(qT_ref, iqT_ref, miscT_ref, kv_ref, misc_ref, cT_ref, mq_ref, wuv_ref, gain_ref, tri_ref,
                o_ref, key_scr, m_scr, acc_scr, *, k_top):
    i = pl.program_id(1)
    t0 = i * Q_BLOCK
    kc = DSA_KEY_CHUNK
    nch = (t0 + Q_BLOCK + kc - 1) // kc
    t_lane = t0 + lax.broadcasted_iota(I32, (1, Q_BLOCK), 1)
    row_iota = lax.broadcasted_iota(I32, (kc, Q_BLOCK), 0)

    qcat = (_dot(mq_ref[...], qT_ref[...]) * (HEAD_DIM ** -0.5 * LOG2E)).astype(MXU_DTYPE)
    qs_t = jnp.concatenate([qcat[h * 256:(h + 1) * 256] for h in range(A_HEADS)], axis=1)
    iq_t = iqT_ref[...]
    iq_all = jnp.concatenate([iq_t[h * IDX_DIM:(h + 1) * IDX_DIM, :] for h in range(IDX_HEADS)], axis=1)
    w_t = miscT_ref[MISC_IW:MISC_IW + IDX_HEADS, :] * (IDX_HEADS ** -0.5 * IDX_DIM ** -0.5)

    def idx_body(c, _):
        k0 = pl.multiple_of(c * kc, kc)
        ik = misc_ref[pl.ds(k0, kc), MISC_IK:MISC_IK + IDX_DIM]
        d = _dot(ik, iq_all)
        score = jnp.zeros((kc, Q_BLOCK), F32)
        for h in range(IDX_HEADS):
            score = score + jnp.maximum(d[:, h * Q_BLOCK:(h + 1) * Q_BLOCK], 0.0) * w_t[h:h + 1, :]
        score = jnp.where(row_iota + k0 <= t_lane, score, -jnp.inf)
        bits = pltpu.bitcast(score, I32)
        key_scr[pl.ds(k0, kc), :] = bits ^ ((bits >> 31) & 0x7FFFFFFF)
        return 0

    lax.fori_loop(0, nch, idx_body, 0)

    def count(pred):
        def body(c, acc):
            k0 = pl.multiple_of(c * kc, kc)
            hit = pred(key_scr[pl.ds(k0, kc), :])
            return acc + jnp.sum(hit.reshape(kc // SUBLANES, SUBLANES, Q_BLOCK), axis=0)
        acc = lax.fori_loop(0, nch, body, jnp.zeros((SUBLANES, Q_BLOCK), I32))
        return jnp.sum(acc, axis=0, keepdims=True)

    def thr_bit(b, prefix):
        cand = prefix | lax.shift_left(jnp.int32(1), 31 - b)
        cand_s = cand ^ INT_MIN
        cnt = count(lambda keys: jnp.where(keys >= cand_s, 1, 0))
        return jnp.where(cnt >= k_top, cand, prefix)

    thr = lax.fori_loop(0, 32, thr_bit, jnp.zeros((1, Q_BLOCK), I32)) ^ INT_MIN
    n_gt = count(lambda keys: jnp.where(keys > thr, 1, 0))
    need = jnp.where(thr == NEG_INF_KEY, 0, k_top - n_gt).astype(F32)

    m_scr[...] = jnp.full_like(m_scr, NEG)
    acc_scr[...] = jnp.zeros_like(acc_scr)

    def att_body(c, n_tied):
        k0 = pl.multiple_of(c * kc, kc)
        s = _dot(kv_ref[pl.ds(k0, kc), :], qs_t)
        keys = key_scr[pl.ds(k0, kc), :]
        tied = _dot(tri_ref[...], jnp.where(keys == thr, 1.0, 0.0)) + n_tied
        bias = jnp.where(keys > thr, 0.0,
                         jnp.where(keys == thr, jnp.where(tied <= need, 0.0, NEG), NEG))
        v_t = jnp.concatenate([cT_ref[c * (kc // LANES) + j] for j in range(kc // LANES)], axis=1)
        _online_softmax_step(s, _tile4(bias), _with_ones(v_t), m_scr, acc_scr)
        return tied[kc - 1:kc, :]

    lax.fori_loop(0, nch, att_body, jnp.zeros((1, Q_BLOCK), F32))

    acc = acc_scr[...]
    o_lat = (acc[:A_KV_RANK] / acc[A_KV_RANK:A_KV_RANK + 1]).astype(MXU_DTYPE)
    out_t = jnp.concatenate(
        [_dot(wuv_ref[h], o_lat[:, h * Q_BLOCK:(h + 1) * Q_BLOCK]) for h in range(A_HEADS)], axis=0)
    out_t = _rms(out_t, 0) * gain_ref[...]
    o_ref[...] = out_t.T.astype(o_ref.dtype)


def _dsa(aq_t, iq_t, misc_t, akv, misc, ac_t, w_uk, w_uv, gain):
    bsz, _, seq = aq_t.shape
    nq = seq // Q_BLOCK
    k_top = min(DSA_TOPK, seq // 4)
    seq_pad = -(-seq // DSA_KEY_CHUNK) * DSA_KEY_CHUNK
    tri = jnp.asarray(np.tril(np.ones((DSA_KEY_CHUNK, DSA_KEY_CHUNK), np.float32)), MXU_DTYPE)
    mq = jnp.zeros((A_HEADS, 256, A_HEADS, HEAD_DIM), F32)
    for h in range(A_HEADS):
        mq = mq.at[h, :A_KV_RANK, h, ROPE_DIM:].set(w_uk[:, h, :].astype(F32))
        mq = mq.at[h, A_KV_RANK:A_KV_RANK + ROPE_DIM, h, :ROPE_DIM].set(jnp.eye(ROPE_DIM, dtype=F32))
    mq = mq.reshape(A_HEADS * 256, A_HEADS * HEAD_DIM).astype(MXU_DTYPE)
    wuv_t = jnp.transpose(w_uv, (1, 2, 0)).astype(MXU_DTYPE)
    per_q = lambda w: pl.BlockSpec((None, w, Q_BLOCK), lambda b, i: (b, 0, i))
    per_b = lambda *s: pl.BlockSpec((None,) + s, lambda b, i: (b,) + (0,) * len(s))
    const = lambda shape: pl.BlockSpec(shape, lambda b, i: (0,) * len(shape))
    return pl.pallas_call(
        functools.partial(_dsa_kernel, k_top=k_top),
        out_shape=jax.ShapeDtypeStruct((bsz, seq, A_HEADS * A_VDIM), MXU_DTYPE),
        grid=(bsz, nq),
        in_specs=[per_q(256), per_q(128), per_q(128), per_b(seq, 256), per_b(seq, 128),
                  per_b(seq // LANES, LANES, LANES), const(mq.shape), const(wuv_t.shape),
                  const((A_HEADS * A_VDIM, 1)), const(tri.shape)],
        out_specs=pl.BlockSpec((None, Q_BLOCK, A_HEADS * A_VDIM), lambda b, i: (b, i, 0)),
        scratch_shapes=[pltpu.VMEM((seq_pad, Q_BLOCK), I32),
                        pltpu.VMEM((1, NQ4), F32),
                        pltpu.VMEM((A_KV_RANK + ONES_ROWS, NQ4), F32)],
        compiler_params=_cparams(("arbitrary", "arbitrary")),
        name="dsa_attention",
    )(aq_t, iq_t, misc_t, akv, misc, ac_t, mq, wuv_t, gain.reshape(-1, 1), tri)


def _nsa_kernel(qT_ref, qrT_ref, miscT_ref, cmp_ref, vcT_ref, ks_ref, vsT_ref, kw_ref, vwT_ref,
                ov_ref, gain_ref, o_ref, sel_scr, m_scr, acc_scr, *, n_top, n_blk):
    i = pl.program_id(1)
    t0 = i * Q_BLOCK
    kc = NSA_KEY_CHUNK
    nch = (t0 + Q_BLOCK + kc - 1) // kc
    t_lane = t0 + lax.broadcasted_iota(I32, (1, Q_BLOCK), 1)
    gates = jax.nn.sigmoid(miscT_ref[MISC_GATE:MISC_GATE + 3 * C_HEADS, :])
    n_cmp = cmp_ref.shape[0]
    cmp_iota = lax.broadcasted_iota(I32, (n_cmp, Q_BLOCK), 0)
    blk_iota = lax.broadcasted_iota(I32, (n_blk, Q_BLOCK), 0)
    row_iota = lax.broadcasted_iota(I32, (kc, Q_BLOCK), 0)
    win_iota = lax.broadcasted_iota(I32, (Q_BLOCK, Q_BLOCK), 0)
    groups = range(C_KV_HEADS)
    gsl = [slice(g * HEAD_DIM, (g + 1) * HEAD_DIM) for g in groups]

    def heads_t(ref, g):
        return jnp.concatenate([ref[h * HEAD_DIM:(h + 1) * HEAD_DIM, :]
                                for h in range(g * C_GROUP, (g + 1) * C_GROUP)], axis=1)

    o_c = []
    for g in groups:
        s_c = _dot(cmp_ref[:, gsl[g]], heads_t(qT_ref, g))
        vis = _tile4(jnp.where(cmp_iota * CMP_STRIDE + (CMP_BLOCK - 1) <= t_lane, 1, 0)) > 0
        s_c = jnp.where(vis, s_c, NEG)
        p_c = jnp.where(vis, jnp.exp2(s_c - jnp.max(s_c, axis=0, keepdims=True)), 0.0)
        l_c = jnp.sum(p_c, axis=0, keepdims=True)
        p_c = p_c * (1.0 / jnp.maximum(l_c, 1e-30))
        o_c.append(_dot(vcT_ref[gsl[g], :], p_c))

        p_sum = (p_c[:, 0:Q_BLOCK] + p_c[:, Q_BLOCK:2 * Q_BLOCK]
                 + p_c[:, 2 * Q_BLOCK:3 * Q_BLOCK] + p_c[:, 3 * Q_BLOCK:4 * Q_BLOCK])
        p_hi = p_sum.astype(MXU_DTYPE)
        p_lo = p_sum - p_hi.astype(F32)
        imp = _dot(ov_ref[...], p_hi) + _dot(ov_ref[...], p_lo)
        cur = t_lane // SEL_BLOCK
        forced = jnp.where(blk_iota == 0, 1, jnp.where(blk_iota == cur, 1,
                           jnp.where(blk_iota == cur - 1, 1, 0)))
        imp = jnp.where(forced > 0, SEL_FORCE, imp)
        imp = jnp.where(blk_iota * SEL_BLOCK <= t_lane, imp, -jnp.inf)
        n_grp = n_blk // SUBLANES
        imp_g = [imp[v * SUBLANES:(v + 1) * SUBLANES, :] for v in range(n_grp)]
        rank_g = [jnp.zeros((SUBLANES, Q_BLOCK), I32) for _ in range(n_grp)]
        for mp in range(n_blk):
            row = imp[mp:mp + 1, :]
            for v in range(n_grp):
                if v > mp // SUBLANES:
                    beats = jnp.where(row >= imp_g[v], 1, 0)
                elif v < mp // SUBLANES:
                    beats = jnp.where(row > imp_g[v], 1, 0)
                else:
                    later = blk_iota[v * SUBLANES:(v + 1) * SUBLANES, :] > mp
                    beats = jnp.where(row > imp_g[v], 1, jnp.where(row == imp_g[v], jnp.where(later, 1, 0), 0))
                rank_g[v] = rank_g[v] + beats
        rank = jnp.concatenate(rank_g, axis=0)
        sel_scr[g, 0:n_blk, :] = jnp.where(rank < n_top, 0.0, NEG)

    m_scr[...] = jnp.full_like(m_scr, NEG)
    acc_scr[...] = jnp.zeros_like(acc_scr)
    bpc = kc // SEL_BLOCK

    def sel_body(c, _):
        k0 = pl.multiple_of(c * kc, kc)
        causal = jnp.where(row_iota + k0 <= t_lane, 0.0, NEG)
        for g in groups:
            s = _dot(ks_ref[pl.ds(k0, kc), gsl[g]], heads_t(qrT_ref, g))
            sel8 = sel_scr[g, pl.ds(pl.multiple_of(c * bpc, bpc), bpc), :]
            bias = causal + jnp.concatenate(
                [jnp.broadcast_to(sel8[j:j + 1, :], (SEL_BLOCK, Q_BLOCK)) for j in range(bpc)], axis=0)
            v_t = jnp.concatenate([vsT_ref[c * (kc // LANES) + j][gsl[g], :]
                                   for j in range(kc // LANES)], axis=1)
            _online_softmax_step(s, _tile4(bias), _with_ones(v_t), m_scr.at[g], acc_scr.at[g])
        return 0

    lax.fori_loop(0, nch, sel_body, 0)

    slabs = []
    for g in groups:
        acc = acc_scr[g]
        o_s = acc[:HEAD_DIM] / acc[HEAD_DIM:HEAD_DIM + 1]

        k_parts, v_parts, m_parts = [], [], []
        for j in range(WINDOW // Q_BLOCK + 1):
            cj = i - WINDOW // Q_BLOCK + j
            cjc = jnp.maximum(cj, 0)
            k0 = pl.multiple_of(cjc * Q_BLOCK, Q_BLOCK)
            k_parts.append(kw_ref[pl.ds(k0, Q_BLOCK), gsl[g]])
            v_parts.append(vwT_ref[cjc][gsl[g], :])
            kidx = win_iota + k0
            inside = jnp.where(kidx <= t_lane, jnp.where(kidx > t_lane - WINDOW, 0.0, NEG), NEG)
            m_parts.append(jnp.where(cj >= 0, inside, NEG))
        s_w = (_dot(jnp.concatenate(k_parts, axis=0), heads_t(qrT_ref, g))
               + _tile4(jnp.concatenate(m_parts, axis=0)))
        p_w = jnp.exp2(s_w - jnp.max(s_w, axis=0, keepdims=True))
        acc = _dot(_with_ones(jnp.concatenate(v_parts, axis=1)), p_w)
        o_w = acc[:HEAD_DIM] / acc[HEAD_DIM:HEAD_DIM + 1]

        for hh in range(C_GROUP):
            sl = slice(hh * Q_BLOCK, (hh + 1) * Q_BLOCK)
            r = (g * C_GROUP + hh) * 3
            slabs.append(gates[r:r + 1, :] * o_c[g][:, sl] + gates[r + 1:r + 2, :] * o_s[:, sl]
                         + gates[r + 2:r + 3, :] * o_w[:, sl])

    out_t = jnp.concatenate(slabs, axis=0)
    out_t = _rms(out_t, 0) * gain_ref[...]
    o_ref[...] = out_t.T.astype(o_ref.dtype)


def _sel_overlap_t(n_cmp_rows, n_blk):
    cs = np.arange(n_cmp_rows)[None, :] * CMP_STRIDE
    ss = np.arange(n_blk)[:, None] * SEL_BLOCK
    ov = np.minimum(cs + CMP_BLOCK, ss + SEL_BLOCK) - np.maximum(cs, ss)
    return np.clip(ov, 0, None).astype(np.float32) / CMP_BLOCK


def _nsa(cq_t, cqr_t, misc_t, cmp, vcmp_t, ks, vs_t, kw, vw_t, gain):
    bsz, _, seq = cq_t.shape
    nq = seq // Q_BLOCK
    nseg = cmp.shape[1]
    n_blk = seq // SEL_BLOCK
    n_top = min(SEL_TOPN, n_blk)
    seq_pad = -(-seq // NSA_KEY_CHUNK) * NSA_KEY_CHUNK
    ov_np = _sel_overlap_t(nseg, n_blk)
    ov_np[:, (seq - CMP_BLOCK) // CMP_STRIDE + 1:] = 0.0
    ov = jnp.asarray(ov_np, MXU_DTYPE)
    per_q = lambda w: pl.BlockSpec((None, w, Q_BLOCK), lambda b, i: (b, 0, i))
    per_b = lambda *s: pl.BlockSpec((None,) + s, lambda b, i: (b,) + (0,) * len(s))
    const = lambda shape: pl.BlockSpec(shape, lambda b, i: (0,) * len(shape))
    return pl.pallas_call(
        functools.partial(_nsa_kernel, n_top=n_top, n_blk=n_blk),
        out_shape=jax.ShapeDtypeStruct((bsz, seq, C_HEADS * HEAD_DIM), MXU_DTYPE),
        grid=(bsz, nq),
        in_specs=[per_q(512), per_q(512), per_q(128), per_b(nseg, 256), per_b(128, nseg),
                  per_b(seq, 128), per_b(seq // LANES, LANES, LANES),
                  per_b(seq, 128), per_b(seq // LANES, LANES, LANES),
                  const(ov.shape), const((C_HEADS * HEAD_DIM, 1))],
        out_specs=pl.BlockSpec((None, Q_BLOCK, C_HEADS * HEAD_DIM), lambda b, i: (b, i, 0)),
        scratch_shapes=[pltpu.VMEM((C_KV_HEADS, max(n_blk, seq_pad // SEL_BLOCK), Q_BLOCK), F32),
                        pltpu.VMEM((C_KV_HEADS, 1, NQ4), F32),
                        pltpu.VMEM((C_KV_HEADS, HEAD_DIM + ONES_ROWS, NQ4), F32)],
        compiler_params=_cparams(("arbitrary", "arbitrary")),
        name="nsa_attention",
    )(cq_t, cqr_t, misc_t, cmp, vcmp_t, ks, vs_t, kw, vw_t, ov, gain.reshape(-1, 1))


def _out_mlp_kernel(x_ref, a_ref, b_ref, c_ref, wo_ref, g2_ref, wu_ref, wd_ref, fg_ref,
                    o_ref, x1_scr, n2_scr, acc_scr, *, final_norm):
    j = pl.program_id(1)

    @pl.when(j == 0)
    def _():
        mixed = (jnp.dot(a_ref[...], wo_ref[0:256, :], preferred_element_type=F32)
                 + jnp.dot(b_ref[...], wo_ref[256:512, :], preferred_element_type=F32)
                 + jnp.dot(c_ref[...], wo_ref[512:1024, :], preferred_element_type=F32))
        x1 = x_ref[...] + mixed
        x1_scr[...] = x1
        n2_scr[...] = (_rms(x1, -1) * g2_ref[...]).astype(n2_scr.dtype)
        acc_scr[...] = jnp.zeros_like(acc_scr)

    z = jnp.dot(n2_scr[...], wu_ref[...], preferred_element_type=F32)
    acc_scr[...] += _dot(jnp.square(jnp.maximum(z, 0.0)), wd_ref[...])

    @pl.when(j == pl.num_programs(1) - 1)
    def _():
        x2 = x1_scr[...] + acc_scr[...]
        if final_norm:
            x2 = _rms(x2, -1) * fg_ref[...]
        o_ref[...] = x2


def _out_mlp(x, a_n, b_tm, c_n, w_out, ln2_g, w_up, w_down, final_g, final_norm):
    bsz, seq, _ = x.shape
    tm = min(MLP_TM, seq)
    nj = seq // tm
    nf = D_FF // MLP_TF
    rows = lambda w: pl.BlockSpec((None, tm, w), lambda r, f: (r // nj, r % nj, 0))
    const = lambda shape: pl.BlockSpec(shape, lambda r, f: (0,) * len(shape))
    return pl.pallas_call(
        functools.partial(_out_mlp_kernel, final_norm=final_norm),
        out_shape=jax.ShapeDtypeStruct(x.shape, F32),
        grid=(bsz * nj, nf),
        in_specs=[rows(D_MODEL), rows(256),
                  pl.BlockSpec((tm, B_WIDTH), lambda r, f: (r % nj, r // nj)),
                  rows(512), const((D_MODEL, D_MODEL)), const((1, D_MODEL)),
                  pl.BlockSpec((D_MODEL, MLP_TF), lambda r, f: (0, f)),
                  pl.BlockSpec((MLP_TF, D_MODEL), lambda r, f: (f, 0)),
                  const((1, D_MODEL))],
        out_specs=rows(D_MODEL),
        scratch_shapes=[pltpu.VMEM((tm, D_MODEL), F32), pltpu.VMEM((tm, D_MODEL), MXU_DTYPE),
                        pltpu.VMEM((tm, D_MODEL), F32)],
        compiler_params=_cparams(("arbitrary", "arbitrary")),
        name="out_proj_mlp",
    )(x, a_n, b_tm, c_n, w_out.astype(MXU_DTYPE), ln2_g.reshape(1, -1), w_up.astype(MXU_DTYPE),
      w_down.astype(MXU_DTYPE), final_g.reshape(1, -1))


def kernel(x, positions, ln1_g, w_in, kv_norm_g, w_uk, w_uv, s5_lambda_re, s5_lambda_im, s5_log_step, s5_b_re, s5_b_im, s5_c_re, s5_c_im, s5_d, s5_glu_w, s5_glu_b, cmp_pos_k, cmp_pos_v, cmp_k_w1, cmp_k_w2, cmp_v_w1, cmp_v_w2, gain_a, gain_b, gain_c, w_out, ln2_g, w_up, w_down, final_g):
    bsz, seq, _ = x.shape
    depth = w_in.shape[0]
    src, partner, pat_np = _proj_layout()
    pat = jnp.asarray(pat_np, jnp.bfloat16)
    tab = _rope_table(positions)

    def regather(w, idx):
        cols = jnp.take(w, jnp.asarray(np.maximum(idx, 0)), axis=1)
        return jnp.where(jnp.asarray(idx >= 0)[None, :], cols, 0.0).astype(MXU_DTYPE)

    for layer in range(depth):
        w_r = regather(w_in[layer], src)
        w_sw = regather(w_in[layer], partner)
        (aq_t, akv, ac_t, iq_t, misc, misc_t, u_tm, cq_t, cqr_t, kvc, ks, vs_t, kw, vw_t) = _in_proj(
            x, tab, ln1_g[layer], kv_norm_g[layer], w_r, w_sw, pat)

        ar, ai, bbr, bbi = _s5_discretize(s5_lambda_re[layer], s5_lambda_im[layer], s5_log_step[layer],
                                          s5_b_re[layer], s5_b_im[layer])
        b_tm = _s5(u_tm, bsz, ar, ai, bbr, bbi, s5_c_re[layer], s5_c_im[layer], s5_d[layer].reshape(-1),
                   s5_glu_w[layer], s5_glu_b[layer], gain_b[layer])

        cmp, vcmp_t = _compress(kvc, cmp_pos_k[layer], cmp_pos_v[layer], cmp_k_w1[layer],
                                cmp_k_w2[layer], cmp_v_w1[layer], cmp_v_w2[layer])
        a_n = _dsa(aq_t, iq_t, misc_t, akv, misc, ac_t, w_uk[layer], w_uv[layer], gain_a[layer])
        c_n = _nsa(cq_t, cqr_t, misc_t, cmp, vcmp_t, ks, vs_t, kw, vw_t, gain_c[layer])

        x = _out_mlp(x, a_n, b_tm, c_n, w_out[layer], ln2_g[layer], w_up[layer], w_down[layer],
                     final_g, final_norm=(layer == depth - 1))
    return x
```

```python
import functools
import math

import numpy as np
import jax
import jax.numpy as jnp
from jax import lax
from jax.experimental import pallas as pl
from jax.experimental.pallas import tpu as pltpu

F32 = jnp.float32
I32 = jnp.int32
MXU_DTYPE = jnp.bfloat16

D_MODEL = 1024
HEAD_DIM = 64
ROPE_THETA = 500000.0
ROPE_DIM = HEAD_DIM // 4
NORM_EPS = 1e-6
Q_BLOCK = 128
NEG = -1e30
LOG2E = math.log2(math.e)
D_FF = 4 * D_MODEL

A_HEADS = 4
A_NOPE = HEAD_DIM - ROPE_DIM
A_VDIM = HEAD_DIM
A_KV_RANK = 128
IDX_HEADS = 4
IDX_DIM = 32
IDX_ROPE = IDX_DIM // 4
DSA_TOPK = 256

B_WIDTH = 256
S5_GROUP = 16
S5_GROUPS = B_WIDTH // S5_GROUP
S5_STATE = 64
S5_NSTATE = S5_GROUPS * S5_STATE

C_HEADS = 8
C_KV_HEADS = 2
C_GROUP = C_HEADS // C_KV_HEADS
CMP_BLOCK = 32
CMP_STRIDE = 16
SEL_BLOCK = 64
SEL_TOPN = 16
SEL_FORCE = 1e9
WINDOW = 512

IN_SIZES = (A_HEADS * HEAD_DIM, A_KV_RANK, ROPE_DIM, IDX_HEADS * IDX_DIM, IDX_DIM, IDX_HEADS,
            B_WIDTH, C_HEADS * HEAD_DIM, 6 * C_KV_HEADS * HEAD_DIM, 3 * C_HEADS)

LANES = 128
SUBLANES = 8
VMEM_LIMIT = 56 * 1024 * 1024

SEG_AQ, SEG_AKV, SEG_IQ, SEG_MISC, SEG_BU, SEG_CQ, SEG_KVC, SEG_KS, SEG_VS, SEG_KW, SEG_VW = (
    0, 256, 512, 640, 768, 1024, 1536, 1792, 1920, 2048, 2176)
N_PROJ = 2304
MISC_IK, MISC_IW, MISC_GATE = 0, 32, 36
TAB_COS, TAB_SIN, TAB_ICOS, TAB_ISIN, TAB_ONE = 0, 8, 16, 20, 24
TAB_PART = 32

PROJ_TM = 512
DSA_KEY_CHUNK = 512
NSA_KEY_CHUNK = 1024
S5_TC = 64
MLP_TM = 1024
MLP_TF = 1024


def _cparams(sem):
    return pltpu.CompilerParams(dimension_semantics=sem, vmem_limit_bytes=VMEM_LIMIT)


def _dot(a, b):
    return jnp.dot(a.astype(MXU_DTYPE), b.astype(MXU_DTYPE), preferred_element_type=F32)


def _rms(x, axis):
    return x * lax.rsqrt(jnp.mean(x * x, axis=axis, keepdims=True) + NORM_EPS)


def _rope_tab_kernel(pos_ref, freq_ref, out_ref):
    ang = pos_ref[...].astype(F32) * freq_ref[...]
    lane = lax.broadcasted_iota(I32, ang.shape, 1)
    l32 = lane % TAB_PART
    is_cos = (l32 < TAB_SIN) | ((l32 >= TAB_ICOS) & (l32 < TAB_ISIN))
    is_sin = ((l32 >= TAB_SIN) & (l32 < TAB_ICOS)) | ((l32 >= TAB_ISIN) & (l32 < TAB_ONE))
    val = jnp.where(is_cos, jnp.cos(ang),
                    jnp.where(is_sin, jnp.sin(ang), jnp.where(l32 == TAB_ONE, 1.0, 0.0)))
    hi = val.astype(jnp.bfloat16).astype(F32)
    r1 = val - hi
    mid = r1.astype(jnp.bfloat16).astype(F32)
    lo = r1 - mid
    part = lane // TAB_PART
    out = jnp.where(part == 0, hi, jnp.where(part == 1, mid, jnp.where(part == 2, lo, 0.0)))
    out_ref[...] = out.astype(jnp.bfloat16)


def _rope_table(positions):
    bsz, seq = positions.shape
    t = bsz * seq
    inv_r = (np.float32(ROPE_THETA) ** (-np.arange(0, ROPE_DIM, 2, dtype=np.float32) / ROPE_DIM))
    inv_i = (np.float32(ROPE_THETA) ** (-np.arange(0, IDX_ROPE, 2, dtype=np.float32) / IDX_ROPE))
    f32 = np.zeros(TAB_PART, np.float32)
    f32[TAB_COS:TAB_COS + 8] = inv_r
    f32[TAB_SIN:TAB_SIN + 8] = inv_r
    f32[TAB_ICOS:TAB_ICOS + 4] = inv_i
    f32[TAB_ISIN:TAB_ISIN + 4] = inv_i
    freq = jnp.asarray(np.tile(f32, LANES // TAB_PART)[None, :])
    tq = min(1024, t)
    return pl.pallas_call(
        _rope_tab_kernel,
        out_shape=jax.ShapeDtypeStruct((t, LANES), jnp.bfloat16),
        grid=(t // tq,),
        in_specs=[pl.BlockSpec((tq, 1), lambda i: (i, 0)),
                  pl.BlockSpec((1, LANES), lambda i: (0, 0))],
        out_specs=pl.BlockSpec((tq, LANES), lambda i: (i, 0)),
        compiler_params=_cparams(("arbitrary",)),
        name="rope_table",
    )(positions.reshape(t, 1), freq).reshape(bsz, seq, LANES)


PAT_HEAD, PAT_KROPE, PAT_IDX, PAT_MISC = range(4)
N_PAT = 4


def _proj_layout():
    offs = np.concatenate([[0], np.cumsum(IN_SIZES)])
    o_aq, o_ckv, o_kr, o_iq, o_ik, o_iw, o_bu, o_cq, o_kv, o_gate = offs[:10]
    src = -np.ones(N_PROJ, np.int64)
    partner = -np.ones(N_PROJ, np.int64)
    clane = np.full(N_PROJ, TAB_ONE, np.int64)
    slane = -np.ones(N_PROJ, np.int64)
    ssign = np.zeros(N_PROJ, np.float32)

    def plain(c0, o0, w):
        src[c0:c0 + w] = np.arange(o0, o0 + w)

    def rope(c0, o0, half, cos_lane, sin_lane):
        for j in range(half):
            partner[c0 + j] = o0 + half + j
            partner[c0 + half + j] = o0 + j
            clane[c0 + j] = clane[c0 + half + j] = cos_lane + j
            slane[c0 + j] = slane[c0 + half + j] = sin_lane + j
            ssign[c0 + j] = -1.0
            ssign[c0 + half + j] = 1.0

    plain(SEG_AQ, o_aq, A_HEADS * HEAD_DIM)
    for h in range(A_HEADS):
        rope(SEG_AQ + h * HEAD_DIM, o_aq + h * HEAD_DIM, ROPE_DIM // 2, TAB_COS, TAB_SIN)
    plain(SEG_AKV, o_ckv, A_KV_RANK)
    plain(SEG_AKV + A_KV_RANK, o_kr, ROPE_DIM)
    rope(SEG_AKV + A_KV_RANK, o_kr, ROPE_DIM // 2, TAB_COS, TAB_SIN)
    plain(SEG_IQ, o_iq, IDX_HEADS * IDX_DIM)
    for h in range(IDX_HEADS):
        rope(SEG_IQ + h * IDX_DIM, o_iq + h * IDX_DIM, IDX_ROPE // 2, TAB_ICOS, TAB_ISIN)
    plain(SEG_MISC + MISC_IK, o_ik, IDX_DIM)
    rope(SEG_MISC + MISC_IK, o_ik, IDX_ROPE // 2, TAB_ICOS, TAB_ISIN)
    plain(SEG_MISC + MISC_IW, o_iw, IDX_HEADS)
    plain(SEG_MISC + MISC_GATE, o_gate, 3 * C_HEADS)
    plain(SEG_BU, o_bu, B_WIDTH)
    plain(SEG_CQ, o_cq, C_HEADS * HEAD_DIM)
    for h in range(C_HEADS):
        rope(SEG_CQ + h * HEAD_DIM, o_cq + h * HEAD_DIM, ROPE_DIM // 2, TAB_COS, TAB_SIN)
    plain(SEG_KVC, o_kv, 6 * C_KV_HEADS * HEAD_DIM)
    for seg, sub in ((SEG_KS, 2), (SEG_KW, 4)):
        for g in range(C_KV_HEADS):
            rope(seg + g * HEAD_DIM, o_kv + sub * C_KV_HEADS * HEAD_DIM + g * HEAD_DIM,
                 ROPE_DIM // 2, TAB_COS, TAB_SIN)

    k = np.arange(LANES)[:, None]
    live = k < 3 * TAB_PART
    ec = (live & ((k % TAB_PART) == clane[None, :])).astype(np.float32)
    es = (live & ((k % TAB_PART) == slane[None, :])).astype(np.float32) * ssign[None, :]
    starts = {PAT_HEAD: SEG_AQ, PAT_KROPE: SEG_AKV + A_KV_RANK, PAT_IDX: SEG_IQ, PAT_MISC: SEG_MISC}
    blocks = [m[:, starts[p]:starts[p] + LANES] for m in (ec, es) for p in range(N_PAT)]
    return src, partner, np.concatenate(blocks, axis=1)


def _in_proj_kernel(x_ref, tab_ref, g_ref, kvg_ref, w_ref, wsw_ref, pat_ref,
                    aqT_ref, akv_ref, acT_ref, iqT_ref, misc_ref, miscT_ref, u_ref,
                    cqT_ref, cqrT_ref, kvc_ref, ks_ref, vsT_ref, kw_ref, vwT_ref):
    x = x_ref[...]
    n = (_rms(x, -1) * g_ref[...]).astype(MXU_DTYPE)
    cs = jnp.dot(tab_ref[...], pat_ref[...], preferred_element_type=F32)

    def plain(c0, w):
        return jnp.dot(n, w_ref[:, c0:c0 + w], preferred_element_type=F32)

    def rotate(p, c0, pat):
        nb = p.shape[1] // LANES
        psw = jnp.dot(n, wsw_ref[:, c0:c0 + p.shape[1]], preferred_element_type=F32)
        cos = cs[:, pat * LANES:(pat + 1) * LANES]
        sin = cs[:, (N_PAT + pat) * LANES:(N_PAT + pat + 1) * LANES]
        return p * jnp.concatenate([cos] * nb, axis=1) + psw * jnp.concatenate([sin] * nb, axis=1)

    def chunked_t(val, ref):
        vt = val.T.astype(ref.dtype)
        for c in range(ref.shape[0]):
            ref[c] = vt[:, c * LANES:(c + 1) * LANES]

    aqT_ref[...] = rotate(plain(SEG_AQ, 256), SEG_AQ, PAT_HEAD).T.astype(aqT_ref.dtype)

    akv = plain(SEG_AKV, 256)
    lat = _rms(akv[:, :A_KV_RANK], -1) * kvg_ref[...]
    k_rope = rotate(akv[:, A_KV_RANK:], SEG_AKV + A_KV_RANK, PAT_KROPE)
    akv_ref[...] = jnp.concatenate([lat, k_rope], axis=1).astype(akv_ref.dtype)
    chunked_t(lat, acT_ref)

    iqT_ref[...] = rotate(plain(SEG_IQ, 128), SEG_IQ, PAT_IDX).T.astype(iqT_ref.dtype)
    misc = rotate(plain(SEG_MISC, 128), SEG_MISC, PAT_MISC)
    misc_ref[...] = misc
    miscT_ref[...] = misc.T
    u_ref[...] = plain(SEG_BU, 256)
    qk_scale = HEAD_DIM ** -0.5 * LOG2E
    cq = plain(SEG_CQ, 512)
    cqT_ref[...] = (cq * qk_scale).T.astype(cqT_ref.dtype)
    cqrT_ref[...] = (rotate(cq, SEG_CQ, PAT_HEAD) * qk_scale).T.astype(cqrT_ref.dtype)
    kvc_ref[...] = plain(SEG_KVC, 256)
    ks_ref[...] = rotate(plain(SEG_KS, 128), SEG_KS, PAT_HEAD).astype(ks_ref.dtype)
    chunked_t(plain(SEG_VS, 128), vsT_ref)
    kw_ref[...] = rotate(plain(SEG_KW, 128), SEG_KW, PAT_HEAD).astype(kw_ref.dtype)
    chunked_t(plain(SEG_VW, 128), vwT_ref)


def _in_proj(x, tab, ln_g, kv_g, w_r, w_sw, pat):
    bsz, seq, _ = x.shape
    tm = min(PROJ_TM, seq)
    nj = seq // tm
    nck = tm // LANES
    bf = MXU_DTYPE

    def rows(w):
        return pl.BlockSpec((None, tm, w), lambda b, j: (b, j, 0))

    def cols(w):
        return pl.BlockSpec((None, w, tm), lambda b, j: (b, 0, j))

    def chunks():
        return pl.BlockSpec((None, nck, LANES, LANES), lambda b, j: (b, j, 0, 0))

    def const(shape):
        return pl.BlockSpec(shape, lambda b, j: (0,) * len(shape))

    sds = jax.ShapeDtypeStruct
    out_shape = (
        sds((bsz, 256, seq), bf),
        sds((bsz, seq, 256), bf),
        sds((bsz, seq // LANES, LANES, LANES), bf),
        sds((bsz, 128, seq), bf),
        sds((bsz, seq, 128), F32),
        sds((bsz, 128, seq), F32),
        sds((seq, bsz * B_WIDTH), F32),
        sds((bsz, 512, seq), bf),
        sds((bsz, 512, seq), bf),
        sds((bsz, seq, 256), F32),
        sds((bsz, seq, 128), bf),
        sds((bsz, seq // LANES, LANES, LANES), bf),
        sds((bsz, seq, 128), bf),
        sds((bsz, seq // LANES, LANES, LANES), bf),
    )
    out_specs = (cols(256), rows(256), chunks(), cols(128), rows(128), cols(128),
                 pl.BlockSpec((tm, B_WIDTH), lambda b, j: (j, b)),
                 cols(512), cols(512), rows(256), rows(128), chunks(), rows(128), chunks())
    return pl.pallas_call(
        _in_proj_kernel,
        out_shape=out_shape,
        grid=(bsz, nj),
        in_specs=[rows(D_MODEL), rows(LANES), const((1, D_MODEL)), const((1, A_KV_RANK)),
                  const((D_MODEL, N_PROJ)), const((D_MODEL, N_PROJ)),
                  const((LANES, 2 * N_PAT * LANES))],
        out_specs=out_specs,
        compiler_params=_cparams(("arbitrary", "arbitrary")),
        name="in_proj",
    )(x, tab, ln_g.reshape(1, -1), kv_g.reshape(1, -1), w_r, w_sw, pat)


def _s5_disc_kernel(lr_ref, li_ref, ls_ref, br_ref, bi_ref, ar_ref, ai_ref, bbr_ref, bbi_ref):
    lr, li = lr_ref[...], li_ref[...]
    step = jnp.exp(ls_ref[...])
    mag = jnp.exp(lr * step)
    ar = mag * jnp.cos(li * step)
    ai = mag * jnp.sin(li * step)
    den = lr * lr + li * li
    zr = ((ar - 1.0) * lr + ai * li) / den
    zi = (ai * lr - (ar - 1.0) * li) / den
    br, bi = br_ref[...], bi_ref[...]
    ar_ref[...] = ar
    ai_ref[...] = ai
    bbr_ref[...] = zr * br - zi * bi
    bbi_ref[...] = zr * bi + zi * br


def _s5_discretize(lam_re, lam_im, log_step, b_re, b_im):
    g, p, h = b_re.shape
    ex = lambda a: jnp.repeat(a.astype(F32), h, axis=1)
    ls = jnp.broadcast_to(log_step.astype(F32)[:, None], (g, p * h))
    sds = jax.ShapeDtypeStruct((g, p * h), F32)
    ar, ai, bbr, bbi = pl.pallas_call(
        _s5_disc_kernel, out_shape=(sds, sds, sds, sds), name="s5_discretize",
    )(ex(lam_re), ex(lam_im), ls, b_re.astype(F32).reshape(g, p * h), b_im.astype(F32).reshape(g, p * h))
    ar = ar.reshape(g, p, h)[:, :, 0]
    ai = ai.reshape(g, p, h)[:, :, 0]
    return ar, ai, bbr.reshape(g, p, h), bbi.reshape(g, p, h)


def _s5_kernel(u_ref, bmat_ref, a_ref, cre_ref, cim_ref, d_ref, gw_ref, gb_ref, gain_ref,
               o_ref, u_scr, x_scr, h_scr, *, tc):
    ns = S5_NSTATE

    @pl.when(pl.program_id(0) == 0)
    def _():
        h_scr[...] = jnp.zeros_like(h_scr)

    halves = range(B_WIDTH // LANES)
    for b in range(SUBLANES):
        for hf in halves:
            c0 = b * B_WIDTH + hf * LANES
            u_scr[hf, pl.ds(b, tc, stride=SUBLANES), :] = u_ref[:, c0:c0 + LANES]
    u = jnp.concatenate([u_scr[hf] for hf in halves], axis=1)
    x_scr[...] = _dot(u, bmat_ref[...])
    ar = a_ref[0:SUBLANES, :]
    ai = a_ref[SUBLANES:2 * SUBLANES, :]

    def step(t, carry):
        hr, hi = carry
        r0 = pl.multiple_of(t * SUBLANES, SUBLANES)
        xr = x_scr[pl.ds(r0, SUBLANES), 0:ns]
        xi = x_scr[pl.ds(r0, SUBLANES), ns:2 * ns]
        nhr = ar * hr - ai * hi + xr
        nhi = ar * hi + ai * hr + xi
        x_scr[pl.ds(r0, SUBLANES), 0:ns] = nhr
        x_scr[pl.ds(r0, SUBLANES), ns:2 * ns] = nhi
        return nhr, nhi

    hr, hi = lax.fori_loop(0, tc, step, (h_scr[:, 0:ns], h_scr[:, ns:2 * ns]), unroll=8)
    h_scr[:, 0:ns] = hr
    h_scr[:, ns:2 * ns] = hi

    y = (_dot(x_scr[:, 0:ns], cre_ref[...]) - _dot(x_scr[:, ns:2 * ns], cim_ref[...])
         + d_ref[...] * u)
    y = jax.nn.gelu(y)
    y = y * jax.nn.sigmoid(_dot(y, gw_ref[...]) + gb_ref[...])
    y = _rms(y, -1) * gain_ref[...]
    for hf in halves:
        u_scr[hf] = y[:, hf * LANES:(hf + 1) * LANES]
    for b in range(SUBLANES):
        for hf in halves:
            c0 = b * B_WIDTH + hf * LANES
            o_ref[:, c0:c0 + LANES] = u_scr[hf, pl.ds(b, tc, stride=SUBLANES), :].astype(o_ref.dtype)


def _s5(u_tm, bsz, ar, ai, bbr, bbi, c_re, c_im, d, glu_w, glu_b, gain):
    assert bsz == SUBLANES, "the S5 scan keeps one batch row per sublane"
    seq = u_tm.shape[0]
    tc = min(S5_TC, seq)
    rows = tc * bsz
    eye = jnp.eye(S5_GROUPS, dtype=F32)
    bmat = jnp.concatenate(
        [jnp.einsum('gph,gk->ghkp', b, eye).reshape(B_WIDTH, S5_NSTATE) for b in (bbr, bbi)], axis=1)
    cre = jnp.einsum('ghp,gk->gpkh', c_re.astype(F32), eye).reshape(S5_NSTATE, B_WIDTH)
    cim = jnp.einsum('ghp,gk->gpkh', c_im.astype(F32), eye).reshape(S5_NSTATE, B_WIDTH)
    avec = jnp.concatenate([jnp.broadcast_to(a.reshape(1, S5_NSTATE), (SUBLANES, S5_NSTATE))
                            for a in (ar, ai)], axis=0)
    const = lambda shape: pl.BlockSpec(shape, lambda i: (0,) * len(shape))
    out = pl.pallas_call(
        functools.partial(_s5_kernel, tc=tc),
        out_shape=jax.ShapeDtypeStruct((seq, bsz * B_WIDTH), MXU_DTYPE),
        grid=(seq // tc,),
        in_specs=[pl.BlockSpec((tc, bsz * B_WIDTH), lambda i: (i, 0)),
                  const((B_WIDTH, 2 * S5_NSTATE)), const((2 * SUBLANES, S5_NSTATE)),
                  const((S5_NSTATE, B_WIDTH)), const((S5_NSTATE, B_WIDTH)),
                  const((1, B_WIDTH)), const((B_WIDTH, B_WIDTH)), const((1, B_WIDTH)),
                  const((1, B_WIDTH))],
        out_specs=pl.BlockSpec((tc, bsz * B_WIDTH), lambda i: (i, 0)),
        scratch_shapes=[pltpu.VMEM((B_WIDTH // LANES, rows, LANES), F32),
                        pltpu.VMEM((rows, 2 * S5_NSTATE), F32),
                        pltpu.VMEM((SUBLANES, 2 * S5_NSTATE), F32)],
        compiler_params=_cparams(("arbitrary",)),
        name="s5_scan",
    )(u_tm, bmat.astype(MXU_DTYPE), avec,
      cre.astype(MXU_DTYPE), cim.astype(MXU_DTYPE), d.astype(F32).reshape(1, B_WIDTH),
      glu_w.astype(MXU_DTYPE), glu_b.astype(F32).reshape(1, B_WIDTH), gain.reshape(1, B_WIDTH))
    return out


def _compress_kernel(k_ref, v_ref, pa_ref, pb_ref, wa_ref, wb_ref, w2_ref, cmp_ref, vT_ref):
    nseg = cmp_ref.shape[0]
    cw = k_ref.shape[1] + v_ref.shape[1]
    xa = jnp.zeros((nseg, cw), F32)
    xb = jnp.zeros((nseg, cw), F32)
    for l in range(CMP_STRIDE):
        rows = pl.ds(l, nseg, stride=CMP_STRIDE)
        tok = jnp.concatenate([k_ref[rows, :], v_ref[rows, :]], axis=1)
        cols = slice(l * cw, (l + 1) * cw)
        xa = xa + _dot(tok + pa_ref[:, cols], wa_ref[cols, :])
        xb = xb + _dot(tok + pb_ref[:, cols], wb_ref[cols, :])
    pre = xa + pltpu.roll(xb, nseg - 1, 0)
    out = _dot(jax.nn.gelu(pre), w2_ref[...])
    cmp_ref[...] = out.astype(cmp_ref.dtype)
    vT_ref[...] = out[:, LANES:].T.astype(vT_ref.dtype)


def _compress(kvc, pos_k, pos_v, k_w1, k_w2, v_w1, v_w2):
    bsz, seq, _ = kvc.shape
    nseg = seq // CMP_STRIDE
    width = CMP_STRIDE * 256
    eye = jnp.eye(4, dtype=F32)
    w1 = jnp.stack([k_w1, k_w1, v_w1, v_w1]).astype(F32).reshape(4, CMP_BLOCK, HEAD_DIM, HEAD_DIM)
    pos = jnp.stack([pos_k, pos_k, pos_v, pos_v]).astype(F32)

    def half(lo):
        w = jnp.einsum('slde,st->lsdte', w1[:, lo:lo + CMP_STRIDE], eye).reshape(width, 256)
        p = jnp.transpose(pos[:, lo:lo + CMP_STRIDE], (1, 0, 2)).reshape(1, width)
        return w.astype(MXU_DTYPE), p

    wa, pa = half(0)
    wb, pb = half(CMP_STRIDE)
    w2 = jnp.einsum('sde,st->sdte', jnp.stack([k_w2, k_w2, v_w2, v_w2]).astype(F32), eye).reshape(256, 256)
    const = lambda shape: pl.BlockSpec(shape, lambda b: (0,) * len(shape))
    return pl.pallas_call(
        _compress_kernel,
        out_shape=(jax.ShapeDtypeStruct((bsz, nseg, 256), MXU_DTYPE),
                   jax.ShapeDtypeStruct((bsz, 128, nseg), MXU_DTYPE)),
        grid=(bsz,),
        in_specs=[pl.BlockSpec((None, seq, LANES), lambda b: (b, 0, 0)),
                  pl.BlockSpec((None, seq, LANES), lambda b: (b, 0, 1)),
                  const((1, width)), const((1, width)), const((width, 256)), const((width, 256)),
                  const((256, 256))],
        out_specs=(pl.BlockSpec((None, nseg, 256), lambda b: (b, 0, 0)),
                   pl.BlockSpec((None, 128, nseg), lambda b: (b, 0, 0))),
        compiler_params=_cparams(("arbitrary",)),
        name="nsa_compress",
    )(kvc, kvc, pa, pb, wa, wb, w2.astype(MXU_DTYPE))


ONES_ROWS = 16
NQ4 = 4 * Q_BLOCK


def _with_ones(v_t):
    return jnp.concatenate([v_t, jnp.ones((ONES_ROWS, v_t.shape[1]), v_t.dtype)], axis=0)


def _online_softmax_step(s, bias, v_ext, m_ref, acc_ref):
    s = s + bias
    m_old = m_ref[...]
    m_new = jnp.maximum(m_old, jnp.max(s, axis=0, keepdims=True))
    p = jnp.exp2(s - m_new)
    acc_ref[...] = jnp.exp2(m_old - m_new) * acc_ref[...] + _dot(v_ext, p)
    m_ref[...] = m_new


def _tile4(x):
    return jnp.concatenate([x, x, x, x], axis=1)


INT_MIN = -2 ** 31
NEG_INF_KEY = int(np.int32(np.uint32(0xFF800000 ^ 0x7FFFFFFF)))


def _dsa_kernel(qT_ref, iqT_ref, miscT_ref, kv_ref, misc_ref, cT_ref, mq_ref, wuv_ref, gain_ref, tri_ref,
                o_ref, key_scr, m_scr, acc_scr, *, k_top, max_chunks):
    i = pl.program_id(1)
    t0 = i * Q_BLOCK
    kc = DSA_KEY_CHUNK
    nch = (t0 + Q_BLOCK + kc - 1) // kc
    t_lane = t0 + lax.broadcasted_iota(I32, (1, Q_BLOCK), 1)
    row_iota = lax.broadcasted_iota(I32, (kc, Q_BLOCK), 0)

    qcat = (_dot(mq_ref[...], qT_ref[...]) * (HEAD_DIM ** -0.5 * LOG2E)).astype(MXU_DTYPE)
    qs_t = jnp.concatenate([qcat[h * 256:(h + 1) * 256] for h in range(A_HEADS)], axis=1)
    iq_t = iqT_ref[...]
    iq_all = jnp.concatenate([iq_t[h * IDX_DIM:(h + 1) * IDX_DIM, :] for h in range(IDX_HEADS)], axis=1)
    w_t = miscT_ref[MISC_IW:MISC_IW + IDX_HEADS, :] * (IDX_HEADS ** -0.5 * IDX_DIM ** -0.5)

    def idx_body(c, diagonal):
        k0 = c * kc
        ik = misc_ref[pl.ds(k0, kc), MISC_IK:MISC_IK + IDX_DIM]
        d = _dot(ik, iq_all)
        score = jnp.zeros((kc, Q_BLOCK), F32)
        for h in range(IDX_HEADS):
            score = score + jnp.maximum(d[:, h * Q_BLOCK:(h + 1) * Q_BLOCK], 0.0) * w_t[h:h + 1, :]
        if diagonal:
            score = jnp.where(row_iota + k0 <= t_lane, score, -jnp.inf)
        bits = pltpu.bitcast(score, I32)
        key_scr[pl.ds(k0, kc), :] = bits ^ ((bits >> 31) & 0x7FFFFFFF)

    def per_chunk_count(fn):
        def variant(n_chunks):
            fn(n_chunks)
            return 0
        lax.switch(nch - 1, [functools.partial(variant, n) for n in range(1, max_chunks + 1)])

    per_chunk_count(lambda n_chunks: [idx_body(c, c == n_chunks - 1) for c in range(n_chunks)])

    def search(n_chunks):
        def count(pred):
            acc = jnp.zeros((SUBLANES, Q_BLOCK), I32)
            for c in range(n_chunks):
                hit = pred(key_scr[c * kc:(c + 1) * kc, :])
                acc = acc + jnp.sum(hit.reshape(kc // SUBLANES, SUBLANES, Q_BLOCK), axis=0)
            return jnp.sum(acc, axis=0, keepdims=True)

        def thr_bit(b, prefix):
            cand = prefix | lax.shift_left(jnp.int32(1), 31 - b)
            cand_s = cand ^ INT_MIN
            cnt = count(lambda keys: jnp.where(keys >= cand_s, 1, 0))
            return jnp.where(cnt >= k_top, cand, prefix)

        thr = lax.fori_loop(0, 32, thr_bit, jnp.zeros((1, Q_BLOCK), I32)) ^ INT_MIN
        return thr, count(lambda keys: jnp.where(keys > thr, 1, 0))

    thr, n_gt = lax.switch(nch - 1, [functools.partial(search, n) for n in range(1, max_chunks + 1)])
    need = jnp.where(thr == NEG_INF_KEY, 0, k_top - n_gt).astype(F32)

    m_scr[...] = jnp.full_like(m_scr, NEG)
    acc_scr[...] = jnp.zeros_like(acc_scr)

    def att_body(c, n_tied):
        k0 = c * kc
        s = _dot(kv_ref[pl.ds(k0, kc), :], qs_t)
        keys = key_scr[pl.ds(k0, kc), :]
        tied = _dot(tri_ref[...], jnp.where(keys == thr, 1.0, 0.0)) + n_tied
        bias = jnp.where(keys > thr, 0.0,
                         jnp.where(keys == thr, jnp.where(tied <= need, 0.0, NEG), NEG))
        v_t = jnp.concatenate([cT_ref[c * (kc // LANES) + j] for j in range(kc // LANES)], axis=1)
        _online_softmax_step(s, _tile4(bias), _with_ones(v_t), m_scr, acc_scr)
        return tied[kc - 1:kc, :]

    def attend(n_chunks):
        n_tied = jnp.zeros((1, Q_BLOCK), F32)
        for c in range(n_chunks):
            n_tied = att_body(c, n_tied)

    per_chunk_count(attend)

    acc = acc_scr[...]
    o_lat = (acc[:A_KV_RANK] / acc[A_KV_RANK:A_KV_RANK + 1]).astype(MXU_DTYPE)
    out_t = jnp.concatenate(
        [_dot(wuv_ref[h], o_lat[:, h * Q_BLOCK:(h + 1) * Q_BLOCK]) for h in range(A_HEADS)], axis=0)
    out_t = _rms(out_t, 0) * gain_ref[...]
    o_ref[...] = out_t.T.astype(o_ref.dtype)


def _dsa(aq_t, iq_t, misc_t, akv, misc, ac_t, w_uk, w_uv, gain):
    bsz, _, seq = aq_t.shape
    nq = seq // Q_BLOCK
    k_top = min(DSA_TOPK, seq // 4)
    seq_pad = -(-seq // DSA_KEY_CHUNK) * DSA_KEY_CHUNK
    tri = jnp.asarray(np.tril(np.ones((DSA_KEY_CHUNK, DSA_KEY_CHUNK), np.float32)), MXU_DTYPE)
    mq = jnp.zeros((A_HEADS, 256, A_HEADS, HEAD_DIM), F32)
    for h in range(A_HEADS):
        mq = mq.at[h, :A_KV_RANK, h, ROPE_DIM:].set(w_uk[:, h, :].astype(F32))
        mq = mq.at[h, A_KV_RANK:A_KV_RANK + ROPE_DIM, h, :ROPE_DIM].set(jnp.eye(ROPE_DIM, dtype=F32))
    mq = mq.reshape(A_HEADS * 256, A_HEADS * HEAD_DIM).astype(MXU_DTYPE)
    wuv_t = jnp.transpose(w_uv, (1, 2, 0)).astype(MXU_DTYPE)
    per_q = lambda w: pl.BlockSpec((None, w, Q_BLOCK), lambda b, i: (b, 0, i))
    per_b = lambda *s: pl.BlockSpec((None,) + s, lambda b, i: (b,) + (0,) * len(s))
    const = lambda shape: pl.BlockSpec(shape, lambda b, i: (0,) * len(shape))
    return pl.pallas_call(
        functools.partial(_dsa_kernel, k_top=k_top, max_chunks=seq_pad // DSA_KEY_CHUNK),
        out_shape=jax.ShapeDtypeStruct((bsz, seq, A_HEADS * A_VDIM), MXU_DTYPE),
        grid=(bsz, nq),
        in_specs=[per_q(256), per_q(128), per_q(128), per_b(seq, 256), per_b(seq, 128),
                  per_b(seq // LANES, LANES, LANES), const(mq.shape), const(wuv_t.shape),
                  const((A_HEADS * A_VDIM, 1)), const(tri.shape)],
        out_specs=pl.BlockSpec((None, Q_BLOCK, A_HEADS * A_VDIM), lambda b, i: (b, i, 0)),
        scratch_shapes=[pltpu.VMEM((seq_pad, Q_BLOCK), I32),
                        pltpu.VMEM((1, NQ4), F32),
                        pltpu.VMEM((A_KV_RANK + ONES_ROWS, NQ4), F32)],
        compiler_params=_cparams(("arbitrary", "arbitrary")),
        name="dsa_attention",
    )(aq_t, iq_t, misc_t, akv, misc, ac_t, mq, wuv_t, gain.reshape(-1, 1), tri)


def _nsa_kernel(qT_ref, qrT_ref, miscT_ref, cmp_ref, vcT_ref, ks_ref, vsT_ref, kw_ref, vwT_ref,
                ov_ref, gain_ref, o_ref, sel_scr, m_scr, acc_scr, *, n_top, n_blk, max_chunks):
    i = pl.program_id(1)
    t0 = i * Q_BLOCK
    kc = NSA_KEY_CHUNK
    nch = (t0 + Q_BLOCK + kc - 1) // kc
    t_lane = t0 + lax.broadcasted_iota(I32, (1, Q_BLOCK), 1)
    gates = jax.nn.sigmoid(miscT_ref[MISC_GATE:MISC_GATE + 3 * C_HEADS, :])
    n_cmp = cmp_ref.shape[0]
    cmp_iota = lax.broadcasted_iota(I32, (n_cmp, Q_BLOCK), 0)
    blk_iota = lax.broadcasted_iota(I32, (n_blk, Q_BLOCK), 0)
    row_iota = lax.broadcasted_iota(I32, (kc, Q_BLOCK), 0)
    win_iota = lax.broadcasted_iota(I32, (Q_BLOCK, Q_BLOCK), 0)
    groups = range(C_KV_HEADS)
    gsl = [slice(g * HEAD_DIM, (g + 1) * HEAD_DIM) for g in groups]

    def heads_t(ref, g):
        return jnp.concatenate([ref[h * HEAD_DIM:(h + 1) * HEAD_DIM, :]
                                for h in range(g * C_GROUP, (g + 1) * C_GROUP)], axis=1)

    o_c = []
    for g in groups:
        s_c = _dot(cmp_ref[:, gsl[g]], heads_t(qT_ref, g))
        vis = _tile4(jnp.where(cmp_iota * CMP_STRIDE + (CMP_BLOCK - 1) <= t_lane, 1, 0)) > 0
        s_c = jnp.where(vis, s_c, NEG)
        p_c = jnp.where(vis, jnp.exp2(s_c - jnp.max(s_c, axis=0, keepdims=True)), 0.0)
        l_c = jnp.sum(p_c, axis=0, keepdims=True)
        p_c = p_c * (1.0 / jnp.maximum(l_c, 1e-30))
        o_c.append(_dot(vcT_ref[gsl[g], :], p_c))

        p_sum = (p_c[:, 0:Q_BLOCK] + p_c[:, Q_BLOCK:2 * Q_BLOCK]
                 + p_c[:, 2 * Q_BLOCK:3 * Q_BLOCK] + p_c[:, 3 * Q_BLOCK:4 * Q_BLOCK])
        p_hi = p_sum.astype(MXU_DTYPE)
        p_lo = p_sum - p_hi.astype(F32)
        imp = _dot(ov_ref[...], p_hi) + _dot(ov_ref[...], p_lo)
        cur = t_lane // SEL_BLOCK
        forced = jnp.where(blk_iota == 0, 1, jnp.where(blk_iota == cur, 1,
                           jnp.where(blk_iota == cur - 1, 1, 0)))
        imp = jnp.where(forced > 0, SEL_FORCE, imp)
        imp = jnp.where(blk_iota * SEL_BLOCK <= t_lane, imp, -jnp.inf)
        n_grp = n_blk // SUBLANES
        imp_g = [imp[v * SUBLANES:(v + 1) * SUBLANES, :] for v in range(n_grp)]
        rank_g = [jnp.zeros((SUBLANES, Q_BLOCK), I32) for _ in range(n_grp)]
        for mp in range(n_blk):
            row = imp[mp:mp + 1, :]
            for v in range(n_grp):
                if v > mp // SUBLANES:
                    beats = jnp.where(row >= imp_g[v], 1, 0)
                elif v < mp // SUBLANES:
                    beats = jnp.where(row > imp_g[v], 1, 0)
                else:
                    later = blk_iota[v * SUBLANES:(v + 1) * SUBLANES, :] > mp
                    beats = jnp.where(row > imp_g[v], 1, jnp.where(row == imp_g[v], jnp.where(later, 1, 0), 0))
                rank_g[v] = rank_g[v] + beats
        rank = jnp.concatenate(rank_g, axis=0)
        sel_scr[g, 0:n_blk, :] = jnp.where(rank < n_top, 0.0, NEG)

    m_scr[...] = jnp.full_like(m_scr, NEG)
    acc_scr[...] = jnp.zeros_like(acc_scr)
    bpc = kc // SEL_BLOCK

    def sel_body(c, diagonal):
        k0 = c * kc
        for g in groups:
            s = _dot(ks_ref[pl.ds(k0, kc), gsl[g]], heads_t(qrT_ref, g))
            sel8 = sel_scr[g, pl.ds(c * bpc, bpc), :]
            bias = jnp.concatenate(
                [jnp.broadcast_to(sel8[j:j + 1, :], (SEL_BLOCK, Q_BLOCK)) for j in range(bpc)], axis=0)
            if diagonal:
                bias = jnp.where(row_iota + k0 <= t_lane, bias, NEG)
            v_t = jnp.concatenate([vsT_ref[c * (kc // LANES) + j][gsl[g], :]
                                   for j in range(kc // LANES)], axis=1)
            _online_softmax_step(s, _tile4(bias), _with_ones(v_t), m_scr.at[g], acc_scr.at[g])

    def sweep(n_chunks):
        for c in range(n_chunks):
            sel_body(c, c == n_chunks - 1)
        return 0

    lax.switch(nch - 1, [functools.partial(sweep, n) for n in range(1, max_chunks + 1)])

    slabs = []
    for g in groups:
        acc = acc_scr[g]
        o_s = acc[:HEAD_DIM] / acc[HEAD_DIM:HEAD_DIM + 1]

        k_parts, v_parts, m_parts = [], [], []
        for j in range(WINDOW // Q_BLOCK + 1):
            cj = i - WINDOW // Q_BLOCK + j
            cjc = jnp.maximum(cj, 0)
            k0 = pl.multiple_of(cjc * Q_BLOCK, Q_BLOCK)
            k_parts.append(kw_ref[pl.ds(k0, Q_BLOCK), gsl[g]])
            v_parts.append(vwT_ref[cjc][gsl[g], :])
            kidx = win_iota + k0
            inside = jnp.where(kidx <= t_lane, jnp.where(kidx > t_lane - WINDOW, 0.0, NEG), NEG)
            m_parts.append(jnp.where(cj >= 0, inside, NEG))
        s_w = (_dot(jnp.concatenate(k_parts, axis=0), heads_t(qrT_ref, g))
               + _tile4(jnp.concatenate(m_parts, axis=0)))
        p_w = jnp.exp2(s_w - jnp.max(s_w, axis=0, keepdims=True))
        acc = _dot(_with_ones(jnp.concatenate(v_parts, axis=1)), p_w)
        o_w = acc[:HEAD_DIM] / acc[HEAD_DIM:HEAD_DIM + 1]

        for hh in range(C_GROUP):
            sl = slice(hh * Q_BLOCK, (hh + 1) * Q_BLOCK)
            r = (g * C_GROUP + hh) * 3
            slabs.append(gates[r:r + 1, :] * o_c[g][:, sl] + gates[r + 1:r + 2, :] * o_s[:, sl]
                         + gates[r + 2:r + 3, :] * o_w[:, sl])

    out_t = jnp.concatenate(slabs, axis=0)
    out_t = _rms(out_t, 0) * gain_ref[...]
    o_ref[...] = out_t.T.astype(o_ref.dtype)


def _sel_overlap_t(n_cmp_rows, n_blk):
    cs = np.arange(n_cmp_rows)[None, :] * CMP_STRIDE
    ss = np.arange(n_blk)[:, None] * SEL_BLOCK
    ov = np.minimum(cs + CMP_BLOCK, ss + SEL_BLOCK) - np.maximum(cs, ss)
    return np.clip(ov, 0, None).astype(np.float32) / CMP_BLOCK


def _nsa(cq_t, cqr_t, misc_t, cmp, vcmp_t, ks, vs_t, kw, vw_t, gain):
    bsz, _, seq = cq_t.shape
    nq = seq // Q_BLOCK
    nseg = cmp.shape[1]
    n_blk = seq // SEL_BLOCK
    n_top = min(SEL_TOPN, n_blk)
    seq_pad = -(-seq // NSA_KEY_CHUNK) * NSA_KEY_CHUNK
    ov_np = _sel_overlap_t(nseg, n_blk)
    ov_np[:, (seq - CMP_BLOCK) // CMP_STRIDE + 1:] = 0.0
    ov = jnp.asarray(ov_np, MXU_DTYPE)
    per_q = lambda w: pl.BlockSpec((None, w, Q_BLOCK), lambda b, i: (b, 0, i))
    per_b = lambda *s: pl.BlockSpec((None,) + s, lambda b, i: (b,) + (0,) * len(s))
    const = lambda shape: pl.BlockSpec(shape, lambda b, i: (0,) * len(shape))
    return pl.pallas_call(
        functools.partial(_nsa_kernel, n_top=n_top, n_blk=n_blk, max_chunks=seq_pad // NSA_KEY_CHUNK),
        out_shape=jax.ShapeDtypeStruct((bsz, seq, C_HEADS * HEAD_DIM), MXU_DTYPE),
        grid=(bsz, nq),
        in_specs=[per_q(512), per_q(512), per_q(128), per_b(nseg, 256), per_b(128, nseg),
                  per_b(seq, 128), per_b(seq // LANES, LANES, LANES),
                  per_b(seq, 128), per_b(seq // LANES, LANES, LANES),
                  const(ov.shape), const((C_HEADS * HEAD_DIM, 1))],
        out_specs=pl.BlockSpec((None, Q_BLOCK, C_HEADS * HEAD_DIM), lambda b, i: (b, i, 0)),
        scratch_shapes=[pltpu.VMEM((C_KV_HEADS, max(n_blk, seq_pad // SEL_BLOCK), Q_BLOCK), F32),
                        pltpu.VMEM((C_KV_HEADS, 1, NQ4), F32),
                        pltpu.VMEM((C_KV_HEADS, HEAD_DIM + ONES_ROWS, NQ4), F32)],
        compiler_params=_cparams(("arbitrary", "arbitrary")),
        name="nsa_attention",
    )(cq_t, cqr_t, misc_t, cmp, vcmp_t, ks, vs_t, kw, vw_t, ov, gain.reshape(-1, 1))


def _out_mlp_kernel(x_ref, a_ref, b_ref, c_ref, wo_ref, g2_ref, wu_ref, wd_ref, fg_ref,
                    o_ref, x1_scr, n2_scr, acc_scr, *, final_norm):
    j = pl.program_id(1)

    @pl.when(j == 0)
    def _():
        mixed = (jnp.dot(a_ref[...], wo_ref[0:256, :], preferred_element_type=F32)
                 + jnp.dot(b_ref[...], wo_ref[256:512, :], preferred_element_type=F32)
                 + jnp.dot(c_ref[...], wo_ref[512:1024, :], preferred_element_type=F32))
        x1 = x_ref[...] + mixed
        x1_scr[...] = x1
        n2_scr[...] = (_rms(x1, -1) * g2_ref[...]).astype(n2_scr.dtype)
        acc_scr[...] = jnp.zeros_like(acc_scr)

    z = jnp.dot(n2_scr[...], wu_ref[...], preferred_element_type=F32)
    acc_scr[...] += _dot(jnp.square(jnp.maximum(z, 0.0)), wd_ref[...])

    @pl.when(j == pl.num_programs(1) - 1)
    def _():
        x2 = x1_scr[...] + acc_scr[...]
        if final_norm:
            x2 = _rms(x2, -1) * fg_ref[...]
        o_ref[...] = x2


def _out_mlp(x, a_n, b_tm, c_n, w_out, ln2_g, w_up, w_down, final_g, final_norm):
    bsz, seq, _ = x.shape
    tm = min(MLP_TM, seq)
    nj = seq // tm
    nf = D_FF // MLP_TF
    rows = lambda w: pl.BlockSpec((None, tm, w), lambda r, f: (r // nj, r % nj, 0))
    const = lambda shape: pl.BlockSpec(shape, lambda r, f: (0,) * len(shape))
    return pl.pallas_call(
        functools.partial(_out_mlp_kernel, final_norm=final_norm),
        out_shape=jax.ShapeDtypeStruct(x.shape, F32),
        grid=(bsz * nj, nf),
        in_specs=[rows(D_MODEL), rows(256),
                  pl.BlockSpec((tm, B_WIDTH), lambda r, f: (r % nj, r // nj)),
                  rows(512), const((D_MODEL, D_MODEL)), const((1, D_MODEL)),
                  pl.BlockSpec((D_MODEL, MLP_TF), lambda r, f: (0, f)),
                  pl.BlockSpec((MLP_TF, D_MODEL), lambda r, f: (f, 0)),
                  const((1, D_MODEL))],
        out_specs=rows(D_MODEL),
        scratch_shapes=[pltpu.VMEM((tm, D_MODEL), F32), pltpu.VMEM((tm, D_MODEL), MXU_DTYPE),
                        pltpu.VMEM((tm, D_MODEL), F32)],
        compiler_params=_cparams(("arbitrary", "arbitrary")),
        name="out_proj_mlp",
    )(x, a_n, b_tm, c_n, w_out.astype(MXU_DTYPE), ln2_g.reshape(1, -1), w_up.astype(MXU_DTYPE),
      w_down.astype(MXU_DTYPE), final_g.reshape(1, -1))


def kernel(x, positions, ln1_g, w_in, kv_norm_g, w_uk, w_uv, s5_lambda_re, s5_lambda_im, s5_log_step, s5_b_re, s5_b_im, s5_c_re, s5_c_im, s5_d, s5_glu_w, s5_glu_b, cmp_pos_k, cmp_pos_v, cmp_k_w1, cmp_k_w2, cmp_v_w1, cmp_v_w2, gain_a, gain_b, gain_c, w_out, ln2_g, w_up, w_down, final_g):
    bsz, seq, _ = x.shape
    depth = w_in.shape[0]
    src, partner, pat_np = _proj_layout()
    pat = jnp.asarray(pat_np, jnp.bfloat16)
    tab = _rope_table(positions)

    def regather(w, idx):
        cols = jnp.take(w, jnp.asarray(np.maximum(idx, 0)), axis=1)
        return jnp.where(jnp.asarray(idx >= 0)[None, :], cols, 0.0).astype(MXU_DTYPE)

    for layer in range(depth):
        w_r = regather(w_in[layer], src)
        w_sw = regather(w_in[layer], partner)
        (aq_t, akv, ac_t, iq_t, misc, misc_t, u_tm, cq_t, cqr_t, kvc, ks, vs_t, kw, vw_t) = _in_proj(
            x, tab, ln1_g[layer], kv_norm_g[layer], w_r, w_sw, pat)

        ar, ai, bbr, bbi = _s5_discretize(s5_lambda_re[layer], s5_lambda_im[layer], s5_log_step[layer],
                                          s5_b_re[layer], s5_b_im[layer])
        b_tm = _s5(u_tm, bsz, ar, ai, bbr, bbi, s5_c_re[layer], s5_c_im[layer], s5_d[layer].reshape(-1),
                   s5_glu_w[layer], s5_glu_b[layer], gain_b[layer])

        cmp, vcmp_t = _compress(kvc, cmp_pos_k[layer], cmp_pos_v[layer], cmp_k_w1[layer],
                                cmp_k_w2[layer], cmp_v_w1[layer], cmp_v_w2[layer])
        a_n = _dsa(aq_t, iq_t, misc_t, akv, misc, ac_t, w_uk[layer], w_uv[layer], gain_a[layer])
        c_n = _nsa(cq_t, cqr_t, misc_t, cmp, vcmp_t, ks, vs_t, kw, vw_t, gain_c[layer])

        x = _out_mlp(x, a_n, b_tm, c_n, w_out[layer], ln2_g[layer], w_up[layer], w_down[layer],
                     final_g, final_norm=(layer == depth - 1))
    return x
```

```python
import functools
import math

import numpy as np
import jax
import jax.numpy as jnp
from jax import lax
from jax.experimental import pallas as pl
from jax.experimental.pallas import tpu as pltpu

F32 = jnp.float32
I32 = jnp.int32
MXU_DTYPE = jnp.bfloat16

D_MODEL = 1024
HEAD_DIM = 64
ROPE_THETA = 500000.0
ROPE_DIM = HEAD_DIM // 4
NORM_EPS = 1e-6
Q_BLOCK = 128
NEG = -1e30
LOG2E = math.log2(math.e)
D_FF = 4 * D_MODEL

A_HEADS = 4
A_NOPE = HEAD_DIM - ROPE_DIM
A_VDIM = HEAD_DIM
A_KV_RANK = 128
IDX_HEADS = 4
IDX_DIM = 32
IDX_ROPE = IDX_DIM // 4
DSA_TOPK = 256

B_WIDTH = 256
S5_GROUP = 16
S5_GROUPS = B_WIDTH // S5_GROUP
S5_STATE = 64
S5_NSTATE = S5_GROUPS * S5_STATE

C_HEADS = 8
C_KV_HEADS = 2
C_GROUP = C_HEADS // C_KV_HEADS
CMP_BLOCK = 32
CMP_STRIDE = 16
SEL_BLOCK = 64
SEL_TOPN = 16
SEL_FORCE = 1e9
WINDOW = 512

IN_SIZES = (A_HEADS * HEAD_DIM, A_KV_RANK, ROPE_DIM, IDX_HEADS * IDX_DIM, IDX_DIM, IDX_HEADS,
            B_WIDTH, C_HEADS * HEAD_DIM, 6 * C_KV_HEADS * HEAD_DIM, 3 * C_HEADS)

LANES = 128
SUBLANES = 8
VMEM_LIMIT = 56 * 1024 * 1024

SEG_AQ, SEG_AKV, SEG_IQ, SEG_MISC, SEG_BU, SEG_CQ, SEG_KVC, SEG_KS, SEG_VS, SEG_KW, SEG_VW = (
    0, 256, 512, 640, 768, 1024, 1536, 1792, 1920, 2048, 2176)
N_PROJ = 2304
MISC_IK, MISC_IW, MISC_GATE = 0, 32, 36
TAB_COS, TAB_SIN, TAB_ICOS, TAB_ISIN, TAB_ONE = 0, 8, 16, 20, 24
TAB_PART = 32

PROJ_TM = 512
DSA_KEY_CHUNK = 512
NSA_KEY_CHUNK = 1024
S5_TC = 64
MLP_TM = 1024
MLP_TF = 1024


def _cparams(sem):
    return pltpu.CompilerParams(dimension_semantics=sem, vmem_limit_bytes=VMEM_LIMIT)


def _dot(a, b):
    return jnp.dot(a.astype(MXU_DTYPE), b.astype(MXU_DTYPE), preferred_element_type=F32)


def _rms(x, axis):
    return x * lax.rsqrt(jnp.mean(x * x, axis=axis, keepdims=True) + NORM_EPS)


def _rope_tab_kernel(pos_ref, freq_ref, out_ref):
    ang = pos_ref[...].astype(F32) * freq_ref[...]
    lane = lax.broadcasted_iota(I32, ang.shape, 1)
    l32 = lane % TAB_PART
    is_cos = (l32 < TAB_SIN) | ((l32 >= TAB_ICOS) & (l32 < TAB_ISIN))
    is_sin = ((l32 >= TAB_SIN) & (l32 < TAB_ICOS)) | ((l32 >= TAB_ISIN) & (l32 < TAB_ONE))
    val = jnp.where(is_cos, jnp.cos(ang),
                    jnp.where(is_sin, jnp.sin(ang), jnp.where(l32 == TAB_ONE, 1.0, 0.0)))
    hi = val.astype(jnp.bfloat16).astype(F32)
    r1 = val - hi
    mid = r1.astype(jnp.bfloat16).astype(F32)
    lo = r1 - mid
    part = lane // TAB_PART
    out = jnp.where(part == 0, hi, jnp.where(part == 1, mid, jnp.where(part == 2, lo, 0.0)))
    out_ref[...] = out.astype(jnp.bfloat16)


def _rope_table(positions):
    bsz, seq = positions.shape
    t = bsz * seq
    inv_r = (np.float32(ROPE_THETA) ** (-np.arange(0, ROPE_DIM, 2, dtype=np.float32) / ROPE_DIM))
    inv_i = (np.float32(ROPE_THETA) ** (-np.arange(0, IDX_ROPE, 2, dtype=np.float32) / IDX_ROPE))
    f32 = np.zeros(TAB_PART, np.float32)
    f32[TAB_COS:TAB_COS + 8] = inv_r
    f32[TAB_SIN:TAB_SIN + 8] = inv_r
    f32[TAB_ICOS:TAB_ICOS + 4] = inv_i
    f32[TAB_ISIN:TAB_ISIN + 4] = inv_i
    freq = jnp.asarray(np.tile(f32, LANES // TAB_PART)[None, :])
    tq = min(1024, t)
    return pl.pallas_call(
        _rope_tab_kernel,
        out_shape=jax.ShapeDtypeStruct((t, LANES), jnp.bfloat16),
        grid=(t // tq,),
        in_specs=[pl.BlockSpec((tq, 1), lambda i: (i, 0)),
                  pl.BlockSpec((1, LANES), lambda i: (0, 0))],
        out_specs=pl.BlockSpec((tq, LANES), lambda i: (i, 0)),
        compiler_params=_cparams(("arbitrary",)),
        name="rope_table",
    )(positions.reshape(t, 1), freq).reshape(bsz, seq, LANES)


PAT_HEAD, PAT_KROPE, PAT_IDX, PAT_MISC = range(4)
N_PAT = 4


def _proj_layout():
    offs = np.concatenate([[0], np.cumsum(IN_SIZES)])
    o_aq, o_ckv, o_kr, o_iq, o_ik, o_iw, o_bu, o_cq, o_kv, o_gate = offs[:10]
    src = -np.ones(N_PROJ, np.int64)
    partner = -np.ones(N_PROJ, np.int64)
    clane = np.full(N_PROJ, TAB_ONE, np.int64)
    slane = -np.ones(N_PROJ, np.int64)
    ssign = np.zeros(N_PROJ, np.float32)

    def plain(c0, o0, w):
        src[c0:c0 + w] = np.arange(o0, o0 + w)

    def rope(c0, o0, half, cos_lane, sin_lane):
        for j in range(half):
            partner[c0 + j] = o0 + half + j
            partner[c0 + half + j] = o0 + j
            clane[c0 + j] = clane[c0 + half + j] = cos_lane + j
            slane[c0 + j] = slane[c0 + half + j] = sin_lane + j
            ssign[c0 + j] = -1.0
            ssign[c0 + half + j] = 1.0

    plain(SEG_AQ, o_aq, A_HEADS * HEAD_DIM)
    for h in range(A_HEADS):
        rope(SEG_AQ + h * HEAD_DIM, o_aq + h * HEAD_DIM, ROPE_DIM // 2, TAB_COS, TAB_SIN)
    plain(SEG_AKV, o_ckv, A_KV_RANK)
    plain(SEG_AKV + A_KV_RANK, o_kr, ROPE_DIM)
    rope(SEG_AKV + A_KV_RANK, o_kr, ROPE_DIM // 2, TAB_COS, TAB_SIN)
    plain(SEG_IQ, o_iq, IDX_HEADS * IDX_DIM)
    for h in range(IDX_HEADS):
        rope(SEG_IQ + h * IDX_DIM, o_iq + h * IDX_DIM, IDX_ROPE // 2, TAB_ICOS, TAB_ISIN)
    plain(SEG_MISC + MISC_IK, o_ik, IDX_DIM)
    rope(SEG_MISC + MISC_IK, o_ik, IDX_ROPE // 2, TAB_ICOS, TAB_ISIN)
    plain(SEG_MISC + MISC_IW, o_iw, IDX_HEADS)
    plain(SEG_MISC + MISC_GATE, o_gate, 3 * C_HEADS)
    plain(SEG_BU, o_bu, B_WIDTH)
    plain(SEG_CQ, o_cq, C_HEADS * HEAD_DIM)
    for h in range(C_HEADS):
        rope(SEG_CQ + h * HEAD_DIM, o_cq + h * HEAD_DIM, ROPE_DIM // 2, TAB_COS, TAB_SIN)
    plain(SEG_KVC, o_kv, 6 * C_KV_HEADS * HEAD_DIM)
    for seg, sub in ((SEG_KS, 2), (SEG_KW, 4)):
        for g in range(C_KV_HEADS):
            rope(seg + g * HEAD_DIM, o_kv + sub * C_KV_HEADS * HEAD_DIM + g * HEAD_DIM,
                 ROPE_DIM // 2, TAB_COS, TAB_SIN)

    k = np.arange(LANES)[:, None]
    live = k < 3 * TAB_PART
    ec = (live & ((k % TAB_PART) == clane[None, :])).astype(np.float32)
    es = (live & ((k % TAB_PART) == slane[None, :])).astype(np.float32) * ssign[None, :]
    starts = {PAT_HEAD: SEG_AQ, PAT_KROPE: SEG_AKV + A_KV_RANK, PAT_IDX: SEG_IQ, PAT_MISC: SEG_MISC}
    blocks = [m[:, starts[p]:starts[p] + LANES] for m in (ec, es) for p in range(N_PAT)]
    return src, partner, np.concatenate(blocks, axis=1)


def _in_proj_kernel(x_ref, tab_ref, g_ref, kvg_ref, w_ref, wsw_ref, pat_ref,
                    aqT_ref, akv_ref, acT_ref, iqT_ref, misc_ref, miscT_ref, u_ref,
                    cqT_ref, cqrT_ref, kvc_ref, ks_ref, vsT_ref, kw_ref, vwT_ref):
    x = x_ref[...]
    n = (_rms(x, -1) * g_ref[...]).astype(MXU_DTYPE)
    cs = jnp.dot(tab_ref[...], pat_ref[...], preferred_element_type=F32)

    def plain(c0, w):
        return jnp.dot(n, w_ref[:, c0:c0 + w], preferred_element_type=F32)

    def rotate(p, c0, pat):
        nb = p.shape[1] // LANES
        psw = jnp.dot(n, wsw_ref[:, c0:c0 + p.shape[1]], preferred_element_type=F32)
        cos = cs[:, pat * LANES:(pat + 1) * LANES]
        sin = cs[:, (N_PAT + pat) * LANES:(N_PAT + pat + 1) * LANES]
        return p * jnp.concatenate([cos] * nb, axis=1) + psw * jnp.concatenate([sin] * nb, axis=1)

    def chunked_t(val, ref):
        vt = val.T.astype(ref.dtype)
        for c in range(ref.shape[0]):
            ref[c] = vt[:, c * LANES:(c + 1) * LANES]

    aqT_ref[...] = rotate(plain(SEG_AQ, 256), SEG_AQ, PAT_HEAD).T.astype(aqT_ref.dtype)

    akv = plain(SEG_AKV, 256)
    lat = _rms(akv[:, :A_KV_RANK], -1) * kvg_ref[...]
    k_rope = rotate(akv[:, A_KV_RANK:], SEG_AKV + A_KV_RANK, PAT_KROPE)
    akv_ref[...] = jnp.concatenate([lat, k_rope], axis=1).astype(akv_ref.dtype)
    chunked_t(lat, acT_ref)

    iqT_ref[...] = rotate(plain(SEG_IQ, 128), SEG_IQ, PAT_IDX).T.astype(iqT_ref.dtype)
    misc = rotate(plain(SEG_MISC, 128), SEG_MISC, PAT_MISC)
    misc_ref[...] = misc
    miscT_ref[...] = misc.T
    u_ref[...] = plain(SEG_BU, 256)
    qk_scale = HEAD_DIM ** -0.5 * LOG2E
    cq = plain(SEG_CQ, 512)
    cqT_ref[...] = (cq * qk_scale).T.astype(cqT_ref.dtype)
    cqrT_ref[...] = (rotate(cq, SEG_CQ, PAT_HEAD) * qk_scale).T.astype(cqrT_ref.dtype)
    kvc_ref[...] = plain(SEG_KVC, 256)
    ks_ref[...] = rotate(plain(SEG_KS, 128), SEG_KS, PAT_HEAD).astype(ks_ref.dtype)
    chunked_t(plain(SEG_VS, 128), vsT_ref)
    kw_ref[...] = rotate(plain(SEG_KW, 128), SEG_KW, PAT_HEAD).astype(kw_ref.dtype)
    chunked_t(plain(SEG_VW, 128), vwT_ref)


def _in_proj(x, tab, ln_g, kv_g, w_r, w_sw, pat):
    bsz, seq, _ = x.shape
    tm = min(PROJ_TM, seq)
    nj = seq // tm
    nck = tm // LANES
    bf = MXU_DTYPE

    def rows(w):
        return pl.BlockSpec((None, tm, w), lambda b, j: (b, j, 0))

    def cols(w):
        return pl.BlockSpec((None, w, tm), lambda b, j: (b, 0, j))

    def chunks():
        return pl.BlockSpec((None, nck, LANES, LANES), lambda b, j: (b, j, 0, 0))

    def const(shape):
        return pl.BlockSpec(shape, lambda b, j: (0,) * len(shape))

    sds = jax.ShapeDtypeStruct
    out_shape = (
        sds((bsz, 256, seq), bf),
        sds((bsz, seq, 256), bf),
        sds((bsz, seq // LANES, LANES, LANES), bf),
        sds((bsz, 128, seq), bf),
        sds((bsz, seq, 128), F32),
        sds((bsz, 128, seq), F32),
        sds((seq, bsz * B_WIDTH), F32),
        sds((bsz, 512, seq), bf),
        sds((bsz, 512, seq), bf),
        sds((bsz, seq, 256), F32),
        sds((bsz, seq, 128), bf),
        sds((bsz, seq // LANES, LANES, LANES), bf),
        sds((bsz, seq, 128), bf),
        sds((bsz, seq // LANES, LANES, LANES), bf),
    )
    out_specs = (cols(256), rows(256), chunks(), cols(128), rows(128), cols(128),
                 pl.BlockSpec((tm, B_WIDTH), lambda b, j: (j, b)),
                 cols(512), cols(512), rows(256), rows(128), chunks(), rows(128), chunks())
    return pl.pallas_call(
        _in_proj_kernel,
        out_shape=out_shape,
        grid=(bsz, nj),
        in_specs=[rows(D_MODEL), rows(LANES), const((1, D_MODEL)), const((1, A_KV_RANK)),
                  const((D_MODEL, N_PROJ)), const((D_MODEL, N_PROJ)),
                  const((LANES, 2 * N_PAT * LANES))],
        out_specs=out_specs,
        compiler_params=_cparams(("arbitrary", "arbitrary")),
        name="in_proj",
    )(x, tab, ln_g.reshape(1, -1), kv_g.reshape(1, -1), w_r, w_sw, pat)


def _s5_disc_kernel(lr_ref, li_ref, ls_ref, br_ref, bi_ref, ar_ref, ai_ref, bbr_ref, bbi_ref):
    lr, li = lr_ref[...], li_ref[...]
    step = jnp.exp(ls_ref[...])
    mag = jnp.exp(lr * step)
    ar = mag * jnp.cos(li * step)
    ai = mag * jnp.sin(li * step)
    den = lr * lr + li * li
    zr = ((ar - 1.0) * lr + ai * li) / den
    zi = (ai * lr - (ar - 1.0) * li) / den
    br, bi = br_ref[...], bi_ref[...]
    ar_ref[...] = ar
    ai_ref[...] = ai
    bbr_ref[...] = zr * br - zi * bi
    bbi_ref[...] = zr * bi + zi * br


def _s5_discretize(lam_re, lam_im, log_step, b_re, b_im):
    g, p, h = b_re.shape
    ex = lambda a: jnp.repeat(a.astype(F32), h, axis=1)
    ls = jnp.broadcast_to(log_step.astype(F32)[:, None], (g, p * h))
    sds = jax.ShapeDtypeStruct((g, p * h), F32)
    ar, ai, bbr, bbi = pl.pallas_call(
        _s5_disc_kernel, out_shape=(sds, sds, sds, sds), name="s5_discretize",
    )(ex(lam_re), ex(lam_im), ls, b_re.astype(F32).reshape(g, p * h), b_im.astype(F32).reshape(g, p * h))
    ar = ar.reshape(g, p, h)[:, :, 0]
    ai = ai.reshape(g, p, h)[:, :, 0]
    return ar, ai, bbr.reshape(g, p, h), bbi.reshape(g, p, h)


def _s5_kernel(u_ref, bmat_ref, a_ref, cre_ref, cim_ref, d_ref, gw_ref, gb_ref, gain_ref,
               o_ref, u_scr, x_scr, h_scr, *, tc):
    ns = S5_NSTATE

    @pl.when(pl.program_id(0) == 0)
    def _():
        h_scr[...] = jnp.zeros_like(h_scr)

    halves = range(B_WIDTH // LANES)
    for b in range(SUBLANES):
        for hf in halves:
            c0 = b * B_WIDTH + hf * LANES
            u_scr[hf, pl.ds(b, tc, stride=SUBLANES), :] = u_ref[:, c0:c0 + LANES]
    u = jnp.concatenate([u_scr[hf] for hf in halves], axis=1)
    x_scr[...] = _dot(u, bmat_ref[...])
    ar = a_ref[0:SUBLANES, :]
    ai = a_ref[SUBLANES:2 * SUBLANES, :]

    def step(t, carry):
        hr, hi = carry
        r0 = pl.multiple_of(t * SUBLANES, SUBLANES)
        xr = x_scr[pl.ds(r0, SUBLANES), 0:ns]
        xi = x_scr[pl.ds(r0, SUBLANES), ns:2 * ns]
        nhr = ar * hr - ai * hi + xr
        nhi = ar * hi + ai * hr + xi
        x_scr[pl.ds(r0, SUBLANES), 0:ns] = nhr
        x_scr[pl.ds(r0, SUBLANES), ns:2 * ns] = nhi
        return nhr, nhi

    hr, hi = lax.fori_loop(0, tc, step, (h_scr[:, 0:ns], h_scr[:, ns:2 * ns]), unroll=8)
    h_scr[:, 0:ns] = hr
    h_scr[:, ns:2 * ns] = hi

    y = (_dot(x_scr[:, 0:ns], cre_ref[...]) - _dot(x_scr[:, ns:2 * ns], cim_ref[...])
         + d_ref[...] * u)
    y = jax.nn.gelu(y)
    y = y * jax.nn.sigmoid(_dot(y, gw_ref[...]) + gb_ref[...])
    y = _rms(y, -1) * gain_ref[...]
    for hf in halves:
        u_scr[hf] = y[:, hf * LANES:(hf + 1) * LANES]
    for b in range(SUBLANES):
        for hf in halves:
            c0 = b * B_WIDTH + hf * LANES
            o_ref[:, c0:c0 + LANES] = u_scr[hf, pl.ds(b, tc, stride=SUBLANES), :].astype(o_ref.dtype)


def _s5(u_tm, bsz, ar, ai, bbr, bbi, c_re, c_im, d, glu_w, glu_b, gain):
    assert bsz == SUBLANES, "the S5 scan keeps one batch row per sublane"
    seq = u_tm.shape[0]
    tc = min(S5_TC, seq)
    rows = tc * bsz
    eye = jnp.eye(S5_GROUPS, dtype=F32)
    bmat = jnp.concatenate(
        [jnp.einsum('gph,gk->ghkp', b, eye).reshape(B_WIDTH, S5_NSTATE) for b in (bbr, bbi)], axis=1)
    cre = jnp.einsum('ghp,gk->gpkh', c_re.astype(F32), eye).reshape(S5_NSTATE, B_WIDTH)
    cim = jnp.einsum('ghp,gk->gpkh', c_im.astype(F32), eye).reshape(S5_NSTATE, B_WIDTH)
    avec = jnp.concatenate([jnp.broadcast_to(a.reshape(1, S5_NSTATE), (SUBLANES, S5_NSTATE))
                            for a in (ar, ai)], axis=0)
    const = lambda shape: pl.BlockSpec(shape, lambda i: (0,) * len(shape))
    out = pl.pallas_call(
        functools.partial(_s5_kernel, tc=tc),
        out_shape=jax.ShapeDtypeStruct((seq, bsz * B_WIDTH), MXU_DTYPE),
        grid=(seq // tc,),
        in_specs=[pl.BlockSpec((tc, bsz * B_WIDTH), lambda i: (i, 0)),
                  const((B_WIDTH, 2 * S5_NSTATE)), const((2 * SUBLANES, S5_NSTATE)),
                  const((S5_NSTATE, B_WIDTH)), const((S5_NSTATE, B_WIDTH)),
                  const((1, B_WIDTH)), const((B_WIDTH, B_WIDTH)), const((1, B_WIDTH)),
                  const((1, B_WIDTH))],
        out_specs=pl.BlockSpec((tc, bsz * B_WIDTH), lambda i: (i, 0)),
        scratch_shapes=[pltpu.VMEM((B_WIDTH // LANES, rows, LANES), F32),
                        pltpu.VMEM((rows, 2 * S5_NSTATE), F32),
                        pltpu.VMEM((SUBLANES, 2 * S5_NSTATE), F32)],
        compiler_params=_cparams(("arbitrary",)),
        name="s5_scan",
    )(u_tm, bmat.astype(MXU_DTYPE), avec,
      cre.astype(MXU_DTYPE), cim.astype(MXU_DTYPE), d.astype(F32).reshape(1, B_WIDTH),
      glu_w.astype(MXU_DTYPE), glu_b.astype(F32).reshape(1, B_WIDTH), gain.reshape(1, B_WIDTH))
    return out


def _compress_kernel(k_ref, v_ref, pa_ref, pb_ref, wa_ref, wb_ref, w2_ref, cmp_ref, vT_ref):
    nseg = cmp_ref.shape[0]
    cw = k_ref.shape[1] + v_ref.shape[1]
    xa = jnp.zeros((nseg, cw), F32)
    xb = jnp.zeros((nseg, cw), F32)
    for l in range(CMP_STRIDE):
        rows = pl.ds(l, nseg, stride=CMP_STRIDE)
        tok = jnp.concatenate([k_ref[rows, :], v_ref[rows, :]], axis=1)
        cols = slice(l * cw, (l + 1) * cw)
        xa = xa + _dot(tok + pa_ref[:, cols], wa_ref[cols, :])
        xb = xb + _dot(tok + pb_ref[:, cols], wb_ref[cols, :])
    pre = xa + pltpu.roll(xb, nseg - 1, 0)
    out = _dot(jax.nn.gelu(pre), w2_ref[...])
    cmp_ref[...] = out.astype(cmp_ref.dtype)
    vT_ref[...] = out[:, LANES:].T.astype(vT_ref.dtype)


def _compress(kvc, pos_k, pos_v, k_w1, k_w2, v_w1, v_w2):
    bsz, seq, _ = kvc.shape
    nseg = seq // CMP_STRIDE
    width = CMP_STRIDE * 256
    eye = jnp.eye(4, dtype=F32)
    w1 = jnp.stack([k_w1, k_w1, v_w1, v_w1]).astype(F32).reshape(4, CMP_BLOCK, HEAD_DIM, HEAD_DIM)
    pos = jnp.stack([pos_k, pos_k, pos_v, pos_v]).astype(F32)

    def half(lo):
        w = jnp.einsum('slde,st->lsdte', w1[:, lo:lo + CMP_STRIDE], eye).reshape(width, 256)
        p = jnp.transpose(pos[:, lo:lo + CMP_STRIDE], (1, 0, 2)).reshape(1, width)
        return w.astype(MXU_DTYPE), p

    wa, pa = half(0)
    wb, pb = half(CMP_STRIDE)
    w2 = jnp.einsum('sde,st->sdte', jnp.stack([k_w2, k_w2, v_w2, v_w2]).astype(F32), eye).reshape(256, 256)
    const = lambda shape: pl.BlockSpec(shape, lambda b: (0,) * len(shape))
    return pl.pallas_call(
        _compress_kernel,
        out_shape=(jax.ShapeDtypeStruct((bsz, nseg, 256), MXU_DTYPE),
                   jax.ShapeDtypeStruct((bsz, 128, nseg), MXU_DTYPE)),
        grid=(bsz,),
        in_specs=[pl.BlockSpec((None, seq, LANES), lambda b: (b, 0, 0)),
                  pl.BlockSpec((None, seq, LANES), lambda b: (b, 0, 1)),
                  const((1, width)), const((1, width)), const((width, 256)), const((width, 256)),
                  const((256, 256))],
        out_specs=(pl.BlockSpec((None, nseg, 256), lambda b: (b, 0, 0)),
                   pl.BlockSpec((None, 128, nseg), lambda b: (b, 0, 0))),
        compiler_params=_cparams(("arbitrary",)),
        name="nsa_compress",
    )(kvc, kvc, pa, pb, wa, wb, w2.astype(MXU_DTYPE))


ONES_ROWS = 16
NQ4 = 4 * Q_BLOCK


def _with_ones(v_t):
    return jnp.concatenate([v_t, jnp.ones((ONES_ROWS, v_t.shape[1]), v_t.dtype)], axis=0)


def _online_softmax_step(s, bias, v_ext, m_ref, acc_ref):
    s = s + bias
    m_old = m_ref[...]
    m_new = jnp.maximum(m_old, jnp.max(s, axis=0, keepdims=True))
    p = jnp.exp2(s - m_new)
    acc_ref[...] = jnp.exp2(m_old - m_new) * acc_ref[...] + _dot(v_ext, p)
    m_ref[...] = m_new


def _tile4(x):
    return jnp.concatenate([x, x, x, x], axis=1)


INT_MIN = -2 ** 31
NEG_INF_KEY = int(np.int32(np.uint32(0xFF800000 ^ 0x7FFFFFFF)))


def _dsa_kernel(qT_ref, iqT_ref, miscT_ref, kv_ref, misc_ref, cT_ref, mq_ref, wuv_ref, gain_ref, tri_ref,
                o_ref, key_scr, m_scr, acc_scr, *, k_top, max_chunks):
    i = pl.program_id(1)
    t0 = i * Q_BLOCK
    kc = DSA_KEY_CHUNK
    nch = (t0 + Q_BLOCK + kc - 1) // kc
    t_lane = t0 + lax.broadcasted_iota(I32, (1, Q_BLOCK), 1)
    row_iota = lax.broadcasted_iota(I32, (kc, Q_BLOCK), 0)

    qcat = (_dot(mq_ref[...], qT_ref[...]) * (HEAD_DIM ** -0.5 * LOG2E)).astype(MXU_DTYPE)
    qs_t = jnp.concatenate([qcat[h * 256:(h + 1) * 256] for h in range(A_HEADS)], axis=1)
    iq_t = iqT_ref[...]
    iq_all = jnp.concatenate([iq_t[h * IDX_DIM:(h + 1) * IDX_DIM, :] for h in range(IDX_HEADS)], axis=1)
    w_t = miscT_ref[MISC_IW:MISC_IW + IDX_HEADS, :] * (IDX_HEADS ** -0.5 * IDX_DIM ** -0.5)

    def idx_body(c, diagonal):
        k0 = c * kc
        ik = misc_ref[pl.ds(k0, kc), MISC_IK:MISC_IK + IDX_DIM]
        d = _dot(ik, iq_all)
        score = jnp.zeros((kc, Q_BLOCK), F32)
        for h in range(IDX_HEADS):
            score = score + jnp.maximum(d[:, h * Q_BLOCK:(h + 1) * Q_BLOCK], 0.0) * w_t[h:h + 1, :]
        if diagonal:
            score = jnp.where(row_iota + k0 <= t_lane, score, -jnp.inf)
        bits = pltpu.bitcast(score, I32)
        key_scr[pl.ds(k0, kc), :] = bits ^ ((bits >> 31) & 0x7FFFFFFF)

    m_scr[...] = jnp.full_like(m_scr, NEG)
    acc_scr[...] = jnp.zeros_like(acc_scr)

    def passes(n_chunks):
        for c in range(n_chunks):
            idx_body(c, c == n_chunks - 1)

        def count(pred):
            acc = jnp.zeros((SUBLANES, Q_BLOCK), I32)
            for c in range(n_chunks):
                hit = pred(key_scr[c * kc:(c + 1) * kc, :])
                acc = acc + jnp.sum(hit.reshape(kc // SUBLANES, SUBLANES, Q_BLOCK), axis=0)
            return jnp.sum(acc, axis=0, keepdims=True)

        def thr_bit(b, prefix):
            cand = prefix | lax.shift_left(jnp.int32(1), 31 - b)
            cand_s = cand ^ INT_MIN
            cnt = count(lambda keys: jnp.where(keys >= cand_s, 1, 0))
            return jnp.where(cnt >= k_top, cand, prefix)

        thr = lax.fori_loop(0, 32, thr_bit, jnp.zeros((1, Q_BLOCK), I32)) ^ INT_MIN
        n_gt = count(lambda keys: jnp.where(keys > thr, 1, 0))
        need = jnp.where(thr == NEG_INF_KEY, 0, k_top - n_gt).astype(F32)

        n_tied = jnp.zeros((1, Q_BLOCK), F32)
        for c in range(n_chunks):
            keys = key_scr[c * kc:(c + 1) * kc, :]
            tied = _dot(tri_ref[...], jnp.where(keys == thr, 1.0, 0.0)) + n_tied
            bias = jnp.where(keys > thr, 0.0,
                             jnp.where(keys == thr, jnp.where(tied <= need, 0.0, NEG), NEG))
            v_t = jnp.concatenate([cT_ref[c * (kc // LANES) + j] for j in range(kc // LANES)], axis=1)
            s = _dot(kv_ref[c * kc:(c + 1) * kc, :], qs_t)
            _online_softmax_step(s, _tile4(bias), _with_ones(v_t), m_scr, acc_scr)
            n_tied = tied[kc - 1:kc, :]
        return 0

    lax.switch(nch - 1, [functools.partial(passes, n) for n in range(1, max_chunks + 1)])

    acc = acc_scr[...]
    o_lat = (acc[:A_KV_RANK] / acc[A_KV_RANK:A_KV_RANK + 1]).astype(MXU_DTYPE)
    out_t = jnp.concatenate(
        [_dot(wuv_ref[h], o_lat[:, h * Q_BLOCK:(h + 1) * Q_BLOCK]) for h in range(A_HEADS)], axis=0)
    out_t = _rms(out_t, 0) * gain_ref[...]
    o_ref[...] = out_t.T.astype(o_ref.dtype)


def _dsa(aq_t, iq_t, misc_t, akv, misc, ac_t, w_uk, w_uv, gain):
    bsz, _, seq = aq_t.shape
    nq = seq // Q_BLOCK
    k_top = min(DSA_TOPK, seq // 4)
    seq_pad = -(-seq // DSA_KEY_CHUNK) * DSA_KEY_CHUNK
    tri = jnp.asarray(np.tril(np.ones((DSA_KEY_CHUNK, DSA_KEY_CHUNK), np.float32)), MXU_DTYPE)
    mq = jnp.zeros((A_HEADS, 256, A_HEADS, HEAD_DIM), F32)
    for h in range(A_HEADS):
        mq = mq.at[h, :A_KV_RANK, h, ROPE_DIM:].set(w_uk[:, h, :].astype(F32))
        mq = mq.at[h, A_KV_RANK:A_KV_RANK + ROPE_DIM, h, :ROPE_DIM].set(jnp.eye(ROPE_DIM, dtype=F32))
    mq = mq.reshape(A_HEADS * 256, A_HEADS * HEAD_DIM).astype(MXU_DTYPE)
    wuv_t = jnp.transpose(w_uv, (1, 2, 0)).astype(MXU_DTYPE)
    per_q = lambda w: pl.BlockSpec((None, w, Q_BLOCK), lambda b, i: (b, 0, i))
    per_b = lambda *s: pl.BlockSpec((None,) + s, lambda b, i: (b,) + (0,) * len(s))
    const = lambda shape: pl.BlockSpec(shape, lambda b, i: (0,) * len(shape))
    return pl.pallas_call(
        functools.partial(_dsa_kernel, k_top=k_top, max_chunks=seq_pad // DSA_KEY_CHUNK),
        out_shape=jax.ShapeDtypeStruct((bsz, seq, A_HEADS * A_VDIM), MXU_DTYPE),
        grid=(bsz, nq),
        in_specs=[per_q(256), per_q(128), per_q(128), per_b(seq, 256), per_b(seq, 128),
                  per_b(seq // LANES, LANES, LANES), const(mq.shape), const(wuv_t.shape),
                  const((A_HEADS * A_VDIM, 1)), const(tri.shape)],
        out_specs=pl.BlockSpec((None, Q_BLOCK, A_HEADS * A_VDIM), lambda b, i: (b, i, 0)),
        scratch_shapes=[pltpu.VMEM((seq_pad, Q_BLOCK), I32),
                        pltpu.VMEM((1, NQ4), F32),
                        pltpu.VMEM((A_KV_RANK + ONES_ROWS, NQ4), F32)],
        compiler_params=_cparams(("arbitrary", "arbitrary")),
        name="dsa_attention",
    )(aq_t, iq_t, misc_t, akv, misc, ac_t, mq, wuv_t, gain.reshape(-1, 1), tri)


def _nsa_kernel(qT_ref, qrT_ref, miscT_ref, cmp_ref, vcT_ref, ks_ref, vsT_ref, kw_ref, vwT_ref,
                ov_ref, gain_ref, o_ref, sel_scr, m_scr, acc_scr, win_scr, *, n_top, n_blk, max_chunks):
    i = pl.program_id(1)
    t0 = i * Q_BLOCK
    kc = NSA_KEY_CHUNK
    nch = (t0 + Q_BLOCK + kc - 1) // kc
    t_lane = t0 + lax.broadcasted_iota(I32, (1, Q_BLOCK), 1)
    gates = jax.nn.sigmoid(miscT_ref[MISC_GATE:MISC_GATE + 3 * C_HEADS, :])
    n_cmp = cmp_ref.shape[0]
    cmp_iota = lax.broadcasted_iota(I32, (n_cmp, Q_BLOCK), 0)
    blk_iota = lax.broadcasted_iota(I32, (n_blk, Q_BLOCK), 0)
    row_iota = lax.broadcasted_iota(I32, (kc, Q_BLOCK), 0)
    win_iota = lax.broadcasted_iota(I32, (Q_BLOCK, Q_BLOCK), 0)
    groups = range(C_KV_HEADS)
    gsl = [slice(g * HEAD_DIM, (g + 1) * HEAD_DIM) for g in groups]

    def heads_t(ref, g):
        return jnp.concatenate([ref[h * HEAD_DIM:(h + 1) * HEAD_DIM, :]
                                for h in range(g * C_GROUP, (g + 1) * C_GROUP)], axis=1)

    o_c = []
    for g in groups:
        s_c = _dot(cmp_ref[:, gsl[g]], heads_t(qT_ref, g))
        vis = _tile4(jnp.where(cmp_iota * CMP_STRIDE + (CMP_BLOCK - 1) <= t_lane, 1, 0)) > 0
        s_c = jnp.where(vis, s_c, NEG)
        p_c = jnp.where(vis, jnp.exp2(s_c - jnp.max(s_c, axis=0, keepdims=True)), 0.0)
        l_c = jnp.sum(p_c, axis=0, keepdims=True)
        p_c = p_c * (1.0 / jnp.maximum(l_c, 1e-30))
        o_c.append(_dot(vcT_ref[gsl[g], :], p_c))

        p_sum = (p_c[:, 0:Q_BLOCK] + p_c[:, Q_BLOCK:2 * Q_BLOCK]
                 + p_c[:, 2 * Q_BLOCK:3 * Q_BLOCK] + p_c[:, 3 * Q_BLOCK:4 * Q_BLOCK])
        p_hi = p_sum.astype(MXU_DTYPE)
        p_lo = p_sum - p_hi.astype(F32)
        imp = _dot(ov_ref[...], p_hi) + _dot(ov_ref[...], p_lo)
        cur = t_lane // SEL_BLOCK
        forced = jnp.where(blk_iota == 0, 1, jnp.where(blk_iota == cur, 1,
                           jnp.where(blk_iota == cur - 1, 1, 0)))
        imp = jnp.where(forced > 0, SEL_FORCE, imp)
        imp = jnp.where(blk_iota * SEL_BLOCK <= t_lane, imp, -jnp.inf)
        n_grp = n_blk // SUBLANES
        imp_g = [imp[v * SUBLANES:(v + 1) * SUBLANES, :] for v in range(n_grp)]
        rank_g = [jnp.zeros((SUBLANES, Q_BLOCK), I32) for _ in range(n_grp)]
        for mp in range(n_blk):
            row = imp[mp:mp + 1, :]
            for v in range(n_grp):
                if v > mp // SUBLANES:
                    beats = jnp.where(row >= imp_g[v], 1, 0)
                elif v < mp // SUBLANES:
                    beats = jnp.where(row > imp_g[v], 1, 0)
                else:
                    later = blk_iota[v * SUBLANES:(v + 1) * SUBLANES, :] > mp
                    beats = jnp.where(row > imp_g[v], 1, jnp.where(row == imp_g[v], jnp.where(later, 1, 0), 0))
                rank_g[v] = rank_g[v] + beats
        rank = jnp.concatenate(rank_g, axis=0)
        sel_scr[g, 0:n_blk, :] = jnp.where(rank < n_top, 0.0, NEG)

    m_scr[...] = jnp.full_like(m_scr, NEG)
    acc_scr[...] = jnp.zeros_like(acc_scr)
    bpc = kc // SEL_BLOCK

    def sel_body(c, diagonal):
        k0 = c * kc
        for g in groups:
            s = _dot(ks_ref[pl.ds(k0, kc), gsl[g]], heads_t(qrT_ref, g))
            sel8 = sel_scr[g, pl.ds(c * bpc, bpc), :]
            bias = jnp.concatenate(
                [jnp.broadcast_to(sel8[j:j + 1, :], (SEL_BLOCK, Q_BLOCK)) for j in range(bpc)], axis=0)
            if diagonal:
                bias = jnp.where(row_iota + k0 <= t_lane, bias, NEG)
            v_t = jnp.concatenate([vsT_ref[c * (kc // LANES) + j][gsl[g], :]
                                   for j in range(kc // LANES)], axis=1)
            _online_softmax_step(s, _tile4(bias), _with_ones(v_t), m_scr.at[g], acc_scr.at[g])

    def window_branch(g):
        k_parts, v_parts, m_parts = [], [], []
        for j in range(WINDOW // Q_BLOCK + 1):
            cj = i - WINDOW // Q_BLOCK + j
            cjc = jnp.maximum(cj, 0)
            k0 = pl.multiple_of(cjc * Q_BLOCK, Q_BLOCK)
            k_parts.append(kw_ref[pl.ds(k0, Q_BLOCK), gsl[g]])
            v_parts.append(vwT_ref[cjc][gsl[g], :])
            kidx = win_iota + k0
            inside = jnp.where(kidx <= t_lane, jnp.where(kidx > t_lane - WINDOW, 0.0, NEG), NEG)
            m_parts.append(jnp.where(cj >= 0, inside, NEG))
        s_w = (_dot(jnp.concatenate(k_parts, axis=0), heads_t(qrT_ref, g))
               + _tile4(jnp.concatenate(m_parts, axis=0)))
        p_w = jnp.exp2(s_w - jnp.max(s_w, axis=0, keepdims=True))
        acc = _dot(_with_ones(jnp.concatenate(v_parts, axis=1)), p_w)
        win_scr[g] = acc[:HEAD_DIM] / acc[HEAD_DIM:HEAD_DIM + 1]

    def sweep(n_chunks):
        for g in groups:
            window_branch(g)
        for c in range(n_chunks):
            sel_body(c, c == n_chunks - 1)
        return 0

    lax.switch(nch - 1, [functools.partial(sweep, n) for n in range(1, max_chunks + 1)])

    slabs = []
    for g in groups:
        acc = acc_scr[g]
        o_s = acc[:HEAD_DIM] / acc[HEAD_DIM:HEAD_DIM + 1]
        o_w = win_scr[g]
        for hh in range(C_GROUP):
            sl = slice(hh * Q_BLOCK, (hh + 1) * Q_BLOCK)
            r = (g * C_GROUP + hh) * 3
            slabs.append(gates[r:r + 1, :] * o_c[g][:, sl] + gates[r + 1:r + 2, :] * o_s[:, sl]
                         + gates[r + 2:r + 3, :] * o_w[:, sl])

    out_t = jnp.concatenate(slabs, axis=0)
    out_t = _rms(out_t, 0) * gain_ref[...]
    o_ref[...] = out_t.T.astype(o_ref.dtype)


def _sel_overlap_t(n_cmp_rows, n_blk):
    cs = np.arange(n_cmp_rows)[None, :] * CMP_STRIDE
    ss = np.arange(n_blk)[:, None] * SEL_BLOCK
    ov = np.minimum(cs + CMP_BLOCK, ss + SEL_BLOCK) - np.maximum(cs, ss)
    return np.clip(ov, 0, None).astype(np.float32) / CMP_BLOCK


def _nsa(cq_t, cqr_t, misc_t, cmp, vcmp_t, ks, vs_t, kw, vw_t, gain):
    bsz, _, seq = cq_t.shape
    nq = seq // Q_BLOCK
    nseg = cmp.shape[1]
    n_blk = seq // SEL_BLOCK
    n_top = min(SEL_TOPN, n_blk)
    seq_pad = -(-seq // NSA_KEY_CHUNK) * NSA_KEY_CHUNK
    ov_np = _sel_overlap_t(nseg, n_blk)
    ov_np[:, (seq - CMP_BLOCK) // CMP_STRIDE + 1:] = 0.0
    ov = jnp.asarray(ov_np, MXU_DTYPE)
    per_q = lambda w: pl.BlockSpec((None, w, Q_BLOCK), lambda b, i: (b, 0, i))
    per_b = lambda *s: pl.BlockSpec((None,) + s, lambda b, i: (b,) + (0,) * len(s))
    const = lambda shape: pl.BlockSpec(shape, lambda b, i: (0,) * len(shape))
    return pl.pallas_call(
        functools.partial(_nsa_kernel, n_top=n_top, n_blk=n_blk, max_chunks=seq_pad // NSA_KEY_CHUNK),
        out_shape=jax.ShapeDtypeStruct((bsz, seq, C_HEADS * HEAD_DIM), MXU_DTYPE),
        grid=(bsz, nq),
        in_specs=[per_q(512), per_q(512), per_q(128), per_b(nseg, 256), per_b(128, nseg),
                  per_b(seq, 128), per_b(seq // LANES, LANES, LANES),
                  per_b(seq, 128), per_b(seq // LANES, LANES, LANES),
                  const(ov.shape), const((C_HEADS * HEAD_DIM, 1))],
        out_specs=pl.BlockSpec((None, Q_BLOCK, C_HEADS * HEAD_DIM), lambda b, i: (b, i, 0)),
        scratch_shapes=[pltpu.VMEM((C_KV_HEADS, max(n_blk, seq_pad // SEL_BLOCK), Q_BLOCK), F32),
                        pltpu.VMEM((C_KV_HEADS, 1, NQ4), F32),
                        pltpu.VMEM((C_KV_HEADS, HEAD_DIM + ONES_ROWS, NQ4), F32),
                        pltpu.VMEM((C_KV_HEADS, HEAD_DIM, NQ4), F32)],
        compiler_params=_cparams(("arbitrary", "arbitrary")),
        name="nsa_attention",
    )(cq_t, cqr_t, misc_t, cmp, vcmp_t, ks, vs_t, kw, vw_t, ov, gain.reshape(-1, 1))


def _out_mlp_kernel(x_ref, a_ref, b_ref, c_ref, wo_ref, g2_ref, wu_ref, wd_ref, fg_ref,
                    o_ref, x1_scr, n2_scr, acc_scr, *, final_norm):
    j = pl.program_id(1)

    @pl.when(j == 0)
    def _():
        mixed = (jnp.dot(a_ref[...], wo_ref[0:256, :], preferred_element_type=F32)
                 + jnp.dot(b_ref[...], wo_ref[256:512, :], preferred_element_type=F32)
                 + jnp.dot(c_ref[...], wo_ref[512:1024, :], preferred_element_type=F32))
        x1 = x_ref[...] + mixed
        x1_scr[...] = x1
        n2_scr[...] = (_rms(x1, -1) * g2_ref[...]).astype(n2_scr.dtype)
        acc_scr[...] = jnp.zeros_like(acc_scr)

    z = jnp.dot(n2_scr[...], wu_ref[...], preferred_element_type=F32)
    acc_scr[...] += _dot(jnp.square(jnp.maximum(z, 0.0)), wd_ref[...])

    @pl.when(j == pl.num_programs(1) - 1)
    def _():
        x2 = x1_scr[...] + acc_scr[...]
        if final_norm:
            x2 = _rms(x2, -1) * fg_ref[...]
        o_ref[...] = x2


def _out_mlp(x, a_n, b_tm, c_n, w_out, ln2_g, w_up, w_down, final_g, final_norm):
    bsz, seq, _ = x.shape
    tm = min(MLP_TM, seq)
    nj = seq // tm
    nf = D_FF // MLP_TF
    rows = lambda w: pl.BlockSpec((None, tm, w), lambda r, f: (r // nj, r % nj, 0))
    const = lambda shape: pl.BlockSpec(shape, lambda r, f: (0,) * len(shape))
    return pl.pallas_call(
        functools.partial(_out_mlp_kernel, final_norm=final_norm),
        out_shape=jax.ShapeDtypeStruct(x.shape, F32),
        grid=(bsz * nj, nf),
        in_specs=[rows(D_MODEL), rows(256),
                  pl.BlockSpec((tm, B_WIDTH), lambda r, f: (r % nj, r // nj)),
                  rows(512), const((D_MODEL, D_MODEL)), const((1, D_MODEL)),
                  pl.BlockSpec((D_MODEL, MLP_TF), lambda r, f: (0, f)),
                  pl.BlockSpec((MLP_TF, D_MODEL), lambda r, f: (f, 0)),
                  const((1, D_MODEL))],
        out_specs=rows(D_MODEL),
        scratch_shapes=[pltpu.VMEM((tm, D_MODEL), F32), pltpu.VMEM((tm, D_MODEL), MXU_DTYPE),
                        pltpu.VMEM((tm, D_MODEL), F32)],
        compiler_params=_cparams(("arbitrary", "arbitrary")),
        name="out_proj_mlp",
    )(x, a_n, b_tm, c_n, w_out.astype(MXU_DTYPE), ln2_g.reshape(1, -1), w_up.astype(MXU_DTYPE),
      w_down.astype(MXU_DTYPE), final_g.reshape(1, -1))


def kernel(x, positions, ln1_g, w_in, kv_norm_g, w_uk, w_uv, s5_lambda_re, s5_lambda_im, s5_log_step, s5_b_re, s5_b_im, s5_c_re, s5_c_im, s5_d, s5_glu_w, s5_glu_b, cmp_pos_k, cmp_pos_v, cmp_k_w1, cmp_k_w2, cmp_v_w1, cmp_v_w2, gain_a, gain_b, gain_c, w_out, ln2_g, w_up, w_down, final_g):
    bsz, seq, _ = x.shape
    depth = w_in.shape[0]
    src, partner, pat_np = _proj_layout()
    pat = jnp.asarray(pat_np, jnp.bfloat16)
    tab = _rope_table(positions)

    def regather(w, idx):
        cols = jnp.take(w, jnp.asarray(np.maximum(idx, 0)), axis=1)
        return jnp.where(jnp.asarray(idx >= 0)[None, :], cols, 0.0).astype(MXU_DTYPE)

    for layer in range(depth):
        w_r = regather(w_in[layer], src)
        w_sw = regather(w_in[layer], partner)
        (aq_t, akv, ac_t, iq_t, misc, misc_t, u_tm, cq_t, cqr_t, kvc, ks, vs_t, kw, vw_t) = _in_proj(
            x, tab, ln1_g[layer], kv_norm_g[layer], w_r, w_sw, pat)

        ar, ai, bbr, bbi = _s5_discretize(s5_lambda_re[layer], s5_lambda_im[layer], s5_log_step[layer],
                                          s5_b_re[layer], s5_b_im[layer])
        b_tm = _s5(u_tm, bsz, ar, ai, bbr, bbi, s5_c_re[layer], s5_c_im[layer], s5_d[layer].reshape(-1),
                   s5_glu_w[layer], s5_glu_b[layer], gain_b[layer])

        cmp, vcmp_t = _compress(kvc, cmp_pos_k[layer], cmp_pos_v[layer], cmp_k_w1[layer],
                                cmp_k_w2[layer], cmp_v_w1[layer], cmp_v_w2[layer])
        a_n = _dsa(aq_t, iq_t, misc_t, akv, misc, ac_t, w_uk[layer], w_uv[layer], gain_a[layer])
        c_n = _nsa(cq_t, cqr_t, misc_t, cmp, vcmp_t, ks, vs_t, kw, vw_t, gain_c[layer])

        x = _out_mlp(x, a_n, b_tm, c_n, w_out[layer], ln2_g[layer], w_up[layer], w_down[layer],
                     final_g, final_norm=(layer == depth - 1))
    return x
```

```python
import functools
import math

import numpy as np
import jax
import jax.numpy as jnp
from jax import lax
from jax.experimental import pallas as pl
from jax.experimental.pallas import tpu as pltpu

F32 = jnp.float32
I32 = jnp.int32
MXU_DTYPE = jnp.bfloat16

D_MODEL = 1024
HEAD_DIM = 64
ROPE_THETA = 500000.0
ROPE_DIM = HEAD_DIM // 4
NORM_EPS = 1e-6
Q_BLOCK = 128
NEG = -1e30
LOG2E = math.log2(math.e)
D_FF = 4 * D_MODEL

A_HEADS = 4
A_NOPE = HEAD_DIM - ROPE_DIM
A_VDIM = HEAD_DIM
A_KV_RANK = 128
IDX_HEADS = 4
IDX_DIM = 32
IDX_ROPE = IDX_DIM // 4
DSA_TOPK = 256

B_WIDTH = 256
S5_GROUP = 16
S5_GROUPS = B_WIDTH // S5_GROUP
S5_STATE = 64
S5_NSTATE = S5_GROUPS * S5_STATE

C_HEADS = 8
C_KV_HEADS = 2
C_GROUP = C_HEADS // C_KV_HEADS
CMP_BLOCK = 32
CMP_STRIDE = 16
SEL_BLOCK = 64
SEL_TOPN = 16
SEL_FORCE = 1e9
WINDOW = 512

IN_SIZES = (A_HEADS * HEAD_DIM, A_KV_RANK, ROPE_DIM, IDX_HEADS * IDX_DIM, IDX_DIM, IDX_HEADS,
            B_WIDTH, C_HEADS * HEAD_DIM, 6 * C_KV_HEADS * HEAD_DIM, 3 * C_HEADS)

LANES = 128
SUBLANES = 8
VMEM_LIMIT = 56 * 1024 * 1024

SEG_AQ, SEG_AKV, SEG_IQ, SEG_MISC, SEG_BU, SEG_CQ, SEG_KVC, SEG_KS, SEG_VS, SEG_KW, SEG_VW = (
    0, 256, 512, 640, 768, 1024, 1536, 1792, 1920, 2048, 2176)
N_PROJ = 2304
MISC_IK, MISC_IW, MISC_GATE = 0, 32, 36
TAB_COS, TAB_SIN, TAB_ICOS, TAB_ISIN, TAB_ONE = 0, 8, 16, 20, 24
TAB_PART = 32

PROJ_TM = 512
DSA_KEY_CHUNK = 512
NSA_KEY_CHUNK = 1024
S5_TC = 64
MLP_TM = 1024
MLP_TF = 1024


def _cparams(sem):
    return pltpu.CompilerParams(dimension_semantics=sem, vmem_limit_bytes=VMEM_LIMIT)


def _dot(a, b):
    return jnp.dot(a.astype(MXU_DTYPE), b.astype(MXU_DTYPE), preferred_element_type=F32)


def _rms(x, axis):
    return x * lax.rsqrt(jnp.mean(x * x, axis=axis, keepdims=True) + NORM_EPS)


def _rope_tab_kernel(pos_ref, freq_ref, out_ref):
    ang = pos_ref[...].astype(F32) * freq_ref[...]
    lane = lax.broadcasted_iota(I32, ang.shape, 1)
    l32 = lane % TAB_PART
    is_cos = (l32 < TAB_SIN) | ((l32 >= TAB_ICOS) & (l32 < TAB_ISIN))
    is_sin = ((l32 >= TAB_SIN) & (l32 < TAB_ICOS)) | ((l32 >= TAB_ISIN) & (l32 < TAB_ONE))
    val = jnp.where(is_cos, jnp.cos(ang),
                    jnp.where(is_sin, jnp.sin(ang), jnp.where(l32 == TAB_ONE, 1.0, 0.0)))
    hi = val.astype(jnp.bfloat16).astype(F32)
    r1 = val - hi
    mid = r1.astype(jnp.bfloat16).astype(F32)
    lo = r1 - mid
    part = lane // TAB_PART
    out = jnp.where(part == 0, hi, jnp.where(part == 1, mid, jnp.where(part == 2, lo, 0.0)))
    out_ref[...] = out.astype(jnp.bfloat16)


def _rope_table(positions):
    bsz, seq = positions.shape
    t = bsz * seq
    inv_r = (np.float32(ROPE_THETA) ** (-np.arange(0, ROPE_DIM, 2, dtype=np.float32) / ROPE_DIM))
    inv_i = (np.float32(ROPE_THETA) ** (-np.arange(0, IDX_ROPE, 2, dtype=np.float32) / IDX_ROPE))
    f32 = np.zeros(TAB_PART, np.float32)
    f32[TAB_COS:TAB_COS + 8] = inv_r
    f32[TAB_SIN:TAB_SIN + 8] = inv_r
    f32[TAB_ICOS:TAB_ICOS + 4] = inv_i
    f32[TAB_ISIN:TAB_ISIN + 4] = inv_i
    freq = jnp.asarray(np.tile(f32, LANES // TAB_PART)[None, :])
    tq = min(1024, t)
    return pl.pallas_call(
        _rope_tab_kernel,
        out_shape=jax.ShapeDtypeStruct((t, LANES), jnp.bfloat16),
        grid=(t // tq,),
        in_specs=[pl.BlockSpec((tq, 1), lambda i: (i, 0)),
                  pl.BlockSpec((1, LANES), lambda i: (0, 0))],
        out_specs=pl.BlockSpec((tq, LANES), lambda i: (i, 0)),
        compiler_params=_cparams(("arbitrary",)),
        name="rope_table",
    )(positions.reshape(t, 1), freq).reshape(bsz, seq, LANES)


PAT_HEAD, PAT_KROPE, PAT_IDX, PAT_MISC = range(4)
N_PAT = 4
PAT_GEOMETRY = {PAT_HEAD: (ROPE_DIM // 2, HEAD_DIM), PAT_KROPE: (ROPE_DIM // 2, LANES),
                PAT_IDX: (IDX_ROPE // 2, IDX_DIM), PAT_MISC: (IDX_ROPE // 2, LANES)}


def _proj_layout():
    offs = np.concatenate([[0], np.cumsum(IN_SIZES)])
    o_aq, o_ckv, o_kr, o_iq, o_ik, o_iw, o_bu, o_cq, o_kv, o_gate = offs[:10]
    src = -np.ones(N_PROJ, np.int64)
    clane = np.full(N_PROJ, TAB_ONE, np.int64)
    slane = -np.ones(N_PROJ, np.int64)
    ssign = np.zeros(N_PROJ, np.float32)

    def plain(c0, o0, w):
        src[c0:c0 + w] = np.arange(o0, o0 + w)

    def rope(c0, o0, half, cos_lane, sin_lane):
        for j in range(half):
            clane[c0 + j] = clane[c0 + half + j] = cos_lane + j
            slane[c0 + j] = slane[c0 + half + j] = sin_lane + j
            ssign[c0 + j] = -1.0
            ssign[c0 + half + j] = 1.0

    plain(SEG_AQ, o_aq, A_HEADS * HEAD_DIM)
    for h in range(A_HEADS):
        rope(SEG_AQ + h * HEAD_DIM, o_aq + h * HEAD_DIM, ROPE_DIM // 2, TAB_COS, TAB_SIN)
    plain(SEG_AKV, o_ckv, A_KV_RANK)
    plain(SEG_AKV + A_KV_RANK, o_kr, ROPE_DIM)
    rope(SEG_AKV + A_KV_RANK, o_kr, ROPE_DIM // 2, TAB_COS, TAB_SIN)
    plain(SEG_IQ, o_iq, IDX_HEADS * IDX_DIM)
    for h in range(IDX_HEADS):
        rope(SEG_IQ + h * IDX_DIM, o_iq + h * IDX_DIM, IDX_ROPE // 2, TAB_ICOS, TAB_ISIN)
    plain(SEG_MISC + MISC_IK, o_ik, IDX_DIM)
    rope(SEG_MISC + MISC_IK, o_ik, IDX_ROPE // 2, TAB_ICOS, TAB_ISIN)
    plain(SEG_MISC + MISC_IW, o_iw, IDX_HEADS)
    plain(SEG_MISC + MISC_GATE, o_gate, 3 * C_HEADS)
    plain(SEG_BU, o_bu, B_WIDTH)
    plain(SEG_CQ, o_cq, C_HEADS * HEAD_DIM)
    for h in range(C_HEADS):
        rope(SEG_CQ + h * HEAD_DIM, o_cq + h * HEAD_DIM, ROPE_DIM // 2, TAB_COS, TAB_SIN)
    plain(SEG_KVC, o_kv, 6 * C_KV_HEADS * HEAD_DIM)
    for seg, sub in ((SEG_KS, 2), (SEG_KW, 4)):
        for g in range(C_KV_HEADS):
            rope(seg + g * HEAD_DIM, o_kv + sub * C_KV_HEADS * HEAD_DIM + g * HEAD_DIM,
                 ROPE_DIM // 2, TAB_COS, TAB_SIN)

    k = np.arange(LANES)[:, None]
    live = k < 3 * TAB_PART
    ec = (live & ((k % TAB_PART) == clane[None, :])).astype(np.float32)
    es = (live & ((k % TAB_PART) == slane[None, :])).astype(np.float32) * ssign[None, :]
    starts = {PAT_HEAD: SEG_AQ, PAT_KROPE: SEG_AKV + A_KV_RANK, PAT_IDX: SEG_IQ, PAT_MISC: SEG_MISC}
    blocks = [m[:, starts[p]:starts[p] + LANES] for m in (ec, es) for p in range(N_PAT)]
    return src, np.concatenate(blocks, axis=1)


def _in_proj_kernel(x_ref, tab_ref, g_ref, kvg_ref, w_ref, pat_ref,
                    aqT_ref, akv_ref, acT_ref, iqT_ref, misc_ref, miscT_ref, u_ref,
                    cqT_ref, cqrT_ref, kvc_ref, ks_ref, vsT_ref, kw_ref, vwT_ref):
    x = x_ref[...]
    n = (_rms(x, -1) * g_ref[...]).astype(MXU_DTYPE)
    cs = jnp.dot(tab_ref[...], pat_ref[...], preferred_element_type=F32)

    def plain(c0, w):
        return jnp.dot(n, w_ref[:, c0:c0 + w], preferred_element_type=F32)

    def rotate(p, pat):
        w = p.shape[1]
        nb = w // LANES
        half, period = PAT_GEOMETRY[pat]
        lane = lax.broadcasted_iota(I32, (1, w), 1)
        first = (lane % period) < half
        partner = jnp.where(first, pltpu.roll(p, w - half, 1), pltpu.roll(p, half, 1))
        cos = cs[:, pat * LANES:(pat + 1) * LANES]
        sin = cs[:, (N_PAT + pat) * LANES:(N_PAT + pat + 1) * LANES]
        return p * jnp.concatenate([cos] * nb, axis=1) + partner * jnp.concatenate([sin] * nb, axis=1)

    def chunked_t(val, ref):
        vt = val.T.astype(ref.dtype)
        for c in range(ref.shape[0]):
            ref[c] = vt[:, c * LANES:(c + 1) * LANES]

    aqT_ref[...] = rotate(plain(SEG_AQ, 256), PAT_HEAD).T.astype(aqT_ref.dtype)

    akv = plain(SEG_AKV, 256)
    lat = _rms(akv[:, :A_KV_RANK], -1) * kvg_ref[...]
    k_rope = rotate(akv[:, A_KV_RANK:], PAT_KROPE)
    akv_ref[...] = jnp.concatenate([lat, k_rope], axis=1).astype(akv_ref.dtype)
    chunked_t(lat, acT_ref)

    iqT_ref[...] = rotate(plain(SEG_IQ, 128), PAT_IDX).T.astype(iqT_ref.dtype)
    misc = rotate(plain(SEG_MISC, 128), PAT_MISC)
    misc_ref[...] = misc
    miscT_ref[...] = misc.T
    u_ref[...] = plain(SEG_BU, 256)
    qk_scale = HEAD_DIM ** -0.5 * LOG2E
    cq = plain(SEG_CQ, 512)
    cqT_ref[...] = (cq * qk_scale).T.astype(cqT_ref.dtype)
    cqrT_ref[...] = (rotate(cq, PAT_HEAD) * qk_scale).T.astype(cqrT_ref.dtype)
    kvc_ref[...] = plain(SEG_KVC, 256)
    ks_ref[...] = rotate(plain(SEG_KS, 128), PAT_HEAD).astype(ks_ref.dtype)
    chunked_t(plain(SEG_VS, 128), vsT_ref)
    kw_ref[...] = rotate(plain(SEG_KW, 128), PAT_HEAD).astype(kw_ref.dtype)
    chunked_t(plain(SEG_VW, 128), vwT_ref)


def _in_proj(x, tab, ln_g, kv_g, w_r, pat):
    bsz, seq, _ = x.shape
    tm = min(PROJ_TM, seq)
    nj = seq // tm
    nck = tm // LANES
    bf = MXU_DTYPE

    def rows(w):
        return pl.BlockSpec((None, tm, w), lambda b, j: (b, j, 0))

    def cols(w):
        return pl.BlockSpec((None, w, tm), lambda b, j: (b, 0, j))

    def chunks():
        return pl.BlockSpec((None, nck, LANES, LANES), lambda b, j: (b, j, 0, 0))

    def const(shape):
        return pl.BlockSpec(shape, lambda b, j: (0,) * len(shape))

    sds = jax.ShapeDtypeStruct
    out_shape = (
        sds((bsz, 256, seq), bf),
        sds((bsz, seq, 256), bf),
        sds((bsz, seq // LANES, LANES, LANES), bf),
        sds((bsz, 128, seq), bf),
        sds((bsz, seq, 128), F32),
        sds((bsz, 128, seq), F32),
        sds((seq, bsz * B_WIDTH), F32),
        sds((bsz, 512, seq), bf),
        sds((bsz, 512, seq), bf),
        sds((bsz, seq, 256), F32),
        sds((bsz, seq, 128), bf),
        sds((bsz, seq // LANES, LANES, LANES), bf),
        sds((bsz, seq, 128), bf),
        sds((bsz, seq // LANES, LANES, LANES), bf),
    )
    out_specs = (cols(256), rows(256), chunks(), cols(128), rows(128), cols(128),
                 pl.BlockSpec((tm, B_WIDTH), lambda b, j: (j, b)),
                 cols(512), cols(512), rows(256), rows(128), chunks(), rows(128), chunks())
    return pl.pallas_call(
        _in_proj_kernel,
        out_shape=out_shape,
        grid=(bsz, nj),
        in_specs=[rows(D_MODEL), rows(LANES), const((1, D_MODEL)), const((1, A_KV_RANK)),
                  const((D_MODEL, N_PROJ)), const((LANES, 2 * N_PAT * LANES))],
        out_specs=out_specs,
        compiler_params=_cparams(("arbitrary", "arbitrary")),
        name="in_proj",
    )(x, tab, ln_g.reshape(1, -1), kv_g.reshape(1, -1), w_r, pat)


def _s5_disc_kernel(lr_ref, li_ref, ls_ref, br_ref, bi_ref, ar_ref, ai_ref, bbr_ref, bbi_ref):
    lr, li = lr_ref[...], li_ref[...]
    step = jnp.exp(ls_ref[...])
    mag = jnp.exp(lr * step)
    ar = mag * jnp.cos(li * step)
    ai = mag * jnp.sin(li * step)
    den = lr * lr + li * li
    zr = ((ar - 1.0) * lr + ai * li) / den
    zi = (ai * lr - (ar - 1.0) * li) / den
    br, bi = br_ref[...], bi_ref[...]
    ar_ref[...] = ar
    ai_ref[...] = ai
    bbr_ref[...] = zr * br - zi * bi
    bbi_ref[...] = zr * bi + zi * br


def _s5_discretize(lam_re, lam_im, log_step, b_re, b_im):
    g, p, h = b_re.shape
    ex = lambda a: jnp.repeat(a.astype(F32), h, axis=1)
    ls = jnp.broadcast_to(log_step.astype(F32)[:, None], (g, p * h))
    sds = jax.ShapeDtypeStruct((g, p * h), F32)
    ar, ai, bbr, bbi = pl.pallas_call(
        _s5_disc_kernel, out_shape=(sds, sds, sds, sds), name="s5_discretize",
    )(ex(lam_re), ex(lam_im), ls, b_re.astype(F32).reshape(g, p * h), b_im.astype(F32).reshape(g, p * h))
    ar = ar.reshape(g, p, h)[:, :, 0]
    ai = ai.reshape(g, p, h)[:, :, 0]
    return ar, ai, bbr.reshape(g, p, h), bbi.reshape(g, p, h)


def _s5_kernel(u_ref, bmat_ref, a_ref, cre_ref, cim_ref, d_ref, gw_ref, gb_ref, gain_ref,
               o_ref, u_scr, x_scr, h_scr, *, tc):
    ns = S5_NSTATE

    @pl.when(pl.program_id(0) == 0)
    def _():
        h_scr[...] = jnp.zeros_like(h_scr)

    halves = range(B_WIDTH // LANES)
    for b in range(SUBLANES):
        for hf in halves:
            c0 = b * B_WIDTH + hf * LANES
            u_scr[hf, pl.ds(b, tc, stride=SUBLANES), :] = u_ref[:, c0:c0 + LANES]
    u = jnp.concatenate([u_scr[hf] for hf in halves], axis=1)
    x_scr[...] = _dot(u, bmat_ref[...])
    ar = a_ref[0:SUBLANES, :]
    ai = a_ref[SUBLANES:2 * SUBLANES, :]

    def step(t, carry):
        hr, hi = carry
        r0 = pl.multiple_of(t * SUBLANES, SUBLANES)
        xr = x_scr[pl.ds(r0, SUBLANES), 0:ns]
        xi = x_scr[pl.ds(r0, SUBLANES), ns:2 * ns]
        nhr = ar * hr - ai * hi + xr
        nhi = ar * hi + ai * hr + xi
        x_scr[pl.ds(r0, SUBLANES), 0:ns] = nhr
        x_scr[pl.ds(r0, SUBLANES), ns:2 * ns] = nhi
        return nhr, nhi

    hr, hi = lax.fori_loop(0, tc, step, (h_scr[:, 0:ns], h_scr[:, ns:2 * ns]), unroll=8)
    h_scr[:, 0:ns] = hr
    h_scr[:, ns:2 * ns] = hi

    y = (_dot(x_scr[:, 0:ns], cre_ref[...]) - _dot(x_scr[:, ns:2 * ns], cim_ref[...])
         + d_ref[...] * u)
    y = jax.nn.gelu(y)
    y = y * jax.nn.sigmoid(_dot(y, gw_ref[...]) + gb_ref[...])
    y = _rms(y, -1) * gain_ref[...]
    for hf in halves:
        u_scr[hf] = y[:, hf * LANES:(hf + 1) * LANES]
    for b in range(SUBLANES):
        for hf in halves:
            c0 = b * B_WIDTH + hf * LANES
            o_ref[:, c0:c0 + LANES] = u_scr[hf, pl.ds(b, tc, stride=SUBLANES), :].astype(o_ref.dtype)


def _s5(u_tm, bsz, ar, ai, bbr, bbi, c_re, c_im, d, glu_w, glu_b, gain):
    assert bsz == SUBLANES, "the S5 scan keeps one batch row per sublane"
    seq = u_tm.shape[0]
    tc = min(S5_TC, seq)
    rows = tc * bsz
    eye = jnp.eye(S5_GROUPS, dtype=F32)
    bmat = jnp.concatenate(
        [jnp.einsum('gph,gk->ghkp', b, eye).reshape(B_WIDTH, S5_NSTATE) for b in (bbr, bbi)], axis=1)
    cre = jnp.einsum('ghp,gk->gpkh', c_re.astype(F32), eye).reshape(S5_NSTATE, B_WIDTH)
    cim = jnp.einsum('ghp,gk->gpkh', c_im.astype(F32), eye).reshape(S5_NSTATE, B_WIDTH)
    avec = jnp.concatenate([jnp.broadcast_to(a.reshape(1, S5_NSTATE), (SUBLANES, S5_NSTATE))
                            for a in (ar, ai)], axis=0)
    const = lambda shape: pl.BlockSpec(shape, lambda i: (0,) * len(shape))
    out = pl.pallas_call(
        functools.partial(_s5_kernel, tc=tc),
        out_shape=jax.ShapeDtypeStruct((seq, bsz * B_WIDTH), MXU_DTYPE),
        grid=(seq // tc,),
        in_specs=[pl.BlockSpec((tc, bsz * B_WIDTH), lambda i: (i, 0)),
                  const((B_WIDTH, 2 * S5_NSTATE)), const((2 * SUBLANES, S5_NSTATE)),
                  const((S5_NSTATE, B_WIDTH)), const((S5_NSTATE, B_WIDTH)),
                  const((1, B_WIDTH)), const((B_WIDTH, B_WIDTH)), const((1, B_WIDTH)),
                  const((1, B_WIDTH))],
        out_specs=pl.BlockSpec((tc, bsz * B_WIDTH), lambda i: (i, 0)),
        scratch_shapes=[pltpu.VMEM((B_WIDTH // LANES, rows, LANES), F32),
                        pltpu.VMEM((rows, 2 * S5_NSTATE), F32),
                        pltpu.VMEM((SUBLANES, 2 * S5_NSTATE), F32)],
        compiler_params=_cparams(("arbitrary",)),
        name="s5_scan",
    )(u_tm, bmat.astype(MXU_DTYPE), avec,
      cre.astype(MXU_DTYPE), cim.astype(MXU_DTYPE), d.astype(F32).reshape(1, B_WIDTH),
      glu_w.astype(MXU_DTYPE), glu_b.astype(F32).reshape(1, B_WIDTH), gain.reshape(1, B_WIDTH))
    return out


def _compress_kernel(k_ref, v_ref, pa_ref, pb_ref, wa_ref, wb_ref, w2_ref, cmp_ref, vT_ref):
    nseg = cmp_ref.shape[0]
    cw = k_ref.shape[1] + v_ref.shape[1]
    xa = jnp.zeros((nseg, cw), F32)
    xb = jnp.zeros((nseg, cw), F32)
    for l in range(CMP_STRIDE):
        rows = pl.ds(l, nseg, stride=CMP_STRIDE)
        tok = jnp.concatenate([k_ref[rows, :], v_ref[rows, :]], axis=1)
        cols = slice(l * cw, (l + 1) * cw)
        xa = xa + _dot(tok + pa_ref[:, cols], wa_ref[cols, :])
        xb = xb + _dot(tok + pb_ref[:, cols], wb_ref[cols, :])
    pre = xa + pltpu.roll(xb, nseg - 1, 0)
    out = _dot(jax.nn.gelu(pre), w2_ref[...])
    cmp_ref[...] = out.astype(cmp_ref.dtype)
    vT_ref[...] = out[:, LANES:].T.astype(vT_ref.dtype)


def _compress(kvc, pos_k, pos_v, k_w1, k_w2, v_w1, v_w2):
    bsz, seq, _ = kvc.shape
    nseg = seq // CMP_STRIDE
    width = CMP_STRIDE * 256
    eye = jnp.eye(4, dtype=F32)
    w1 = jnp.stack([k_w1, k_w1, v_w1, v_w1]).astype(F32).reshape(4, CMP_BLOCK, HEAD_DIM, HEAD_DIM)
    pos = jnp.stack([pos_k, pos_k, pos_v, pos_v]).astype(F32)

    def half(lo):
        w = jnp.einsum('slde,st->lsdte', w1[:, lo:lo + CMP_STRIDE], eye).reshape(width, 256)
        p = jnp.transpose(pos[:, lo:lo + CMP_STRIDE], (1, 0, 2)).reshape(1, width)
        return w.astype(MXU_DTYPE), p

    wa, pa = half(0)
    wb, pb = half(CMP_STRIDE)
    w2 = jnp.einsum('sde,st->sdte', jnp.stack([k_w2, k_w2, v_w2, v_w2]).astype(F32), eye).reshape(256, 256)
    const = lambda shape: pl.BlockSpec(shape, lambda b: (0,) * len(shape))
    return pl.pallas_call(
        _compress_kernel,
        out_shape=(jax.ShapeDtypeStruct((bsz, nseg, 256), MXU_DTYPE),
                   jax.ShapeDtypeStruct((bsz, 128, nseg), MXU_DTYPE)),
        grid=(bsz,),
        in_specs=[pl.BlockSpec((None, seq, LANES), lambda b: (b, 0, 0)),
                  pl.BlockSpec((None, seq, LANES), lambda b: (b, 0, 1)),
                  const((1, width)), const((1, width)), const((width, 256)), const((width, 256)),
                  const((256, 256))],
        out_specs=(pl.BlockSpec((None, nseg, 256), lambda b: (b, 0, 0)),
                   pl.BlockSpec((None, 128, nseg), lambda b: (b, 0, 0))),
        compiler_params=_cparams(("arbitrary",)),
        name="nsa_compress",
    )(kvc, kvc, pa, pb, wa, wb, w2.astype(MXU_DTYPE))


ONES_ROWS = 16
NQ4 = 4 * Q_BLOCK


def _with_ones(v_t):
    return jnp.concatenate([v_t, jnp.ones((ONES_ROWS, v_t.shape[1]), v_t.dtype)], axis=0)


def _online_softmax_step(s, bias, v_ext, m_ref, acc_ref):
    s = s + bias
    m_old = m_ref[...]
    m_new = jnp.maximum(m_old, jnp.max(s, axis=0, keepdims=True))
    p = jnp.exp2(s - m_new)
    acc_ref[...] = jnp.exp2(m_old - m_new) * acc_ref[...] + _dot(v_ext, p)
    m_ref[...] = m_new


def _tile4(x):
    return jnp.concatenate([x, x, x, x], axis=1)


INT_MIN = -2 ** 31
NEG_INF_KEY = int(np.int32(np.uint32(0xFF800000 ^ 0x7FFFFFFF)))


def _dsa_kernel(qT_ref, iqT_ref, miscT_ref, kv_ref, misc_ref, cT_ref, mq_ref, wuv_ref, gain_ref, tri_ref,
                o_ref, key_scr, m_scr, acc_scr, *, k_top, max_chunks):
    i = pl.program_id(1)
    t0 = i * Q_BLOCK
    kc = DSA_KEY_CHUNK
    nch = (t0 + Q_BLOCK + kc - 1) // kc
    t_lane = t0 + lax.broadcasted_iota(I32, (1, Q_BLOCK), 1)
    row_iota = lax.broadcasted_iota(I32, (kc, Q_BLOCK), 0)

    qcat = (_dot(mq_ref[...], qT_ref[...]) * (HEAD_DIM ** -0.5 * LOG2E)).astype(MXU_DTYPE)
    qs_t = jnp.concatenate([qcat[h * 256:(h + 1) * 256] for h in range(A_HEADS)], axis=1)
    iq_t = iqT_ref[...]
    iq_all = jnp.concatenate([iq_t[h * IDX_DIM:(h + 1) * IDX_DIM, :] for h in range(IDX_HEADS)], axis=1)
    w_t = miscT_ref[MISC_IW:MISC_IW + IDX_HEADS, :] * (IDX_HEADS ** -0.5 * IDX_DIM ** -0.5)

    def idx_body(c, diagonal):
        k0 = c * kc
        ik = misc_ref[pl.ds(k0, kc), MISC_IK:MISC_IK + IDX_DIM]
        d = _dot(ik, iq_all)
        score = jnp.zeros((kc, Q_BLOCK), F32)
        for h in range(IDX_HEADS):
            score = score + jnp.maximum(d[:, h * Q_BLOCK:(h + 1) * Q_BLOCK], 0.0) * w_t[h:h + 1, :]
        if diagonal:
            score = jnp.where(row_iota + k0 <= t_lane, score, -jnp.inf)
        bits = pltpu.bitcast(score, I32)
        key_scr[pl.ds(k0, kc), :] = bits ^ ((bits >> 31) & 0x7FFFFFFF)

    m_scr[...] = jnp.full_like(m_scr, NEG)
    acc_scr[...] = jnp.zeros_like(acc_scr)

    def passes(n_chunks):
        for c in range(n_chunks):
            idx_body(c, c == n_chunks - 1)

        def count(pred):
            acc = jnp.zeros((SUBLANES, Q_BLOCK), I32)
            for c in range(n_chunks):
                hit = pred(key_scr[c * kc:(c + 1) * kc, :])
                acc = acc + jnp.sum(hit.reshape(kc // SUBLANES, SUBLANES, Q_BLOCK), axis=0)
            return jnp.sum(acc, axis=0, keepdims=True)

        def thr_bit(b, prefix):
            cand = prefix | lax.shift_left(jnp.int32(1), 31 - b)
            cand_s = cand ^ INT_MIN
            cnt = count(lambda keys: jnp.where(keys >= cand_s, 1, 0))
            return jnp.where(cnt >= k_top, cand, prefix)

        thr = lax.fori_loop(0, 32, thr_bit, jnp.zeros((1, Q_BLOCK), I32)) ^ INT_MIN
        n_gt = count(lambda keys: jnp.where(keys > thr, 1, 0))
        need = jnp.where(thr == NEG_INF_KEY, 0, k_top - n_gt).astype(F32)

        n_tied = jnp.zeros((1, Q_BLOCK), F32)
        for c in range(n_chunks):
            keys = key_scr[c * kc:(c + 1) * kc, :]
            tied = _dot(tri_ref[...], jnp.where(keys == thr, 1.0, 0.0)) + n_tied
            bias = jnp.where(keys > thr, 0.0,
                             jnp.where(keys == thr, jnp.where(tied <= need, 0.0, NEG), NEG))
            v_t = jnp.concatenate([cT_ref[c * (kc // LANES) + j] for j in range(kc // LANES)], axis=1)
            s = _dot(kv_ref[c * kc:(c + 1) * kc, :], qs_t)
            _online_softmax_step(s, _tile4(bias), _with_ones(v_t), m_scr, acc_scr)
            n_tied = tied[kc - 1:kc, :]
        return 0

    lax.switch(nch - 1, [functools.partial(passes, n) for n in range(1, max_chunks + 1)])

    acc = acc_scr[...]
    o_lat = (acc[:A_KV_RANK] / acc[A_KV_RANK:A_KV_RANK + 1]).astype(MXU_DTYPE)
    out_t = jnp.concatenate(
        [_dot(wuv_ref[h], o_lat[:, h * Q_BLOCK:(h + 1) * Q_BLOCK]) for h in range(A_HEADS)], axis=0)
    out_t = _rms(out_t, 0) * gain_ref[...]
    o_ref[...] = out_t.T.astype(o_ref.dtype)


def _dsa(aq_t, iq_t, misc_t, akv, misc, ac_t, w_uk, w_uv, gain):
    bsz, _, seq = aq_t.shape
    nq = seq // Q_BLOCK
    k_top = min(DSA_TOPK, seq // 4)
    seq_pad = -(-seq // DSA_KEY_CHUNK) * DSA_KEY_CHUNK
    tri = jnp.asarray(np.tril(np.ones((DSA_KEY_CHUNK, DSA_KEY_CHUNK), np.float32)), MXU_DTYPE)
    mq = jnp.zeros((A_HEADS, 256, A_HEADS, HEAD_DIM), F32)
    for h in range(A_HEADS):
        mq = mq.at[h, :A_KV_RANK, h, ROPE_DIM:].set(w_uk[:, h, :].astype(F32))
        mq = mq.at[h, A_KV_RANK:A_KV_RANK + ROPE_DIM, h, :ROPE_DIM].set(jnp.eye(ROPE_DIM, dtype=F32))
    mq = mq.reshape(A_HEADS * 256, A_HEADS * HEAD_DIM).astype(MXU_DTYPE)
    wuv_t = jnp.transpose(w_uv, (1, 2, 0)).astype(MXU_DTYPE)
    per_q = lambda w: pl.BlockSpec((None, w, Q_BLOCK), lambda b, i: (b, 0, i))
    per_b = lambda *s: pl.BlockSpec((None,) + s, lambda b, i: (b,) + (0,) * len(s))
    const = lambda shape: pl.BlockSpec(shape, lambda b, i: (0,) * len(shape))
    return pl.pallas_call(
        functools.partial(_dsa_kernel, k_top=k_top, max_chunks=seq_pad // DSA_KEY_CHUNK),
        out_shape=jax.ShapeDtypeStruct((bsz, seq, A_HEADS * A_VDIM), MXU_DTYPE),
        grid=(bsz, nq),
        in_specs=[per_q(256), per_q(128), per_q(128), per_b(seq, 256), per_b(seq, 128),
                  per_b(seq // LANES, LANES, LANES), const(mq.shape), const(wuv_t.shape),
                  const((A_HEADS * A_VDIM, 1)), const(tri.shape)],
        out_specs=pl.BlockSpec((None, Q_BLOCK, A_HEADS * A_VDIM), lambda b, i: (b, i, 0)),
        scratch_shapes=[pltpu.VMEM((seq_pad, Q_BLOCK), I32),
                        pltpu.VMEM((1, NQ4), F32),
                        pltpu.VMEM((A_KV_RANK + ONES_ROWS, NQ4), F32)],
        compiler_params=_cparams(("arbitrary", "arbitrary")),
        name="dsa_attention",
    )(aq_t, iq_t, misc_t, akv, misc, ac_t, mq, wuv_t, gain.reshape(-1, 1), tri)


def _nsa_kernel(qT_ref, qrT_ref, miscT_ref, cmp_ref, vcT_ref, ks_ref, vsT_ref, kw_ref, vwT_ref,
                ov_ref, gain_ref, o_ref, sel_scr, m_scr, acc_scr, win_scr, *, n_top, n_blk, max_chunks):
    i = pl.program_id(1)
    t0 = i * Q_BLOCK
    kc = NSA_KEY_CHUNK
    nch = (t0 + Q_BLOCK + kc - 1) // kc
    t_lane = t0 + lax.broadcasted_iota(I32, (1, Q_BLOCK), 1)
    gates = jax.nn.sigmoid(miscT_ref[MISC_GATE:MISC_GATE + 3 * C_HEADS, :])
    n_cmp = cmp_ref.shape[0]
    cmp_iota = lax.broadcasted_iota(I32, (n_cmp, Q_BLOCK), 0)
    blk_iota = lax.broadcasted_iota(I32, (n_blk, Q_BLOCK), 0)
    row_iota = lax.broadcasted_iota(I32, (kc, Q_BLOCK), 0)
    win_iota = lax.broadcasted_iota(I32, (Q_BLOCK, Q_BLOCK), 0)
    groups = range(C_KV_HEADS)
    gsl = [slice(g * HEAD_DIM, (g + 1) * HEAD_DIM) for g in groups]

    def heads_t(ref, g):
        return jnp.concatenate([ref[h * HEAD_DIM:(h + 1) * HEAD_DIM, :]
                                for h in range(g * C_GROUP, (g + 1) * C_GROUP)], axis=1)

    o_c = []
    for g in groups:
        s_c = _dot(cmp_ref[:, gsl[g]], heads_t(qT_ref, g))
        vis = _tile4(jnp.where(cmp_iota * CMP_STRIDE + (CMP_BLOCK - 1) <= t_lane, 1, 0)) > 0
        s_c = jnp.where(vis, s_c, NEG)
        p_c = jnp.where(vis, jnp.exp2(s_c - jnp.max(s_c, axis=0, keepdims=True)), 0.0)
        l_c = jnp.sum(p_c, axis=0, keepdims=True)
        p_c = p_c * (1.0 / jnp.maximum(l_c, 1e-30))
        o_c.append(_dot(vcT_ref[gsl[g], :], p_c))

        p_sum = (p_c[:, 0:Q_BLOCK] + p_c[:, Q_BLOCK:2 * Q_BLOCK]
                 + p_c[:, 2 * Q_BLOCK:3 * Q_BLOCK] + p_c[:, 3 * Q_BLOCK:4 * Q_BLOCK])
        p_hi = p_sum.astype(MXU_DTYPE)
        p_lo = p_sum - p_hi.astype(F32)
        imp = _dot(ov_ref[...], p_hi) + _dot(ov_ref[...], p_lo)
        cur = t_lane // SEL_BLOCK
        forced = jnp.where(blk_iota == 0, 1, jnp.where(blk_iota == cur, 1,
                           jnp.where(blk_iota == cur - 1, 1, 0)))
        imp = jnp.where(forced > 0, SEL_FORCE, imp)
        imp = jnp.where(blk_iota * SEL_BLOCK <= t_lane, imp, -jnp.inf)
        n_grp = n_blk // SUBLANES
        imp_g = [imp[v * SUBLANES:(v + 1) * SUBLANES, :] for v in range(n_grp)]
        rank_g = [jnp.zeros((SUBLANES, Q_BLOCK), I32) for _ in range(n_grp)]
        for mp in range(n_blk):
            row = imp[mp:mp + 1, :]
            for v in range(n_grp):
                if v > mp // SUBLANES:
                    beats = jnp.where(row >= imp_g[v], 1, 0)
                elif v < mp // SUBLANES:
                    beats = jnp.where(row > imp_g[v], 1, 0)
                else:
                    later = blk_iota[v * SUBLANES:(v + 1) * SUBLANES, :] > mp
                    beats = jnp.where(row > imp_g[v], 1, jnp.where(row == imp_g[v], jnp.where(later, 1, 0), 0))
                rank_g[v] = rank_g[v] + beats
        rank = jnp.concatenate(rank_g, axis=0)
        sel_scr[g, 0:n_blk, :] = jnp.where(rank < n_top, 0.0, NEG)

    m_scr[...] = jnp.full_like(m_scr, NEG)
    acc_scr[...] = jnp.zeros_like(acc_scr)
    bpc = kc // SEL_BLOCK

    def sel_body(c, diagonal):
        k0 = c * kc
        for g in groups:
            s = _dot(ks_ref[pl.ds(k0, kc), gsl[g]], heads_t(qrT_ref, g))
            sel8 = sel_scr[g, pl.ds(c * bpc, bpc), :]
            bias = jnp.concatenate(
                [jnp.broadcast_to(sel8[j:j + 1, :], (SEL_BLOCK, Q_BLOCK)) for j in range(bpc)], axis=0)
            if diagonal:
                bias = jnp.where(row_iota + k0 <= t_lane, bias, NEG)
            v_t = jnp.concatenate([vsT_ref[c * (kc // LANES) + j][gsl[g], :]
                                   for j in range(kc // LANES)], axis=1)
            _online_softmax_step(s, _tile4(bias), _with_ones(v_t), m_scr.at[g], acc_scr.at[g])

    def window_branch(g):
        k_parts, v_parts, m_parts = [], [], []
        for j in range(WINDOW // Q_BLOCK + 1):
            cj = i - WINDOW // Q_BLOCK + j
            cjc = jnp.maximum(cj, 0)
            k0 = pl.multiple_of(cjc * Q_BLOCK, Q_BLOCK)
            k_parts.append(kw_ref[pl.ds(k0, Q_BLOCK), gsl[g]])
            v_parts.append(vwT_ref[cjc][gsl[g], :])
            kidx = win_iota + k0
            inside = jnp.where(kidx <= t_lane, jnp.where(kidx > t_lane - WINDOW, 0.0, NEG), NEG)
            m_parts.append(jnp.where(cj >= 0, inside, NEG))
        s_w = (_dot(jnp.concatenate(k_parts, axis=0), heads_t(qrT_ref, g))
               + _tile4(jnp.concatenate(m_parts, axis=0)))
        p_w = jnp.exp2(s_w - jnp.max(s_w, axis=0, keepdims=True))
        acc = _dot(_with_ones(jnp.concatenate(v_parts, axis=1)), p_w)
        win_scr[g] = acc[:HEAD_DIM] / acc[HEAD_DIM:HEAD_DIM + 1]

    def sweep(n_chunks):
        for g in groups:
            window_branch(g)
        for c in range(n_chunks):
            sel_body(c, c == n_chunks - 1)
        return 0

    lax.switch(nch - 1, [functools.partial(sweep, n) for n in range(1, max_chunks + 1)])

    slabs = []
    for g in groups:
        acc = acc_scr[g]
        o_s = acc[:HEAD_DIM] / acc[HEAD_DIM:HEAD_DIM + 1]
        o_w = win_scr[g]
        for hh in range(C_GROUP):
            sl = slice(hh * Q_BLOCK, (hh + 1) * Q_BLOCK)
            r = (g * C_GROUP + hh) * 3
            slabs.append(gates[r:r + 1, :] * o_c[g][:, sl] + gates[r + 1:r + 2, :] * o_s[:, sl]
                         + gates[r + 2:r + 3, :] * o_w[:, sl])

    out_t = jnp.concatenate(slabs, axis=0)
    out_t = _rms(out_t, 0) * gain_ref[...]
    o_ref[...] = out_t.T.astype(o_ref.dtype)


def _sel_overlap_t(n_cmp_rows, n_blk):
    cs = np.arange(n_cmp_rows)[None, :] * CMP_STRIDE
    ss = np.arange(n_blk)[:, None] * SEL_BLOCK
    ov = np.minimum(cs + CMP_BLOCK, ss + SEL_BLOCK) - np.maximum(cs, ss)
    return np.clip(ov, 0, None).astype(np.float32) / CMP_BLOCK


def _nsa(cq_t, cqr_t, misc_t, cmp, vcmp_t, ks, vs_t, kw, vw_t, gain):
    bsz, _, seq = cq_t.shape
    nq = seq // Q_BLOCK
    nseg = cmp.shape[1]
    n_blk = seq // SEL_BLOCK
    n_top = min(SEL_TOPN, n_blk)
    seq_pad = -(-seq // NSA_KEY_CHUNK) * NSA_KEY_CHUNK
    ov_np = _sel_overlap_t(nseg, n_blk)
    ov_np[:, (seq - CMP_BLOCK) // CMP_STRIDE + 1:] = 0.0
    ov = jnp.asarray(ov_np, MXU_DTYPE)
    per_q = lambda w: pl.BlockSpec((None, w, Q_BLOCK), lambda b, i: (b, 0, i))
    per_b = lambda *s: pl.BlockSpec((None,) + s, lambda b, i: (b,) + (0,) * len(s))
    const = lambda shape: pl.BlockSpec(shape, lambda b, i: (0,) * len(shape))
    return pl.pallas_call(
        functools.partial(_nsa_kernel, n_top=n_top, n_blk=n_blk, max_chunks=seq_pad // NSA_KEY_CHUNK),
        out_shape=jax.ShapeDtypeStruct((bsz, seq, C_HEADS * HEAD_DIM), MXU_DTYPE),
        grid=(bsz, nq),
        in_specs=[per_q(512), per_q(512), per_q(128), per_b(nseg, 256), per_b(128, nseg),
                  per_b(seq, 128), per_b(seq // LANES, LANES, LANES),
                  per_b(seq, 128), per_b(seq // LANES, LANES, LANES),
                  const(ov.shape), const((C_HEADS * HEAD_DIM, 1))],
        out_specs=pl.BlockSpec((None, Q_BLOCK, C_HEADS * HEAD_DIM), lambda b, i: (b, i, 0)),
        scratch_shapes=[pltpu.VMEM((C_KV_HEADS, max(n_blk, seq_pad // SEL_BLOCK), Q_BLOCK), F32),
                        pltpu.VMEM((C_KV_HEADS, 1, NQ4), F32),
                        pltpu.VMEM((C_KV_HEADS, HEAD_DIM + ONES_ROWS, NQ4), F32),
                        pltpu.VMEM((C_KV_HEADS, HEAD_DIM, NQ4), F32)],
        compiler_params=_cparams(("arbitrary", "arbitrary")),
        name="nsa_attention",
    )(cq_t, cqr_t, misc_t, cmp, vcmp_t, ks, vs_t, kw, vw_t, ov, gain.reshape(-1, 1))


def _out_mlp_kernel(x_ref, a_ref, b_ref, c_ref, wo_ref, g2_ref, wu_ref, wd_ref, fg_ref,
                    o_ref, x1_scr, n2_scr, acc_scr, *, final_norm):
    j = pl.program_id(1)

    @pl.when(j == 0)
    def _():
        mixed = (jnp.dot(a_ref[...], wo_ref[0:256, :], preferred_element_type=F32)
                 + jnp.dot(b_ref[...], wo_ref[256:512, :], preferred_element_type=F32)
                 + jnp.dot(c_ref[...], wo_ref[512:1024, :], preferred_element_type=F32))
        x1 = x_ref[...] + mixed
        x1_scr[...] = x1
        n2_scr[...] = (_rms(x1, -1) * g2_ref[...]).astype(n2_scr.dtype)
        acc_scr[...] = jnp.zeros_like(acc_scr)

    z = jnp.dot(n2_scr[...], wu_ref[...], preferred_element_type=F32)
    acc_scr[...] += _dot(jnp.square(jnp.maximum(z, 0.0)), wd_ref[...])

    @pl.when(j == pl.num_programs(1) - 1)
    def _():
        x2 = x1_scr[...] + acc_scr[...]
        if final_norm:
            x2 = _rms(x2, -1) * fg_ref[...]
        o_ref[...] = x2


def _out_mlp(x, a_n, b_tm, c_n, w_out, ln2_g, w_up, w_down, final_g, final_norm):
    bsz, seq, _ = x.shape
    tm = min(MLP_TM, seq)
    nj = seq // tm
    nf = D_FF // MLP_TF
    rows = lambda w: pl.BlockSpec((None, tm, w), lambda r, f: (r // nj, r % nj, 0))
    const = lambda shape: pl.BlockSpec(shape, lambda r, f: (0,) * len(shape))
    return pl.pallas_call(
        functools.partial(_out_mlp_kernel, final_norm=final_norm),
        out_shape=jax.ShapeDtypeStruct(x.shape, F32),
        grid=(bsz * nj, nf),
        in_specs=[rows(D_MODEL), rows(256),
                  pl.BlockSpec((tm, B_WIDTH), lambda r, f: (r % nj, r // nj)),
                  rows(512), const((D_MODEL, D_MODEL)), const((1, D_MODEL)),
                  pl.BlockSpec((D_MODEL, MLP_TF), lambda r, f: (0, f)),
                  pl.BlockSpec((MLP_TF, D_MODEL), lambda r, f: (f, 0)),
                  const((1, D_MODEL))],
        out_specs=rows(D_MODEL),
        scratch_shapes=[pltpu.VMEM((tm, D_MODEL), F32), pltpu.VMEM((tm, D_MODEL), MXU_DTYPE),
                        pltpu.VMEM((tm, D_MODEL), F32)],
        compiler_params=_cparams(("arbitrary", "arbitrary")),
        name="out_proj_mlp",
    )(x, a_n, b_tm, c_n, w_out.astype(MXU_DTYPE), ln2_g.reshape(1, -1), w_up.astype(MXU_DTYPE),
      w_down.astype(MXU_DTYPE), final_g.reshape(1, -1))


def kernel(x, positions, ln1_g, w_in, kv_norm_g, w_uk, w_uv, s5_lambda_re, s5_lambda_im, s5_log_step, s5_b_re, s5_b_im, s5_c_re, s5_c_im, s5_d, s5_glu_w, s5_glu_b, cmp_pos_k, cmp_pos_v, cmp_k_w1, cmp_k_w2, cmp_v_w1, cmp_v_w2, gain_a, gain_b, gain_c, w_out, ln2_g, w_up, w_down, final_g):
    bsz, seq, _ = x.shape
    depth = w_in.shape[0]
    src, pat_np = _proj_layout()
    pat = jnp.asarray(pat_np, jnp.bfloat16)
    tab = _rope_table(positions)

    def regather(w, idx):
        cols = jnp.take(w, jnp.asarray(np.maximum(idx, 0)), axis=1)
        return jnp.where(jnp.asarray(idx >= 0)[None, :], cols, 0.0).astype(MXU_DTYPE)

    for layer in range(depth):
        w_r = regather(w_in[layer], src)
        (aq_t, akv, ac_t, iq_t, misc, misc_t, u_tm, cq_t, cqr_t, kvc, ks, vs_t, kw, vw_t) = _in_proj(
            x, tab, ln1_g[layer], kv_norm_g[layer], w_r, pat)

        ar, ai, bbr, bbi = _s5_discretize(s5_lambda_re[layer], s5_lambda_im[layer], s5_log_step[layer],
                                          s5_b_re[layer], s5_b_im[layer])
        b_tm = _s5(u_tm, bsz, ar, ai, bbr, bbi, s5_c_re[layer], s5_c_im[layer], s5_d[layer].reshape(-1),
                   s5_glu_w[layer], s5_glu_b[layer], gain_b[layer])

        cmp, vcmp_t = _compress(kvc, cmp_pos_k[layer], cmp_pos_v[layer], cmp_k_w1[layer],
                                cmp_k_w2[layer], cmp_v_w1[layer], cmp_v_w2[layer])
        a_n = _dsa(aq_t, iq_t, misc_t, akv, misc, ac_t, w_uk[layer], w_uv[layer], gain_a[layer])
        c_n = _nsa(cq_t, cqr_t, misc_t, cmp, vcmp_t, ks, vs_t, kw, vw_t, gain_c[layer])

        x = _out_mlp(x, a_n, b_tm, c_n, w_out[layer], ln2_g[layer], w_up[layer], w_down[layer],
                     final_g, final_norm=(layer == depth - 1))
    return x
```

```python
import functools
import math

import numpy as np
import jax
import jax.numpy as jnp
from jax import lax
from jax.experimental import pallas as pl
from jax.experimental.pallas import tpu as pltpu

F32 = jnp.float32
I32 = jnp.int32
MXU_DTYPE = jnp.bfloat16

D_MODEL = 1024
HEAD_DIM = 64
ROPE_THETA = 500000.0
ROPE_DIM = HEAD_DIM // 4
NORM_EPS = 1e-6
Q_BLOCK = 128
NEG = -1e30
LOG2E = math.log2(math.e)
D_FF = 4 * D_MODEL

A_HEADS = 4
A_NOPE = HEAD_DIM - ROPE_DIM
A_VDIM = HEAD_DIM
A_KV_RANK = 128
IDX_HEADS = 4
IDX_DIM = 32
IDX_ROPE = IDX_DIM // 4
DSA_TOPK = 256

B_WIDTH = 256
S5_GROUP = 16
S5_GROUPS = B_WIDTH // S5_GROUP
S5_STATE = 64
S5_NSTATE = S5_GROUPS * S5_STATE

C_HEADS = 8
C_KV_HEADS = 2
C_GROUP = C_HEADS // C_KV_HEADS
CMP_BLOCK = 32
CMP_STRIDE = 16
SEL_BLOCK = 64
SEL_TOPN = 16
SEL_FORCE = 1e9
WINDOW = 512

IN_SIZES = (A_HEADS * HEAD_DIM, A_KV_RANK, ROPE_DIM, IDX_HEADS * IDX_DIM, IDX_DIM, IDX_HEADS,
            B_WIDTH, C_HEADS * HEAD_DIM, 6 * C_KV_HEADS * HEAD_DIM, 3 * C_HEADS)

LANES = 128
SUBLANES = 8
VMEM_LIMIT = 56 * 1024 * 1024

SEG_AQ, SEG_AKV, SEG_IQ, SEG_MISC, SEG_BU, SEG_CQ, SEG_KVC, SEG_KS, SEG_VS, SEG_KW, SEG_VW = (
    0, 256, 512, 640, 768, 1024, 1536, 1792, 1920, 2048, 2176)
N_PROJ = 2304
MISC_IK, MISC_IW, MISC_GATE = 0, 32, 36
TAB_COS, TAB_SIN, TAB_ICOS, TAB_ISIN, TAB_ONE = 0, 8, 16, 20, 24
TAB_PART = 32

PROJ_TM = 512
DSA_KEY_CHUNK = 512
NSA_KEY_CHUNK = 1024
S5_TC = 128
MLP_TM = 1024
MLP_TF = 1024


def _cparams(sem):
    return pltpu.CompilerParams(dimension_semantics=sem, vmem_limit_bytes=VMEM_LIMIT)


def _dot(a, b):
    return jnp.dot(a.astype(MXU_DTYPE), b.astype(MXU_DTYPE), preferred_element_type=F32)


def _rms(x, axis):
    return x * lax.rsqrt(jnp.mean(x * x, axis=axis, keepdims=True) + NORM_EPS)


def _rope_tab_kernel(pos_ref, freq_ref, out_ref):
    ang = pos_ref[...].astype(F32) * freq_ref[...]
    lane = lax.broadcasted_iota(I32, ang.shape, 1)
    l32 = lane % TAB_PART
    is_cos = (l32 < TAB_SIN) | ((l32 >= TAB_ICOS) & (l32 < TAB_ISIN))
    is_sin = ((l32 >= TAB_SIN) & (l32 < TAB_ICOS)) | ((l32 >= TAB_ISIN) & (l32 < TAB_ONE))
    val = jnp.where(is_cos, jnp.cos(ang),
                    jnp.where(is_sin, jnp.sin(ang), jnp.where(l32 == TAB_ONE, 1.0, 0.0)))
    hi = val.astype(jnp.bfloat16).astype(F32)
    r1 = val - hi
    mid = r1.astype(jnp.bfloat16).astype(F32)
    lo = r1 - mid
    part = lane // TAB_PART
    out = jnp.where(part == 0, hi, jnp.where(part == 1, mid, jnp.where(part == 2, lo, 0.0)))
    out_ref[...] = out.astype(jnp.bfloat16)


def _rope_table(positions):
    bsz, seq = positions.shape
    t = bsz * seq
    inv_r = (np.float32(ROPE_THETA) ** (-np.arange(0, ROPE_DIM, 2, dtype=np.float32) / ROPE_DIM))
    inv_i = (np.float32(ROPE_THETA) ** (-np.arange(0, IDX_ROPE, 2, dtype=np.float32) / IDX_ROPE))
    f32 = np.zeros(TAB_PART, np.float32)
    f32[TAB_COS:TAB_COS + 8] = inv_r
    f32[TAB_SIN:TAB_SIN + 8] = inv_r
    f32[TAB_ICOS:TAB_ICOS + 4] = inv_i
    f32[TAB_ISIN:TAB_ISIN + 4] = inv_i
    freq = jnp.asarray(np.tile(f32, LANES // TAB_PART)[None, :])
    tq = min(1024, t)
    return pl.pallas_call(
        _rope_tab_kernel,
        out_shape=jax.ShapeDtypeStruct((t, LANES), jnp.bfloat16),
        grid=(t // tq,),
        in_specs=[pl.BlockSpec((tq, 1), lambda i: (i, 0)),
                  pl.BlockSpec((1, LANES), lambda i: (0, 0))],
        out_specs=pl.BlockSpec((tq, LANES), lambda i: (i, 0)),
        compiler_params=_cparams(("arbitrary",)),
        name="rope_table",
    )(positions.reshape(t, 1), freq).reshape(bsz, seq, LANES)


PAT_HEAD, PAT_KROPE, PAT_IDX, PAT_MISC = range(4)
N_PAT = 4
PAT_GEOMETRY = {PAT_HEAD: (ROPE_DIM // 2, HEAD_DIM), PAT_KROPE: (ROPE_DIM // 2, LANES),
                PAT_IDX: (IDX_ROPE // 2, IDX_DIM), PAT_MISC: (IDX_ROPE // 2, LANES)}


def _proj_layout():
    offs = np.concatenate([[0], np.cumsum(IN_SIZES)])
    o_aq, o_ckv, o_kr, o_iq, o_ik, o_iw, o_bu, o_cq, o_kv, o_gate = offs[:10]
    src = -np.ones(N_PROJ, np.int64)
    clane = np.full(N_PROJ, TAB_ONE, np.int64)
    slane = -np.ones(N_PROJ, np.int64)
    ssign = np.zeros(N_PROJ, np.float32)

    def plain(c0, o0, w):
        src[c0:c0 + w] = np.arange(o0, o0 + w)

    def rope(c0, o0, half, cos_lane, sin_lane):
        for j in range(half):
            clane[c0 + j] = clane[c0 + half + j] = cos_lane + j
            slane[c0 + j] = slane[c0 + half + j] = sin_lane + j
            ssign[c0 + j] = -1.0
            ssign[c0 + half + j] = 1.0

    plain(SEG_AQ, o_aq, A_HEADS * HEAD_DIM)
    for h in range(A_HEADS):
        rope(SEG_AQ + h * HEAD_DIM, o_aq + h * HEAD_DIM, ROPE_DIM // 2, TAB_COS, TAB_SIN)
    plain(SEG_AKV, o_ckv, A_KV_RANK)
    plain(SEG_AKV + A_KV_RANK, o_kr, ROPE_DIM)
    rope(SEG_AKV + A_KV_RANK, o_kr, ROPE_DIM // 2, TAB_COS, TAB_SIN)
    plain(SEG_IQ, o_iq, IDX_HEADS * IDX_DIM)
    for h in range(IDX_HEADS):
        rope(SEG_IQ + h * IDX_DIM, o_iq + h * IDX_DIM, IDX_ROPE // 2, TAB_ICOS, TAB_ISIN)
    plain(SEG_MISC + MISC_IK, o_ik, IDX_DIM)
    rope(SEG_MISC + MISC_IK, o_ik, IDX_ROPE // 2, TAB_ICOS, TAB_ISIN)
    plain(SEG_MISC + MISC_IW, o_iw, IDX_HEADS)
    plain(SEG_MISC + MISC_GATE, o_gate, 3 * C_HEADS)
    plain(SEG_BU, o_bu, B_WIDTH)
    plain(SEG_CQ, o_cq, C_HEADS * HEAD_DIM)
    for h in range(C_HEADS):
        rope(SEG_CQ + h * HEAD_DIM, o_cq + h * HEAD_DIM, ROPE_DIM // 2, TAB_COS, TAB_SIN)
    plain(SEG_KVC, o_kv, 6 * C_KV_HEADS * HEAD_DIM)
    for seg, sub in ((SEG_KS, 2), (SEG_KW, 4)):
        for g in range(C_KV_HEADS):
            rope(seg + g * HEAD_DIM, o_kv + sub * C_KV_HEADS * HEAD_DIM + g * HEAD_DIM,
                 ROPE_DIM // 2, TAB_COS, TAB_SIN)

    k = np.arange(LANES)[:, None]
    live = k < 3 * TAB_PART
    ec = (live & ((k % TAB_PART) == clane[None, :])).astype(np.float32)
    es = (live & ((k % TAB_PART) == slane[None, :])).astype(np.float32) * ssign[None, :]
    starts = {PAT_HEAD: SEG_AQ, PAT_KROPE: SEG_AKV + A_KV_RANK, PAT_IDX: SEG_IQ, PAT_MISC: SEG_MISC}
    blocks = [m[:, starts[p]:starts[p] + LANES] for m in (ec, es) for p in range(N_PAT)]
    return src, np.concatenate(blocks, axis=1)


def _in_proj_kernel(x_ref, tab_ref, g_ref, kvg_ref, w_ref, pat_ref,
                    aqT_ref, akv_ref, acT_ref, iqT_ref, misc_ref, miscT_ref, u_ref,
                    cqT_ref, cqrT_ref, kvc_ref, ks_ref, vsT_ref, kw_ref, vwT_ref):
    x = x_ref[...]
    n = (_rms(x, -1) * g_ref[...]).astype(MXU_DTYPE)
    cs = jnp.dot(tab_ref[...], pat_ref[...], preferred_element_type=F32)

    def plain(c0, w):
        return jnp.dot(n, w_ref[:, c0:c0 + w], preferred_element_type=F32)

    def rotate(p, pat):
        w = p.shape[1]
        nb = w // LANES
        half, period = PAT_GEOMETRY[pat]
        lane = lax.broadcasted_iota(I32, (1, w), 1)
        first = (lane % period) < half
        partner = jnp.where(first, pltpu.roll(p, w - half, 1), pltpu.roll(p, half, 1))
        cos = cs[:, pat * LANES:(pat + 1) * LANES]
        sin = cs[:, (N_PAT + pat) * LANES:(N_PAT + pat + 1) * LANES]
        return p * jnp.concatenate([cos] * nb, axis=1) + partner * jnp.concatenate([sin] * nb, axis=1)

    def chunked_t(val, ref):
        vt = val.T.astype(ref.dtype)
        for c in range(ref.shape[0]):
            ref[c] = vt[:, c * LANES:(c + 1) * LANES]

    aqT_ref[...] = rotate(plain(SEG_AQ, 256), PAT_HEAD).T.astype(aqT_ref.dtype)

    akv = plain(SEG_AKV, 256)
    lat = _rms(akv[:, :A_KV_RANK], -1) * kvg_ref[...]
    k_rope = rotate(akv[:, A_KV_RANK:], PAT_KROPE)
    akv_ref[...] = jnp.concatenate([lat, k_rope], axis=1).astype(akv_ref.dtype)
    chunked_t(lat, acT_ref)

    iqT_ref[...] = rotate(plain(SEG_IQ, 128), PAT_IDX).T.astype(iqT_ref.dtype)
    misc = rotate(plain(SEG_MISC, 128), PAT_MISC)
    misc_ref[...] = misc
    miscT_ref[...] = misc.T
    u_ref[...] = plain(SEG_BU, 256)
    qk_scale = HEAD_DIM ** -0.5 * LOG2E
    cq = plain(SEG_CQ, 512)
    cqT_ref[...] = (cq * qk_scale).T.astype(cqT_ref.dtype)
    cqrT_ref[...] = (rotate(cq, PAT_HEAD) * qk_scale).T.astype(cqrT_ref.dtype)
    kvc_ref[...] = plain(SEG_KVC, 256)
    ks_ref[...] = rotate(plain(SEG_KS, 128), PAT_HEAD).astype(ks_ref.dtype)
    chunked_t(plain(SEG_VS, 128), vsT_ref)
    kw_ref[...] = rotate(plain(SEG_KW, 128), PAT_HEAD).astype(kw_ref.dtype)
    chunked_t(plain(SEG_VW, 128), vwT_ref)


def _in_proj(x, tab, ln_g, kv_g, w_r, pat):
    bsz, seq, _ = x.shape
    tm = min(PROJ_TM, seq)
    nj = seq // tm
    nck = tm // LANES
    bf = MXU_DTYPE

    def rows(w):
        return pl.BlockSpec((None, tm, w), lambda b, j: (b, j, 0))

    def cols(w):
        return pl.BlockSpec((None, w, tm), lambda b, j: (b, 0, j))

    def chunks():
        return pl.BlockSpec((None, nck, LANES, LANES), lambda b, j: (b, j, 0, 0))

    def const(shape):
        return pl.BlockSpec(shape, lambda b, j: (0,) * len(shape))

    sds = jax.ShapeDtypeStruct
    out_shape = (
        sds((bsz, 256, seq), bf),
        sds((bsz, seq, 256), bf),
        sds((bsz, seq // LANES, LANES, LANES), bf),
        sds((bsz, 128, seq), bf),
        sds((bsz, seq, 128), F32),
        sds((bsz, 128, seq), F32),
        sds((seq, bsz * B_WIDTH), F32),
        sds((bsz, 512, seq), bf),
        sds((bsz, 512, seq), bf),
        sds((bsz, seq, 256), F32),
        sds((bsz, seq, 128), bf),
        sds((bsz, seq // LANES, LANES, LANES), bf),
        sds((bsz, seq, 128), bf),
        sds((bsz, seq // LANES, LANES, LANES), bf),
    )
    out_specs = (cols(256), rows(256), chunks(), cols(128), rows(128), cols(128),
                 pl.BlockSpec((tm, B_WIDTH), lambda b, j: (j, b)),
                 cols(512), cols(512), rows(256), rows(128), chunks(), rows(128), chunks())
    return pl.pallas_call(
        _in_proj_kernel,
        out_shape=out_shape,
        grid=(bsz, nj),
        in_specs=[rows(D_MODEL), rows(LANES), const((1, D_MODEL)), const((1, A_KV_RANK)),
                  const((D_MODEL, N_PROJ)), const((LANES, 2 * N_PAT * LANES))],
        out_specs=out_specs,
        compiler_params=_cparams(("arbitrary", "arbitrary")),
        name="in_proj",
    )(x, tab, ln_g.reshape(1, -1), kv_g.reshape(1, -1), w_r, pat)


def _s5_disc_kernel(lr_ref, li_ref, ls_ref, br_ref, bi_ref, ar_ref, ai_ref, bbr_ref, bbi_ref):
    lr, li = lr_ref[...], li_ref[...]
    step = jnp.exp(ls_ref[...])
    mag = jnp.exp(lr * step)
    ar = mag * jnp.cos(li * step)
    ai = mag * jnp.sin(li * step)
    den = lr * lr + li * li
    zr = ((ar - 1.0) * lr + ai * li) / den
    zi = (ai * lr - (ar - 1.0) * li) / den
    br, bi = br_ref[...], bi_ref[...]
    ar_ref[...] = ar
    ai_ref[...] = ai
    bbr_ref[...] = zr * br - zi * bi
    bbi_ref[...] = zr * bi + zi * br


def _s5_discretize(lam_re, lam_im, log_step, b_re, b_im):
    g, p, h = b_re.shape
    ex = lambda a: jnp.repeat(a.astype(F32), h, axis=1)
    ls = jnp.broadcast_to(log_step.astype(F32)[:, None], (g, p * h))
    sds = jax.ShapeDtypeStruct((g, p * h), F32)
    ar, ai, bbr, bbi = pl.pallas_call(
        _s5_disc_kernel, out_shape=(sds, sds, sds, sds), name="s5_discretize",
    )(ex(lam_re), ex(lam_im), ls, b_re.astype(F32).reshape(g, p * h), b_im.astype(F32).reshape(g, p * h))
    ar = ar.reshape(g, p, h)[:, :, 0]
    ai = ai.reshape(g, p, h)[:, :, 0]
    return ar, ai, bbr.reshape(g, p, h), bbi.reshape(g, p, h)


def _s5_kernel(u_ref, bmat_ref, a_ref, cre_ref, cim_ref, d_ref, gw_ref, gb_ref, gain_ref,
               o_ref, u_scr, x_scr, h_scr, *, tc):
    ns = S5_NSTATE

    @pl.when(pl.program_id(0) == 0)
    def _():
        h_scr[...] = jnp.zeros_like(h_scr)

    halves = range(B_WIDTH // LANES)
    for b in range(SUBLANES):
        for hf in halves:
            c0 = b * B_WIDTH + hf * LANES
            u_scr[hf, pl.ds(b, tc, stride=SUBLANES), :] = u_ref[:, c0:c0 + LANES]
    u = jnp.concatenate([u_scr[hf] for hf in halves], axis=1)
    x_scr[...] = _dot(u, bmat_ref[...])
    ar = a_ref[0:SUBLANES, :]
    ai = a_ref[SUBLANES:2 * SUBLANES, :]

    def step(t, carry):
        hr, hi = carry
        r0 = pl.multiple_of(t * SUBLANES, SUBLANES)
        xr = x_scr[pl.ds(r0, SUBLANES), 0:ns]
        xi = x_scr[pl.ds(r0, SUBLANES), ns:2 * ns]
        nhr = ar * hr - ai * hi + xr
        nhi = ar * hi + ai * hr + xi
        x_scr[pl.ds(r0, SUBLANES), 0:ns] = nhr
        x_scr[pl.ds(r0, SUBLANES), ns:2 * ns] = nhi
        return nhr, nhi

    hr, hi = lax.fori_loop(0, tc, step, (h_scr[:, 0:ns], h_scr[:, ns:2 * ns]), unroll=8)
    h_scr[:, 0:ns] = hr
    h_scr[:, ns:2 * ns] = hi

    y = (_dot(x_scr[:, 0:ns], cre_ref[...]) - _dot(x_scr[:, ns:2 * ns], cim_ref[...])
         + d_ref[...] * u)
    y = jax.nn.gelu(y)
    y = y * jax.nn.sigmoid(_dot(y, gw_ref[...]) + gb_ref[...])
    y = _rms(y, -1) * gain_ref[...]
    for hf in halves:
        u_scr[hf] = y[:, hf * LANES:(hf + 1) * LANES]
    for b in range(SUBLANES):
        for hf in halves:
            c0 = b * B_WIDTH + hf * LANES
            o_ref[:, c0:c0 + LANES] = u_scr[hf, pl.ds(b, tc, stride=SUBLANES), :].astype(o_ref.dtype)


def _s5(u_tm, bsz, ar, ai, bbr, bbi, c_re, c_im, d, glu_w, glu_b, gain):
    assert bsz == SUBLANES, "the S5 scan keeps one batch row per sublane"
    seq = u_tm.shape[0]
    tc = min(S5_TC, seq)
    rows = tc * bsz
    eye = jnp.eye(S5_GROUPS, dtype=F32)
    bmat = jnp.concatenate(
        [jnp.einsum('gph,gk->ghkp', b, eye).reshape(B_WIDTH, S5_NSTATE) for b in (bbr, bbi)], axis=1)
    cre = jnp.einsum('ghp,gk->gpkh', c_re.astype(F32), eye).reshape(S5_NSTATE, B_WIDTH)
    cim = jnp.einsum('ghp,gk->gpkh', c_im.astype(F32), eye).reshape(S5_NSTATE, B_WIDTH)
    avec = jnp.concatenate([jnp.broadcast_to(a.reshape(1, S5_NSTATE), (SUBLANES, S5_NSTATE))
                            for a in (ar, ai)], axis=0)
    const = lambda shape: pl.BlockSpec(shape, lambda i: (0,) * len(shape))
    out = pl.pallas_call(
        functools.partial(_s5_kernel, tc=tc),
        out_shape=jax.ShapeDtypeStruct((seq, bsz * B_WIDTH), MXU_DTYPE),
        grid=(seq // tc,),
        in_specs=[pl.BlockSpec((tc, bsz * B_WIDTH), lambda i: (i, 0)),
                  const((B_WIDTH, 2 * S5_NSTATE)), const((2 * SUBLANES, S5_NSTATE)),
                  const((S5_NSTATE, B_WIDTH)), const((S5_NSTATE, B_WIDTH)),
                  const((1, B_WIDTH)), const((B_WIDTH, B_WIDTH)), const((1, B_WIDTH)),
                  const((1, B_WIDTH))],
        out_specs=pl.BlockSpec((tc, bsz * B_WIDTH), lambda i: (i, 0)),
        scratch_shapes=[pltpu.VMEM((B_WIDTH // LANES, rows, LANES), F32),
                        pltpu.VMEM((rows, 2 * S5_NSTATE), F32),
                        pltpu.VMEM((SUBLANES, 2 * S5_NSTATE), F32)],
        compiler_params=_cparams(("arbitrary",)),
        name="s5_scan",
    )(u_tm, bmat.astype(MXU_DTYPE), avec,
      cre.astype(MXU_DTYPE), cim.astype(MXU_DTYPE), d.astype(F32).reshape(1, B_WIDTH),
      glu_w.astype(MXU_DTYPE), glu_b.astype(F32).reshape(1, B_WIDTH), gain.reshape(1, B_WIDTH))
    return out


def _compress_kernel(k_ref, v_ref, pa_ref, pb_ref, wa_ref, wb_ref, w2_ref, cmp_ref, vT_ref):
    nseg = cmp_ref.shape[0]
    cw = k_ref.shape[1] + v_ref.shape[1]
    xa = jnp.zeros((nseg, cw), F32)
    xb = jnp.zeros((nseg, cw), F32)
    for l in range(CMP_STRIDE):
        rows = pl.ds(l, nseg, stride=CMP_STRIDE)
        tok = jnp.concatenate([k_ref[rows, :], v_ref[rows, :]], axis=1)
        cols = slice(l * cw, (l + 1) * cw)
        xa = xa + _dot(tok + pa_ref[:, cols], wa_ref[cols, :])
        xb = xb + _dot(tok + pb_ref[:, cols], wb_ref[cols, :])
    pre = xa + pltpu.roll(xb, nseg - 1, 0)
    out = _dot(jax.nn.gelu(pre), w2_ref[...])
    cmp_ref[...] = out.astype(cmp_ref.dtype)
    vT_ref[...] = out[:, LANES:].T.astype(vT_ref.dtype)


def _compress(kvc, pos_k, pos_v, k_w1, k_w2, v_w1, v_w2):
    bsz, seq, _ = kvc.shape
    nseg = seq // CMP_STRIDE
    width = CMP_STRIDE * 256
    eye = jnp.eye(4, dtype=F32)
    w1 = jnp.stack([k_w1, k_w1, v_w1, v_w1]).astype(F32).reshape(4, CMP_BLOCK, HEAD_DIM, HEAD_DIM)
    pos = jnp.stack([pos_k, pos_k, pos_v, pos_v]).astype(F32)

    def half(lo):
        w = jnp.einsum('slde,st->lsdte', w1[:, lo:lo + CMP_STRIDE], eye).reshape(width, 256)
        p = jnp.transpose(pos[:, lo:lo + CMP_STRIDE], (1, 0, 2)).reshape(1, width)
        return w.astype(MXU_DTYPE), p

    wa, pa = half(0)
    wb, pb = half(CMP_STRIDE)
    w2 = jnp.einsum('sde,st->sdte', jnp.stack([k_w2, k_w2, v_w2, v_w2]).astype(F32), eye).reshape(256, 256)
    const = lambda shape: pl.BlockSpec(shape, lambda b: (0,) * len(shape))
    return pl.pallas_call(
        _compress_kernel,
        out_shape=(jax.ShapeDtypeStruct((bsz, nseg, 256), MXU_DTYPE),
                   jax.ShapeDtypeStruct((bsz, 128, nseg), MXU_DTYPE)),
        grid=(bsz,),
        in_specs=[pl.BlockSpec((None, seq, LANES), lambda b: (b, 0, 0)),
                  pl.BlockSpec((None, seq, LANES), lambda b: (b, 0, 1)),
                  const((1, width)), const((1, width)), const((width, 256)), const((width, 256)),
                  const((256, 256))],
        out_specs=(pl.BlockSpec((None, nseg, 256), lambda b: (b, 0, 0)),
                   pl.BlockSpec((None, 128, nseg), lambda b: (b, 0, 0))),
        compiler_params=_cparams(("arbitrary",)),
        name="nsa_compress",
    )(kvc, kvc, pa, pb, wa, wb, w2.astype(MXU_DTYPE))


ONES_ROWS = 16
NQ4 = 4 * Q_BLOCK


def _with_ones(v_t):
    return jnp.concatenate([v_t, jnp.ones((ONES_ROWS, v_t.shape[1]), v_t.dtype)], axis=0)


def _online_softmax_step(s, bias, v_ext, m_ref, acc_ref):
    s = s + bias
    m_old = m_ref[...]
    m_new = jnp.maximum(m_old, jnp.max(s, axis=0, keepdims=True))
    p = jnp.exp2(s - m_new)
    acc_ref[...] = jnp.exp2(m_old - m_new) * acc_ref[...] + _dot(v_ext, p)
    m_ref[...] = m_new


def _tile4(x):
    return jnp.concatenate([x, x, x, x], axis=1)


INT_MIN = -2 ** 31
NEG_INF_KEY = int(np.int32(np.uint32(0xFF800000 ^ 0x7FFFFFFF)))


def _dsa_kernel(qT_ref, iqT_ref, miscT_ref, kv_ref, misc_ref, cT_ref, mq_ref, wuv_ref, gain_ref, tri_ref,
                o_ref, key_scr, m_scr, acc_scr, *, k_top, max_chunks):
    i = pl.program_id(1)
    t0 = i * Q_BLOCK
    kc = DSA_KEY_CHUNK
    nch = (t0 + Q_BLOCK + kc - 1) // kc
    t_lane = t0 + lax.broadcasted_iota(I32, (1, Q_BLOCK), 1)
    row_iota = lax.broadcasted_iota(I32, (kc, Q_BLOCK), 0)

    qcat = (_dot(mq_ref[...], qT_ref[...]) * (HEAD_DIM ** -0.5 * LOG2E)).astype(MXU_DTYPE)
    qs_t = jnp.concatenate([qcat[h * 256:(h + 1) * 256] for h in range(A_HEADS)], axis=1)
    iq_t = iqT_ref[...]
    iq_all = jnp.concatenate([iq_t[h * IDX_DIM:(h + 1) * IDX_DIM, :] for h in range(IDX_HEADS)], axis=1)
    w_t = miscT_ref[MISC_IW:MISC_IW + IDX_HEADS, :] * (IDX_HEADS ** -0.5 * IDX_DIM ** -0.5)

    def idx_body(c, diagonal):
        k0 = c * kc
        ik = misc_ref[pl.ds(k0, kc), MISC_IK:MISC_IK + IDX_DIM]
        d = _dot(ik, iq_all)
        score = jnp.zeros((kc, Q_BLOCK), F32)
        for h in range(IDX_HEADS):
            score = score + jnp.maximum(d[:, h * Q_BLOCK:(h + 1) * Q_BLOCK], 0.0) * w_t[h:h + 1, :]
        if diagonal:
            score = jnp.where(row_iota + k0 <= t_lane, score, -jnp.inf)
        bits = pltpu.bitcast(score, I32)
        key_scr[pl.ds(k0, kc), :] = bits ^ ((bits >> 31) & 0x7FFFFFFF)

    m_scr[...] = jnp.full_like(m_scr, NEG)
    acc_scr[...] = jnp.zeros_like(acc_scr)

    def passes(n_chunks):
        for c in range(n_chunks):
            idx_body(c, c == n_chunks - 1)

        def count(pred):
            acc = jnp.zeros((SUBLANES, Q_BLOCK), I32)
            for c in range(n_chunks):
                hit = pred(key_scr[c * kc:(c + 1) * kc, :])
                acc = acc + jnp.sum(hit.reshape(kc // SUBLANES, SUBLANES, Q_BLOCK), axis=0)
            return jnp.sum(acc, axis=0, keepdims=True)

        def thr_bit(b, prefix):
            cand = prefix | lax.shift_left(jnp.int32(1), 31 - b)
            cand_s = cand ^ INT_MIN
            cnt = count(lambda keys: jnp.where(keys >= cand_s, 1, 0))
            return jnp.where(cnt >= k_top, cand, prefix)

        thr = lax.fori_loop(0, 32, thr_bit, jnp.zeros((1, Q_BLOCK), I32)) ^ INT_MIN
        n_gt = count(lambda keys: jnp.where(keys > thr, 1, 0))
        need = jnp.where(thr == NEG_INF_KEY, 0, k_top - n_gt).astype(F32)

        n_tied = jnp.zeros((1, Q_BLOCK), F32)
        for c in range(n_chunks):
            keys = key_scr[c * kc:(c + 1) * kc, :]
            tied = _dot(tri_ref[...], jnp.where(keys == thr, 1.0, 0.0)) + n_tied
            bias = jnp.where(keys > thr, 0.0,
                             jnp.where(keys == thr, jnp.where(tied <= need, 0.0, NEG), NEG))
            v_t = jnp.concatenate([cT_ref[c * (kc // LANES) + j] for j in range(kc // LANES)], axis=1)
            s = _dot(kv_ref[c * kc:(c + 1) * kc, :], qs_t)
            _online_softmax_step(s, _tile4(bias), _with_ones(v_t), m_scr, acc_scr)
            n_tied = tied[kc - 1:kc, :]
        return 0

    lax.switch(nch - 1, [functools.partial(passes, n) for n in range(1, max_chunks + 1)])

    acc = acc_scr[...]
    o_lat = (acc[:A_KV_RANK] / acc[A_KV_RANK:A_KV_RANK + 1]).astype(MXU_DTYPE)
    out_t = jnp.concatenate(
        [_dot(wuv_ref[h], o_lat[:, h * Q_BLOCK:(h + 1) * Q_BLOCK]) for h in range(A_HEADS)], axis=0)
    out_t = _rms(out_t, 0) * gain_ref[...]
    o_ref[...] = out_t.T.astype(o_ref.dtype)


def _dsa(aq_t, iq_t, misc_t, akv, misc, ac_t, w_uk, w_uv, gain):
    bsz, _, seq = aq_t.shape
    nq = seq // Q_BLOCK
    k_top = min(DSA_TOPK, seq // 4)
    seq_pad = -(-seq // DSA_KEY_CHUNK) * DSA_KEY_CHUNK
    tri = jnp.asarray(np.tril(np.ones((DSA_KEY_CHUNK, DSA_KEY_CHUNK), np.float32)), MXU_DTYPE)
    mq = jnp.zeros((A_HEADS, 256, A_HEADS, HEAD_DIM), F32)
    for h in range(A_HEADS):
        mq = mq.at[h, :A_KV_RANK, h, ROPE_DIM:].set(w_uk[:, h, :].astype(F32))
        mq = mq.at[h, A_KV_RANK:A_KV_RANK + ROPE_DIM, h, :ROPE_DIM].set(jnp.eye(ROPE_DIM, dtype=F32))
    mq = mq.reshape(A_HEADS * 256, A_HEADS * HEAD_DIM).astype(MXU_DTYPE)
    wuv_t = jnp.transpose(w_uv, (1, 2, 0)).astype(MXU_DTYPE)
    per_q = lambda w: pl.BlockSpec((None, w, Q_BLOCK), lambda b, i: (b, 0, i))
    per_b = lambda *s: pl.BlockSpec((None,) + s, lambda b, i: (b,) + (0,) * len(s))
    const = lambda shape: pl.BlockSpec(shape, lambda b, i: (0,) * len(shape))
    return pl.pallas_call(
        functools.partial(_dsa_kernel, k_top=k_top, max_chunks=seq_pad // DSA_KEY_CHUNK),
        out_shape=jax.ShapeDtypeStruct((bsz, seq, A_HEADS * A_VDIM), MXU_DTYPE),
        grid=(bsz, nq),
        in_specs=[per_q(256), per_q(128), per_q(128), per_b(seq, 256), per_b(seq, 128),
                  per_b(seq // LANES, LANES, LANES), const(mq.shape), const(wuv_t.shape),
                  const((A_HEADS * A_VDIM, 1)), const(tri.shape)],
        out_specs=pl.BlockSpec((None, Q_BLOCK, A_HEADS * A_VDIM), lambda b, i: (b, i, 0)),
        scratch_shapes=[pltpu.VMEM((seq_pad, Q_BLOCK), I32),
                        pltpu.VMEM((1, NQ4), F32),
                        pltpu.VMEM((A_KV_RANK + ONES_ROWS, NQ4), F32)],
        compiler_params=_cparams(("arbitrary", "arbitrary")),
        name="dsa_attention",
    )(aq_t, iq_t, misc_t, akv, misc, ac_t, mq, wuv_t, gain.reshape(-1, 1), tri)


def _nsa_kernel(qT_ref, qrT_ref, miscT_ref, cmp_ref, vcT_ref, ks_ref, vsT_ref, kw_ref, vwT_ref,
                ov_ref, gain_ref, o_ref, sel_scr, m_scr, acc_scr, win_scr, *, n_top, n_blk, max_chunks):
    i = pl.program_id(1)
    t0 = i * Q_BLOCK
    kc = NSA_KEY_CHUNK
    nch = (t0 + Q_BLOCK + kc - 1) // kc
    t_lane = t0 + lax.broadcasted_iota(I32, (1, Q_BLOCK), 1)
    gates = jax.nn.sigmoid(miscT_ref[MISC_GATE:MISC_GATE + 3 * C_HEADS, :])
    n_cmp = cmp_ref.shape[0]
    cmp_iota = lax.broadcasted_iota(I32, (n_cmp, Q_BLOCK), 0)
    blk_iota = lax.broadcasted_iota(I32, (n_blk, Q_BLOCK), 0)
    row_iota = lax.broadcasted_iota(I32, (kc, Q_BLOCK), 0)
    win_iota = lax.broadcasted_iota(I32, (Q_BLOCK, Q_BLOCK), 0)
    groups = range(C_KV_HEADS)
    gsl = [slice(g * HEAD_DIM, (g + 1) * HEAD_DIM) for g in groups]

    def heads_t(ref, g):
        return jnp.concatenate([ref[h * HEAD_DIM:(h + 1) * HEAD_DIM, :]
                                for h in range(g * C_GROUP, (g + 1) * C_GROUP)], axis=1)

    o_c = []
    for g in groups:
        s_c = _dot(cmp_ref[:, gsl[g]], heads_t(qT_ref, g))
        vis = _tile4(jnp.where(cmp_iota * CMP_STRIDE + (CMP_BLOCK - 1) <= t_lane, 1, 0)) > 0
        s_c = jnp.where(vis, s_c, NEG)
        p_c = jnp.where(vis, jnp.exp2(s_c - jnp.max(s_c, axis=0, keepdims=True)), 0.0)
        l_c = jnp.sum(p_c, axis=0, keepdims=True)
        p_c = p_c * (1.0 / jnp.maximum(l_c, 1e-30))
        o_c.append(_dot(vcT_ref[gsl[g], :], p_c))

        p_sum = (p_c[:, 0:Q_BLOCK] + p_c[:, Q_BLOCK:2 * Q_BLOCK]
                 + p_c[:, 2 * Q_BLOCK:3 * Q_BLOCK] + p_c[:, 3 * Q_BLOCK:4 * Q_BLOCK])
        p_hi = p_sum.astype(MXU_DTYPE)
        p_lo = p_sum - p_hi.astype(F32)
        imp = _dot(ov_ref[...], p_hi) + _dot(ov_ref[...], p_lo)
        cur = t_lane // SEL_BLOCK
        forced = jnp.where(blk_iota == 0, 1, jnp.where(blk_iota == cur, 1,
                           jnp.where(blk_iota == cur - 1, 1, 0)))
        imp = jnp.where(forced > 0, SEL_FORCE, imp)
        imp = jnp.where(blk_iota * SEL_BLOCK <= t_lane, imp, -jnp.inf)
        n_grp = n_blk // SUBLANES
        imp_g = [imp[v * SUBLANES:(v + 1) * SUBLANES, :] for v in range(n_grp)]
        rank_g = [jnp.zeros((SUBLANES, Q_BLOCK), I32) for _ in range(n_grp)]
        for mp in range(n_blk):
            row = imp[mp:mp + 1, :]
            for v in range(n_grp):
                if v > mp // SUBLANES:
                    beats = jnp.where(row >= imp_g[v], 1, 0)
                elif v < mp // SUBLANES:
                    beats = jnp.where(row > imp_g[v], 1, 0)
                else:
                    later = blk_iota[v * SUBLANES:(v + 1) * SUBLANES, :] > mp
                    beats = jnp.where(row > imp_g[v], 1, jnp.where(row == imp_g[v], jnp.where(later, 1, 0), 0))
                rank_g[v] = rank_g[v] + beats
        rank = jnp.concatenate(rank_g, axis=0)
        sel_scr[g, 0:n_blk, :] = jnp.where(rank < n_top, 0.0, NEG)

    m_scr[...] = jnp.full_like(m_scr, NEG)
    acc_scr[...] = jnp.zeros_like(acc_scr)
    bpc = kc // SEL_BLOCK

    def sel_body(c, diagonal):
        k0 = c * kc
        for g in groups:
            s = _dot(ks_ref[pl.ds(k0, kc), gsl[g]], heads_t(qrT_ref, g))
            sel8 = sel_scr[g, pl.ds(c * bpc, bpc), :]
            bias = jnp.concatenate(
                [jnp.broadcast_to(sel8[j:j + 1, :], (SEL_BLOCK, Q_BLOCK)) for j in range(bpc)], axis=0)
            if diagonal:
                bias = jnp.where(row_iota + k0 <= t_lane, bias, NEG)
            v_t = jnp.concatenate([vsT_ref[c * (kc // LANES) + j][gsl[g], :]
                                   for j in range(kc // LANES)], axis=1)
            _online_softmax_step(s, _tile4(bias), _with_ones(v_t), m_scr.at[g], acc_scr.at[g])

    def window_branch(g):
        k_parts, v_parts, m_parts = [], [], []
        for j in range(WINDOW // Q_BLOCK + 1):
            cj = i - WINDOW // Q_BLOCK + j
            cjc = jnp.maximum(cj, 0)
            k0 = pl.multiple_of(cjc * Q_BLOCK, Q_BLOCK)
            k_parts.append(kw_ref[pl.ds(k0, Q_BLOCK), gsl[g]])
            v_parts.append(vwT_ref[cjc][gsl[g], :])
            kidx = win_iota + k0
            inside = jnp.where(kidx <= t_lane, jnp.where(kidx > t_lane - WINDOW, 0.0, NEG), NEG)
            m_parts.append(jnp.where(cj >= 0, inside, NEG))
        s_w = (_dot(jnp.concatenate(k_parts, axis=0), heads_t(qrT_ref, g))
               + _tile4(jnp.concatenate(m_parts, axis=0)))
        p_w = jnp.exp2(s_w - jnp.max(s_w, axis=0, keepdims=True))
        acc = _dot(_with_ones(jnp.concatenate(v_parts, axis=1)), p_w)
        win_scr[g] = acc[:HEAD_DIM] / acc[HEAD_DIM:HEAD_DIM + 1]

    def sweep(n_chunks):
        for g in groups:
            window_branch(g)
        for c in range(n_chunks):
            sel_body(c, c == n_chunks - 1)
        return 0

    lax.switch(nch - 1, [functools.partial(sweep, n) for n in range(1, max_chunks + 1)])

    slabs = []
    for g in groups:
        acc = acc_scr[g]
        o_s = acc[:HEAD_DIM] / acc[HEAD_DIM:HEAD_DIM + 1]
        o_w = win_scr[g]
        for hh in range(C_GROUP):
            sl = slice(hh * Q_BLOCK, (hh + 1) * Q_BLOCK)
            r = (g * C_GROUP + hh) * 3
            slabs.append(gates[r:r + 1, :] * o_c[g][:, sl] + gates[r + 1:r + 2, :] * o_s[:, sl]
                         + gates[r + 2:r + 3, :] * o_w[:, sl])

    out_t = jnp.concatenate(slabs, axis=0)
    out_t = _rms(out_t, 0) * gain_ref[...]
    o_ref[...] = out_t.T.astype(o_ref.dtype)


def _sel_overlap_t(n_cmp_rows, n_blk):
    cs = np.arange(n_cmp_rows)[None, :] * CMP_STRIDE
    ss = np.arange(n_blk)[:, None] * SEL_BLOCK
    ov = np.minimum(cs + CMP_BLOCK, ss + SEL_BLOCK) - np.maximum(cs, ss)
    return np.clip(ov, 0, None).astype(np.float32) / CMP_BLOCK


def _nsa(cq_t, cqr_t, misc_t, cmp, vcmp_t, ks, vs_t, kw, vw_t, gain):
    bsz, _, seq = cq_t.shape
    nq = seq // Q_BLOCK
    nseg = cmp.shape[1]
    n_blk = seq // SEL_BLOCK
    n_top = min(SEL_TOPN, n_blk)
    seq_pad = -(-seq // NSA_KEY_CHUNK) * NSA_KEY_CHUNK
    ov_np = _sel_overlap_t(nseg, n_blk)
    ov_np[:, (seq - CMP_BLOCK) // CMP_STRIDE + 1:] = 0.0
    ov = jnp.asarray(ov_np, MXU_DTYPE)
    per_q = lambda w: pl.BlockSpec((None, w, Q_BLOCK), lambda b, i: (b, 0, i))
    per_b = lambda *s: pl.BlockSpec((None,) + s, lambda b, i: (b,) + (0,) * len(s))
    const = lambda shape: pl.BlockSpec(shape, lambda b, i: (0,) * len(shape))
    return pl.pallas_call(
        functools.partial(_nsa_kernel, n_top=n_top, n_blk=n_blk, max_chunks=seq_pad // NSA_KEY_CHUNK),
        out_shape=jax.ShapeDtypeStruct((bsz, seq, C_HEADS * HEAD_DIM), MXU_DTYPE),
        grid=(bsz, nq),
        in_specs=[per_q(512), per_q(512), per_q(128), per_b(nseg, 256), per_b(128, nseg),
                  per_b(seq, 128), per_b(seq // LANES, LANES, LANES),
                  per_b(seq, 128), per_b(seq // LANES, LANES, LANES),
                  const(ov.shape), const((C_HEADS * HEAD_DIM, 1))],
        out_specs=pl.BlockSpec((None, Q_BLOCK, C_HEADS * HEAD_DIM), lambda b, i: (b, i, 0)),
        scratch_shapes=[pltpu.VMEM((C_KV_HEADS, max(n_blk, seq_pad // SEL_BLOCK), Q_BLOCK), F32),
                        pltpu.VMEM((C_KV_HEADS, 1, NQ4), F32),
                        pltpu.VMEM((C_KV_HEADS, HEAD_DIM + ONES_ROWS, NQ4), F32),
                        pltpu.VMEM((C_KV_HEADS, HEAD_DIM, NQ4), F32)],
        compiler_params=_cparams(("arbitrary", "arbitrary")),
        name="nsa_attention",
    )(cq_t, cqr_t, misc_t, cmp, vcmp_t, ks, vs_t, kw, vw_t, ov, gain.reshape(-1, 1))


def _out_mlp_kernel(x_ref, a_ref, b_ref, c_ref, wo_ref, g2_ref, wu_ref, wd_ref, fg_ref,
                    o_ref, x1_scr, n2_scr, acc_scr, *, final_norm):
    j = pl.program_id(1)

    @pl.when(j == 0)
    def _():
        mixed = (jnp.dot(a_ref[...], wo_ref[0:256, :], preferred_element_type=F32)
                 + jnp.dot(b_ref[...], wo_ref[256:512, :], preferred_element_type=F32)
                 + jnp.dot(c_ref[...], wo_ref[512:1024, :], preferred_element_type=F32))
        x1 = x_ref[...] + mixed
        x1_scr[...] = x1
        n2_scr[...] = (_rms(x1, -1) * g2_ref[...]).astype(n2_scr.dtype)
        acc_scr[...] = jnp.zeros_like(acc_scr)

    z = jnp.dot(n2_scr[...], wu_ref[...], preferred_element_type=F32)
    acc_scr[...] += _dot(jnp.square(jnp.maximum(z, 0.0)), wd_ref[...])

    @pl.when(j == pl.num_programs(1) - 1)
    def _():
        x2 = x1_scr[...] + acc_scr[...]
        if final_norm:
            x2 = _rms(x2, -1) * fg_ref[...]
        o_ref[...] = x2


def _out_mlp(x, a_n, b_tm, c_n, w_out, ln2_g, w_up, w_down, final_g, final_norm):
    bsz, seq, _ = x.shape
    tm = min(MLP_TM, seq)
    nj = seq // tm
    nf = D_FF // MLP_TF
    rows = lambda w: pl.BlockSpec((None, tm, w), lambda r, f: (r // nj, r % nj, 0))
    const = lambda shape: pl.BlockSpec(shape, lambda r, f: (0,) * len(shape))
    return pl.pallas_call(
        functools.partial(_out_mlp_kernel, final_norm=final_norm),
        out_shape=jax.ShapeDtypeStruct(x.shape, F32),
        grid=(bsz * nj, nf),
        in_specs=[rows(D_MODEL), rows(256),
                  pl.BlockSpec((tm, B_WIDTH), lambda r, f: (r % nj, r // nj)),
                  rows(512), const((D_MODEL, D_MODEL)), const((1, D_MODEL)),
                  pl.BlockSpec((D_MODEL, MLP_TF), lambda r, f: (0, f)),
                  pl.BlockSpec((MLP_TF, D_MODEL), lambda r, f: (f, 0)),
                  const((1, D_MODEL))],
        out_specs=rows(D_MODEL),
        scratch_shapes=[pltpu.VMEM((tm, D_MODEL), F32), pltpu.VMEM((tm, D_MODEL), MXU_DTYPE),
                        pltpu.VMEM((tm, D_MODEL), F32)],
        compiler_params=_cparams(("arbitrary", "arbitrary")),
        name="out_proj_mlp",
    )(x, a_n, b_tm, c_n, w_out.astype(MXU_DTYPE), ln2_g.reshape(1, -1), w_up.astype(MXU_DTYPE),
      w_down.astype(MXU_DTYPE), final_g.reshape(1, -1))


def kernel(x, positions, ln1_g, w_in, kv_norm_g, w_uk, w_uv, s5_lambda_re, s5_lambda_im, s5_log_step, s5_b_re, s5_b_im, s5_c_re, s5_c_im, s5_d, s5_glu_w, s5_glu_b, cmp_pos_k, cmp_pos_v, cmp_k_w1, cmp_k_w2, cmp_v_w1, cmp_v_w2, gain_a, gain_b, gain_c, w_out, ln2_g, w_up, w_down, final_g):
    bsz, seq, _ = x.shape
    depth = w_in.shape[0]
    src, pat_np = _proj_layout()
    pat = jnp.asarray(pat_np, jnp.bfloat16)
    tab = _rope_table(positions)

    def regather(w, idx):
        cols = jnp.take(w, jnp.asarray(np.maximum(idx, 0)), axis=1)
        return jnp.where(jnp.asarray(idx >= 0)[None, :], cols, 0.0).astype(MXU_DTYPE)

    for layer in range(depth):
        w_r = regather(w_in[layer], src)
        (aq_t, akv, ac_t, iq_t, misc, misc_t, u_tm, cq_t, cqr_t, kvc, ks, vs_t, kw, vw_t) = _in_proj(
            x, tab, ln1_g[layer], kv_norm_g[layer], w_r, pat)

        ar, ai, bbr, bbi = _s5_discretize(s5_lambda_re[layer], s5_lambda_im[layer], s5_log_step[layer],
                                          s5_b_re[layer], s5_b_im[layer])
        b_tm = _s5(u_tm, bsz, ar, ai, bbr, bbi, s5_c_re[layer], s5_c_im[layer], s5_d[layer].reshape(-1),
                   s5_glu_w[layer], s5_glu_b[layer], gain_b[layer])

        cmp, vcmp_t = _compress(kvc, cmp_pos_k[layer], cmp_pos_v[layer], cmp_k_w1[layer],
                                cmp_k_w2[layer], cmp_v_w1[layer], cmp_v_w2[layer])
        a_n = _dsa(aq_t, iq_t, misc_t, akv, misc, ac_t, w_uk[layer], w_uv[layer], gain_a[layer])
        c_n = _nsa(cq_t, cqr_t, misc_t, cmp, vcmp_t, ks, vs_t, kw, vw_t, gain_c[layer])

        x = _out_mlp(x, a_n, b_tm, c_n, w_out[layer], ln2_g[layer], w_up[layer], w_down[layer],
                     final_g, final_norm=(layer == depth - 1))
    return x
```

```python
import functools
import math

import numpy as np
import jax
import jax.numpy as jnp
from jax import lax
from jax.experimental import pallas as pl
from jax.experimental.pallas import tpu as pltpu

F32 = jnp.float32
I32 = jnp.int32
MXU_DTYPE = jnp.bfloat16

D_MODEL = 1024
HEAD_DIM = 64
ROPE_THETA = 500000.0
ROPE_DIM = HEAD_DIM // 4
NORM_EPS = 1e-6
Q_BLOCK = 128
NEG = -1e30
LOG2E = math.log2(math.e)
D_FF = 4 * D_MODEL

A_HEADS = 4
A_NOPE = HEAD_DIM - ROPE_DIM
A_VDIM = HEAD_DIM
A_KV_RANK = 128
IDX_HEADS = 4
IDX_DIM = 32
IDX_ROPE = IDX_DIM // 4
DSA_TOPK = 256

B_WIDTH = 256
S5_GROUP = 16
S5_GROUPS = B_WIDTH // S5_GROUP
S5_STATE = 64
S5_NSTATE = S5_GROUPS * S5_STATE

C_HEADS = 8
C_KV_HEADS = 2
C_GROUP = C_HEADS // C_KV_HEADS
CMP_BLOCK = 32
CMP_STRIDE = 16
SEL_BLOCK = 64
SEL_TOPN = 16
SEL_FORCE = 1e9
WINDOW = 512

IN_SIZES = (A_HEADS * HEAD_DIM, A_KV_RANK, ROPE_DIM, IDX_HEADS * IDX_DIM, IDX_DIM, IDX_HEADS,
            B_WIDTH, C_HEADS * HEAD_DIM, 6 * C_KV_HEADS * HEAD_DIM, 3 * C_HEADS)

LANES = 128
SUBLANES = 8
VMEM_LIMIT = 56 * 1024 * 1024

SEG_AQ, SEG_AKV, SEG_IQ, SEG_MISC, SEG_BU, SEG_CQ, SEG_KVC, SEG_KS, SEG_VS, SEG_KW, SEG_VW = (
    0, 256, 512, 640, 768, 1024, 1536, 1792, 1920, 2048, 2176)
N_PROJ = 2304
MISC_IK, MISC_IW, MISC_GATE = 0, 32, 36
TAB_COS, TAB_SIN, TAB_ICOS, TAB_ISIN, TAB_ONE = 0, 8, 16, 20, 24
TAB_PART = 32

PROJ_TM = 512
DSA_KEY_CHUNK = 512
NSA_KEY_CHUNK = 1024
S5_TC = 128
MLP_TM = 1024
MLP_TF = 1024


def _cparams(sem):
    return pltpu.CompilerParams(dimension_semantics=sem, vmem_limit_bytes=VMEM_LIMIT)


def _dot(a, b):
    return jnp.dot(a.astype(MXU_DTYPE), b.astype(MXU_DTYPE), preferred_element_type=F32)


def _rms(x, axis):
    return x * lax.rsqrt(jnp.mean(x * x, axis=axis, keepdims=True) + NORM_EPS)


def _rope_tab_kernel(pos_ref, freq_ref, out_ref):
    ang = pos_ref[...].astype(F32) * freq_ref[...]
    lane = lax.broadcasted_iota(I32, ang.shape, 1)
    l32 = lane % TAB_PART
    is_cos = (l32 < TAB_SIN) | ((l32 >= TAB_ICOS) & (l32 < TAB_ISIN))
    is_sin = ((l32 >= TAB_SIN) & (l32 < TAB_ICOS)) | ((l32 >= TAB_ISIN) & (l32 < TAB_ONE))
    val = jnp.where(is_cos, jnp.cos(ang),
                    jnp.where(is_sin, jnp.sin(ang), jnp.where(l32 == TAB_ONE, 1.0, 0.0)))
    hi = val.astype(jnp.bfloat16).astype(F32)
    r1 = val - hi
    mid = r1.astype(jnp.bfloat16).astype(F32)
    lo = r1 - mid
    part = lane // TAB_PART
    out = jnp.where(part == 0, hi, jnp.where(part == 1, mid, jnp.where(part == 2, lo, 0.0)))
    out_ref[...] = out.astype(jnp.bfloat16)


def _rope_table(positions):
    bsz, seq = positions.shape
    t = bsz * seq
    inv_r = (np.float32(ROPE_THETA) ** (-np.arange(0, ROPE_DIM, 2, dtype=np.float32) / ROPE_DIM))
    inv_i = (np.float32(ROPE_THETA) ** (-np.arange(0, IDX_ROPE, 2, dtype=np.float32) / IDX_ROPE))
    f32 = np.zeros(TAB_PART, np.float32)
    f32[TAB_COS:TAB_COS + 8] = inv_r
    f32[TAB_SIN:TAB_SIN + 8] = inv_r
    f32[TAB_ICOS:TAB_ICOS + 4] = inv_i
    f32[TAB_ISIN:TAB_ISIN + 4] = inv_i
    freq = jnp.asarray(np.tile(f32, LANES // TAB_PART)[None, :])
    tq = min(1024, t)
    return pl.pallas_call(
        _rope_tab_kernel,
        out_shape=jax.ShapeDtypeStruct((t, LANES), jnp.bfloat16),
        grid=(t // tq,),
        in_specs=[pl.BlockSpec((tq, 1), lambda i: (i, 0)),
                  pl.BlockSpec((1, LANES), lambda i: (0, 0))],
        out_specs=pl.BlockSpec((tq, LANES), lambda i: (i, 0)),
        compiler_params=_cparams(("arbitrary",)),
        name="rope_table",
    )(positions.reshape(t, 1), freq).reshape(bsz, seq, LANES)


PAT_HEAD, PAT_KROPE, PAT_IDX, PAT_MISC = range(4)
N_PAT = 4
PAT_GEOMETRY = {PAT_HEAD: (ROPE_DIM // 2, HEAD_DIM), PAT_KROPE: (ROPE_DIM // 2, LANES),
                PAT_IDX: (IDX_ROPE // 2, IDX_DIM), PAT_MISC: (IDX_ROPE // 2, LANES)}


def _proj_layout():
    offs = np.concatenate([[0], np.cumsum(IN_SIZES)])
    o_aq, o_ckv, o_kr, o_iq, o_ik, o_iw, o_bu, o_cq, o_kv, o_gate = offs[:10]
    src = -np.ones(N_PROJ, np.int64)
    clane = np.full(N_PROJ, TAB_ONE, np.int64)
    slane = -np.ones(N_PROJ, np.int64)
    ssign = np.zeros(N_PROJ, np.float32)

    def plain(c0, o0, w):
        src[c0:c0 + w] = np.arange(o0, o0 + w)

    def rope(c0, o0, half, cos_lane, sin_lane):
        for j in range(half):
            clane[c0 + j] = clane[c0 + half + j] = cos_lane + j
            slane[c0 + j] = slane[c0 + half + j] = sin_lane + j
            ssign[c0 + j] = -1.0
            ssign[c0 + half + j] = 1.0

    plain(SEG_AQ, o_aq, A_HEADS * HEAD_DIM)
    for h in range(A_HEADS):
        rope(SEG_AQ + h * HEAD_DIM, o_aq + h * HEAD_DIM, ROPE_DIM // 2, TAB_COS, TAB_SIN)
    plain(SEG_AKV, o_ckv, A_KV_RANK)
    plain(SEG_AKV + A_KV_RANK, o_kr, ROPE_DIM)
    rope(SEG_AKV + A_KV_RANK, o_kr, ROPE_DIM // 2, TAB_COS, TAB_SIN)
    plain(SEG_IQ, o_iq, IDX_HEADS * IDX_DIM)
    for h in range(IDX_HEADS):
        rope(SEG_IQ + h * IDX_DIM, o_iq + h * IDX_DIM, IDX_ROPE // 2, TAB_ICOS, TAB_ISIN)
    plain(SEG_MISC + MISC_IK, o_ik, IDX_DIM)
    rope(SEG_MISC + MISC_IK, o_ik, IDX_ROPE // 2, TAB_ICOS, TAB_ISIN)
    plain(SEG_MISC + MISC_IW, o_iw, IDX_HEADS)
    plain(SEG_MISC + MISC_GATE, o_gate, 3 * C_HEADS)
    plain(SEG_BU, o_bu, B_WIDTH)
    plain(SEG_CQ, o_cq, C_HEADS * HEAD_DIM)
    for h in range(C_HEADS):
        rope(SEG_CQ + h * HEAD_DIM, o_cq + h * HEAD_DIM, ROPE_DIM // 2, TAB_COS, TAB_SIN)
    plain(SEG_KVC, o_kv, 6 * C_KV_HEADS * HEAD_DIM)
    for seg, sub in ((SEG_KS, 2), (SEG_KW, 4)):
        for g in range(C_KV_HEADS):
            rope(seg + g * HEAD_DIM, o_kv + sub * C_KV_HEADS * HEAD_DIM + g * HEAD_DIM,
                 ROPE_DIM // 2, TAB_COS, TAB_SIN)

    k = np.arange(LANES)[:, None]
    live = k < 3 * TAB_PART
    ec = (live & ((k % TAB_PART) == clane[None, :])).astype(np.float32)
    es = (live & ((k % TAB_PART) == slane[None, :])).astype(np.float32) * ssign[None, :]
    starts = {PAT_HEAD: SEG_AQ, PAT_KROPE: SEG_AKV + A_KV_RANK, PAT_IDX: SEG_IQ, PAT_MISC: SEG_MISC}
    blocks = [m[:, starts[p]:starts[p] + LANES] for m in (ec, es) for p in range(N_PAT)]
    return src, np.concatenate(blocks, axis=1)


def _in_proj_kernel(x_ref, tab_ref, g_ref, kvg_ref, w_ref, pat_ref,
                    aqT_ref, akv_ref, acT_ref, iqT_ref, misc_ref, miscT_ref, u_ref,
                    cqT_ref, cqrT_ref, kvc_ref, ks_ref, vsT_ref, kw_ref, vwT_ref):
    x = x_ref[...]
    n = (_rms(x, -1) * g_ref[...]).astype(MXU_DTYPE)
    cs = jnp.dot(tab_ref[...], pat_ref[...], preferred_element_type=F32)

    def plain(c0, w):
        return jnp.dot(n, w_ref[:, c0:c0 + w], preferred_element_type=F32)

    def rotate(p, pat):
        w = p.shape[1]
        nb = w // LANES
        half, period = PAT_GEOMETRY[pat]
        lane = lax.broadcasted_iota(I32, (1, w), 1)
        first = (lane % period) < half
        partner = jnp.where(first, pltpu.roll(p, w - half, 1), pltpu.roll(p, half, 1))
        cos = cs[:, pat * LANES:(pat + 1) * LANES]
        sin = cs[:, (N_PAT + pat) * LANES:(N_PAT + pat + 1) * LANES]
        return p * jnp.concatenate([cos] * nb, axis=1) + partner * jnp.concatenate([sin] * nb, axis=1)

    def chunked_t(val, ref):
        vt = val.T.astype(ref.dtype)
        for c in range(ref.shape[0]):
            ref[c] = vt[:, c * LANES:(c + 1) * LANES]

    aqT_ref[...] = rotate(plain(SEG_AQ, 256), PAT_HEAD).T.astype(aqT_ref.dtype)

    akv = plain(SEG_AKV, 256)
    lat = _rms(akv[:, :A_KV_RANK], -1) * kvg_ref[...]
    k_rope = rotate(akv[:, A_KV_RANK:], PAT_KROPE)
    akv_ref[...] = jnp.concatenate([lat, k_rope], axis=1).astype(akv_ref.dtype)
    chunked_t(lat, acT_ref)

    iqT_ref[...] = rotate(plain(SEG_IQ, 128), PAT_IDX).T.astype(iqT_ref.dtype)
    misc = rotate(plain(SEG_MISC, 128), PAT_MISC)
    misc_ref[...] = misc
    miscT_ref[...] = misc.T
    u_ref[...] = plain(SEG_BU, 256)
    qk_scale = HEAD_DIM ** -0.5 * LOG2E
    cq = plain(SEG_CQ, 512)
    cqT_ref[...] = (cq * qk_scale).T.astype(cqT_ref.dtype)
    cqrT_ref[...] = (rotate(cq, PAT_HEAD) * qk_scale).T.astype(cqrT_ref.dtype)
    kvc_ref[...] = plain(SEG_KVC, 256)
    ks_ref[...] = rotate(plain(SEG_KS, 128), PAT_HEAD).astype(ks_ref.dtype)
    chunked_t(plain(SEG_VS, 128), vsT_ref)
    kw_ref[...] = rotate(plain(SEG_KW, 128), PAT_HEAD).astype(kw_ref.dtype)
    chunked_t(plain(SEG_VW, 128), vwT_ref)


def _in_proj(x, tab, ln_g, kv_g, w_r, pat):
    bsz, seq, _ = x.shape
    tm = min(PROJ_TM, seq)
    nj = seq // tm
    nck = tm // LANES
    bf = MXU_DTYPE

    def rows(w):
        return pl.BlockSpec((None, tm, w), lambda b, j: (b, j, 0))

    def cols(w):
        return pl.BlockSpec((None, w, tm), lambda b, j: (b, 0, j))

    def chunks():
        return pl.BlockSpec((None, nck, LANES, LANES), lambda b, j: (b, j, 0, 0))

    def const(shape):
        return pl.BlockSpec(shape, lambda b, j: (0,) * len(shape))

    sds = jax.ShapeDtypeStruct
    out_shape = (
        sds((bsz, 256, seq), bf),
        sds((bsz, seq, 256), bf),
        sds((bsz, seq // LANES, LANES, LANES), bf),
        sds((bsz, 128, seq), bf),
        sds((bsz, seq, 128), F32),
        sds((bsz, 128, seq), F32),
        sds((seq, bsz * B_WIDTH), F32),
        sds((bsz, 512, seq), bf),
        sds((bsz, 512, seq), bf),
        sds((bsz, seq, 256), F32),
        sds((bsz, seq, 128), bf),
        sds((bsz, seq // LANES, LANES, LANES), bf),
        sds((bsz, seq, 128), bf),
        sds((bsz, seq // LANES, LANES, LANES), bf),
    )
    out_specs = (cols(256), rows(256), chunks(), cols(128), rows(128), cols(128),
                 pl.BlockSpec((tm, B_WIDTH), lambda b, j: (j, b)),
                 cols(512), cols(512), rows(256), rows(128), chunks(), rows(128), chunks())
    return pl.pallas_call(
        _in_proj_kernel,
        out_shape=out_shape,
        grid=(bsz, nj),
        in_specs=[rows(D_MODEL), rows(LANES), const((1, D_MODEL)), const((1, A_KV_RANK)),
                  const((D_MODEL, N_PROJ)), const((LANES, 2 * N_PAT * LANES))],
        out_specs=out_specs,
        compiler_params=_cparams(("arbitrary", "arbitrary")),
        name="in_proj",
    )(x, tab, ln_g.reshape(1, -1), kv_g.reshape(1, -1), w_r, pat)


def _s5_disc_kernel(lr_ref, li_ref, ls_ref, br_ref, bi_ref, ar_ref, ai_ref, bbr_ref, bbi_ref):
    lr, li = lr_ref[...], li_ref[...]
    step = jnp.exp(ls_ref[...])
    mag = jnp.exp(lr * step)
    ar = mag * jnp.cos(li * step)
    ai = mag * jnp.sin(li * step)
    den = lr * lr + li * li
    zr = ((ar - 1.0) * lr + ai * li) / den
    zi = (ai * lr - (ar - 1.0) * li) / den
    br, bi = br_ref[...], bi_ref[...]
    ar_ref[...] = ar
    ai_ref[...] = ai
    bbr_ref[...] = zr * br - zi * bi
    bbi_ref[...] = zr * bi + zi * br


def _s5_discretize(lam_re, lam_im, log_step, b_re, b_im):
    g, p, h = b_re.shape
    ex = lambda a: jnp.repeat(a.astype(F32), h, axis=1)
    ls = jnp.broadcast_to(log_step.astype(F32)[:, None], (g, p * h))
    sds = jax.ShapeDtypeStruct((g, p * h), F32)
    ar, ai, bbr, bbi = pl.pallas_call(
        _s5_disc_kernel, out_shape=(sds, sds, sds, sds), name="s5_discretize",
    )(ex(lam_re), ex(lam_im), ls, b_re.astype(F32).reshape(g, p * h), b_im.astype(F32).reshape(g, p * h))
    ar = ar.reshape(g, p, h)[:, :, 0]
    ai = ai.reshape(g, p, h)[:, :, 0]
    return ar, ai, bbr.reshape(g, p, h), bbi.reshape(g, p, h)


def _s5_kernel(u_ref, bmat_ref, a_ref, cre_ref, cim_ref, d_ref, gw_ref, gb_ref, gain_ref,
               o_ref, u_scr, x_scr, h_scr, *, tc):
    ns = S5_NSTATE

    @pl.when(pl.program_id(0) == 0)
    def _():
        h_scr[...] = jnp.zeros_like(h_scr)

    halves = range(B_WIDTH // LANES)
    for b in range(SUBLANES):
        for hf in halves:
            c0 = b * B_WIDTH + hf * LANES
            u_scr[hf, pl.ds(b, tc, stride=SUBLANES), :] = u_ref[:, c0:c0 + LANES]
    u = jnp.concatenate([u_scr[hf] for hf in halves], axis=1)
    x_scr[...] = _dot(u, bmat_ref[...])
    ar = a_ref[0:SUBLANES, :]
    ai = a_ref[SUBLANES:2 * SUBLANES, :]

    def step(t, carry):
        hr, hi = carry
        r0 = pl.multiple_of(t * SUBLANES, SUBLANES)
        xr = x_scr[pl.ds(r0, SUBLANES), 0:ns]
        xi = x_scr[pl.ds(r0, SUBLANES), ns:2 * ns]
        nhr = ar * hr - ai * hi + xr
        nhi = ar * hi + ai * hr + xi
        x_scr[pl.ds(r0, SUBLANES), 0:ns] = nhr
        x_scr[pl.ds(r0, SUBLANES), ns:2 * ns] = nhi
        return nhr, nhi

    hr, hi = lax.fori_loop(0, tc, step, (h_scr[:, 0:ns], h_scr[:, ns:2 * ns]), unroll=8)
    h_scr[:, 0:ns] = hr
    h_scr[:, ns:2 * ns] = hi

    y = (_dot(x_scr[:, 0:ns], cre_ref[...]) - _dot(x_scr[:, ns:2 * ns], cim_ref[...])
         + d_ref[...] * u)
    y = jax.nn.gelu(y)
    y = y * jax.nn.sigmoid(_dot(y, gw_ref[...]) + gb_ref[...])
    y = _rms(y, -1) * gain_ref[...]
    for hf in halves:
        u_scr[hf] = y[:, hf * LANES:(hf + 1) * LANES]
    for b in range(SUBLANES):
        for hf in halves:
            c0 = b * B_WIDTH + hf * LANES
            o_ref[:, c0:c0 + LANES] = u_scr[hf, pl.ds(b, tc, stride=SUBLANES), :].astype(o_ref.dtype)


def _s5(u_tm, bsz, ar, ai, bbr, bbi, c_re, c_im, d, glu_w, glu_b, gain):
    assert bsz == SUBLANES, "the S5 scan keeps one batch row per sublane"
    seq = u_tm.shape[0]
    tc = min(S5_TC, seq)
    rows = tc * bsz
    eye = jnp.eye(S5_GROUPS, dtype=F32)
    bmat = jnp.concatenate(
        [jnp.einsum('gph,gk->ghkp', b, eye).reshape(B_WIDTH, S5_NSTATE) for b in (bbr, bbi)], axis=1)
    cre = jnp.einsum('ghp,gk->gpkh', c_re.astype(F32), eye).reshape(S5_NSTATE, B_WIDTH)
    cim = jnp.einsum('ghp,gk->gpkh', c_im.astype(F32), eye).reshape(S5_NSTATE, B_WIDTH)
    avec = jnp.concatenate([jnp.broadcast_to(a.reshape(1, S5_NSTATE), (SUBLANES, S5_NSTATE))
                            for a in (ar, ai)], axis=0)
    const = lambda shape: pl.BlockSpec(shape, lambda i: (0,) * len(shape))
    out = pl.pallas_call(
        functools.partial(_s5_kernel, tc=tc),
        out_shape=jax.ShapeDtypeStruct((seq, bsz * B_WIDTH), MXU_DTYPE),
        grid=(seq // tc,),
        in_specs=[pl.BlockSpec((tc, bsz * B_WIDTH), lambda i: (i, 0)),
                  const((B_WIDTH, 2 * S5_NSTATE)), const((2 * SUBLANES, S5_NSTATE)),
                  const((S5_NSTATE, B_WIDTH)), const((S5_NSTATE, B_WIDTH)),
                  const((1, B_WIDTH)), const((B_WIDTH, B_WIDTH)), const((1, B_WIDTH)),
                  const((1, B_WIDTH))],
        out_specs=pl.BlockSpec((tc, bsz * B_WIDTH), lambda i: (i, 0)),
        scratch_shapes=[pltpu.VMEM((B_WIDTH // LANES, rows, LANES), F32),
                        pltpu.VMEM((rows, 2 * S5_NSTATE), F32),
                        pltpu.VMEM((SUBLANES, 2 * S5_NSTATE), F32)],
        compiler_params=_cparams(("arbitrary",)),
        name="s5_scan",
    )(u_tm, bmat.astype(MXU_DTYPE), avec,
      cre.astype(MXU_DTYPE), cim.astype(MXU_DTYPE), d.astype(F32).reshape(1, B_WIDTH),
      glu_w.astype(MXU_DTYPE), glu_b.astype(F32).reshape(1, B_WIDTH), gain.reshape(1, B_WIDTH))
    return out


def _compress_kernel(k_ref, v_ref, pa_ref, pb_ref, wa_ref, wb_ref, w2_ref, cmp_ref, vT_ref):
    nseg = cmp_ref.shape[0]
    cw = k_ref.shape[1] + v_ref.shape[1]
    xa = jnp.zeros((nseg, cw), F32)
    xb = jnp.zeros((nseg, cw), F32)
    for l in range(CMP_STRIDE):
        rows = pl.ds(l, nseg, stride=CMP_STRIDE)
        tok = jnp.concatenate([k_ref[rows, :], v_ref[rows, :]], axis=1)
        cols = slice(l * cw, (l + 1) * cw)
        xa = xa + _dot(tok + pa_ref[:, cols], wa_ref[cols, :])
        xb = xb + _dot(tok + pb_ref[:, cols], wb_ref[cols, :])
    pre = xa + pltpu.roll(xb, nseg - 1, 0)
    out = _dot(jax.nn.gelu(pre), w2_ref[...])
    cmp_ref[...] = out.astype(cmp_ref.dtype)
    vT_ref[...] = out[:, LANES:].T.astype(vT_ref.dtype)


def _compress(kvc, pos_k, pos_v, k_w1, k_w2, v_w1, v_w2):
    bsz, seq, _ = kvc.shape
    nseg = seq // CMP_STRIDE
    width = CMP_STRIDE * 256
    eye = jnp.eye(4, dtype=F32)
    w1 = jnp.stack([k_w1, k_w1, v_w1, v_w1]).astype(F32).reshape(4, CMP_BLOCK, HEAD_DIM, HEAD_DIM)
    pos = jnp.stack([pos_k, pos_k, pos_v, pos_v]).astype(F32)

    def half(lo):
        w = jnp.einsum('slde,st->lsdte', w1[:, lo:lo + CMP_STRIDE], eye).reshape(width, 256)
        p = jnp.transpose(pos[:, lo:lo + CMP_STRIDE], (1, 0, 2)).reshape(1, width)
        return w.astype(MXU_DTYPE), p

    wa, pa = half(0)
    wb, pb = half(CMP_STRIDE)
    w2 = jnp.einsum('sde,st->sdte', jnp.stack([k_w2, k_w2, v_w2, v_w2]).astype(F32), eye).reshape(256, 256)
    const = lambda shape: pl.BlockSpec(shape, lambda b: (0,) * len(shape))
    return pl.pallas_call(
        _compress_kernel,
        out_shape=(jax.ShapeDtypeStruct((bsz, nseg, 256), MXU_DTYPE),
                   jax.ShapeDtypeStruct((bsz, 128, nseg), MXU_DTYPE)),
        grid=(bsz,),
        in_specs=[pl.BlockSpec((None, seq, LANES), lambda b: (b, 0, 0)),
                  pl.BlockSpec((None, seq, LANES), lambda b: (b, 0, 1)),
                  const((1, width)), const((1, width)), const((width, 256)), const((width, 256)),
                  const((256, 256))],
        out_specs=(pl.BlockSpec((None, nseg, 256), lambda b: (b, 0, 0)),
                   pl.BlockSpec((None, 128, nseg), lambda b: (b, 0, 0))),
        compiler_params=_cparams(("arbitrary",)),
        name="nsa_compress",
    )(kvc, kvc, pa, pb, wa, wb, w2.astype(MXU_DTYPE))


ONES_ROWS = 16
NQ4 = 4 * Q_BLOCK


def _with_ones(v_t):
    return jnp.concatenate([v_t, jnp.ones((ONES_ROWS, v_t.shape[1]), v_t.dtype)], axis=0)


def _online_softmax_step(s, bias, v_ext, m_ref, acc_ref):
    s = s + bias
    m_old = m_ref[...]
    m_new = jnp.maximum(m_old, jnp.max(s, axis=0, keepdims=True))
    p = jnp.exp2(s - m_new)
    acc_ref[...] = jnp.exp2(m_old - m_new) * acc_ref[...] + _dot(v_ext, p)
    m_ref[...] = m_new


def _tile4(x):
    return jnp.concatenate([x, x, x, x], axis=1)


INT_MIN = -2 ** 31
NEG_INF_KEY = int(np.int32(np.uint32(0xFF800000 ^ 0x7FFFFFFF)))


def _dsa_kernel(qT_ref, iqT_ref, miscT_ref, kv_ref, misc_ref, cT_ref, mq_ref, wuv_ref, gain_ref, tri_ref,
                o_ref, score_scr, m_scr, acc_scr, *, k_top, max_chunks):
    i = pl.program_id(1)
    t0 = i * Q_BLOCK
    kc = DSA_KEY_CHUNK
    nch = (t0 + Q_BLOCK + kc - 1) // kc
    t_lane = t0 + lax.broadcasted_iota(I32, (1, Q_BLOCK), 1)
    row_iota = lax.broadcasted_iota(I32, (kc, Q_BLOCK), 0)

    qcat = (_dot(mq_ref[...], qT_ref[...]) * (HEAD_DIM ** -0.5 * LOG2E)).astype(MXU_DTYPE)
    qs_t = jnp.concatenate([qcat[h * 256:(h + 1) * 256] for h in range(A_HEADS)], axis=1)
    iq_t = iqT_ref[...]
    iq_all = jnp.concatenate([iq_t[h * IDX_DIM:(h + 1) * IDX_DIM, :] for h in range(IDX_HEADS)], axis=1)
    w_t = miscT_ref[MISC_IW:MISC_IW + IDX_HEADS, :] * (IDX_HEADS ** -0.5 * IDX_DIM ** -0.5)

    def idx_body(c, diagonal):
        k0 = c * kc
        ik = misc_ref[pl.ds(k0, kc), MISC_IK:MISC_IK + IDX_DIM]
        d = _dot(ik, iq_all)
        score = jnp.zeros((kc, Q_BLOCK), F32)
        for h in range(IDX_HEADS):
            score = score + jnp.maximum(d[:, h * Q_BLOCK:(h + 1) * Q_BLOCK], 0.0) * w_t[h:h + 1, :]
        if diagonal:
            score = jnp.where(row_iota + k0 <= t_lane, score, -jnp.inf)
        score_scr[pl.ds(k0, kc), :] = score

    m_scr[...] = jnp.full_like(m_scr, NEG)
    acc_scr[...] = jnp.zeros_like(acc_scr)

    def passes(n_chunks):
        for c in range(n_chunks):
            idx_body(c, c == n_chunks - 1)

        def as_score(key):
            return pltpu.bitcast(key ^ ((key >> 31) & 0x7FFFFFFF), F32)

        def count(pred):
            acc = jnp.zeros((SUBLANES, Q_BLOCK), I32)
            for c in range(n_chunks):
                hit = pred(score_scr[c * kc:(c + 1) * kc, :])
                acc = acc + jnp.sum(hit.reshape(kc // SUBLANES, SUBLANES, Q_BLOCK), axis=0)
            return jnp.sum(acc, axis=0, keepdims=True)

        def thr_bit(b, prefix):
            cand = prefix | lax.shift_left(jnp.int32(1), 31 - b)
            cand_f = as_score(cand ^ INT_MIN)
            cnt = count(lambda sc: jnp.where(sc >= cand_f, 1, 0))
            return jnp.where(cnt >= k_top, cand, prefix)

        thr_key = jnp.maximum(lax.fori_loop(0, 32, thr_bit, jnp.zeros((1, Q_BLOCK), I32)) ^ INT_MIN,
                              NEG_INF_KEY)
        thr = as_score(thr_key)
        n_gt = count(lambda sc: jnp.where(sc > thr, 1, 0))
        need = jnp.where(thr_key == NEG_INF_KEY, 0, k_top - n_gt).astype(F32)

        n_tied = jnp.zeros((1, Q_BLOCK), F32)
        for c in range(n_chunks):
            keys = score_scr[c * kc:(c + 1) * kc, :]
            tied = _dot(tri_ref[...], jnp.where(keys == thr, 1.0, 0.0)) + n_tied
            bias = jnp.where(keys > thr, 0.0,
                             jnp.where(keys == thr, jnp.where(tied <= need, 0.0, NEG), NEG))
            v_t = jnp.concatenate([cT_ref[c * (kc // LANES) + j] for j in range(kc // LANES)], axis=1)
            s = _dot(kv_ref[c * kc:(c + 1) * kc, :], qs_t)
            _online_softmax_step(s, _tile4(bias), _with_ones(v_t), m_scr, acc_scr)
            n_tied = tied[kc - 1:kc, :]
        return 0

    lax.switch(nch - 1, [functools.partial(passes, n) for n in range(1, max_chunks + 1)])

    acc = acc_scr[...]
    o_lat = (acc[:A_KV_RANK] / acc[A_KV_RANK:A_KV_RANK + 1]).astype(MXU_DTYPE)
    out_t = jnp.concatenate(
        [_dot(wuv_ref[h], o_lat[:, h * Q_BLOCK:(h + 1) * Q_BLOCK]) for h in range(A_HEADS)], axis=0)
    out_t = _rms(out_t, 0) * gain_ref[...]
    o_ref[...] = out_t.T.astype(o_ref.dtype)


def _dsa(aq_t, iq_t, misc_t, akv, misc, ac_t, w_uk, w_uv, gain):
    bsz, _, seq = aq_t.shape
    nq = seq // Q_BLOCK
    k_top = min(DSA_TOPK, seq // 4)
    seq_pad = -(-seq // DSA_KEY_CHUNK) * DSA_KEY_CHUNK
    tri = jnp.asarray(np.tril(np.ones((DSA_KEY_CHUNK, DSA_KEY_CHUNK), np.float32)), MXU_DTYPE)
    mq = jnp.zeros((A_HEADS, 256, A_HEADS, HEAD_DIM), F32)
    for h in range(A_HEADS):
        mq = mq.at[h, :A_KV_RANK, h, ROPE_DIM:].set(w_uk[:, h, :].astype(F32))
        mq = mq.at[h, A_KV_RANK:A_KV_RANK + ROPE_DIM, h, :ROPE_DIM].set(jnp.eye(ROPE_DIM, dtype=F32))
    mq = mq.reshape(A_HEADS * 256, A_HEADS * HEAD_DIM).astype(MXU_DTYPE)
    wuv_t = jnp.transpose(w_uv, (1, 2, 0)).astype(MXU_DTYPE)
    per_q = lambda w: pl.BlockSpec((None, w, Q_BLOCK), lambda b, i: (b, 0, i))
    per_b = lambda *s: pl.BlockSpec((None,) + s, lambda b, i: (b,) + (0,) * len(s))
    const = lambda shape: pl.BlockSpec(shape, lambda b, i: (0,) * len(shape))
    return pl.pallas_call(
        functools.partial(_dsa_kernel, k_top=k_top, max_chunks=seq_pad // DSA_KEY_CHUNK),
        out_shape=jax.ShapeDtypeStruct((bsz, seq, A_HEADS * A_VDIM), MXU_DTYPE),
        grid=(bsz, nq),
        in_specs=[per_q(256), per_q(128), per_q(128), per_b(seq, 256), per_b(seq, 128),
                  per_b(seq // LANES, LANES, LANES), const(mq.shape), const(wuv_t.shape),
                  const((A_HEADS * A_VDIM, 1)), const(tri.shape)],
        out_specs=pl.BlockSpec((None, Q_BLOCK, A_HEADS * A_VDIM), lambda b, i: (b, i, 0)),
        scratch_shapes=[pltpu.VMEM((seq_pad, Q_BLOCK), F32),
                        pltpu.VMEM((1, NQ4), F32),
                        pltpu.VMEM((A_KV_RANK + ONES_ROWS, NQ4), F32)],
        compiler_params=_cparams(("arbitrary", "arbitrary")),
        name="dsa_attention",
    )(aq_t, iq_t, misc_t, akv, misc, ac_t, mq, wuv_t, gain.reshape(-1, 1), tri)


def _nsa_kernel(qT_ref, qrT_ref, miscT_ref, cmp_ref, vcT_ref, ks_ref, vsT_ref, kw_ref, vwT_ref,
                ov_ref, gain_ref, o_ref, sel_scr, m_scr, acc_scr, win_scr, *, n_top, n_blk, max_chunks):
    i = pl.program_id(1)
    t0 = i * Q_BLOCK
    kc = NSA_KEY_CHUNK
    nch = (t0 + Q_BLOCK + kc - 1) // kc
    t_lane = t0 + lax.broadcasted_iota(I32, (1, Q_BLOCK), 1)
    gates = jax.nn.sigmoid(miscT_ref[MISC_GATE:MISC_GATE + 3 * C_HEADS, :])
    n_cmp = cmp_ref.shape[0]
    cmp_iota = lax.broadcasted_iota(I32, (n_cmp, Q_BLOCK), 0)
    blk_iota = lax.broadcasted_iota(I32, (n_blk, Q_BLOCK), 0)
    row_iota = lax.broadcasted_iota(I32, (kc, Q_BLOCK), 0)
    win_iota = lax.broadcasted_iota(I32, (Q_BLOCK, Q_BLOCK), 0)
    groups = range(C_KV_HEADS)
    gsl = [slice(g * HEAD_DIM, (g + 1) * HEAD_DIM) for g in groups]

    def heads_t(ref, g):
        return jnp.concatenate([ref[h * HEAD_DIM:(h + 1) * HEAD_DIM, :]
                                for h in range(g * C_GROUP, (g + 1) * C_GROUP)], axis=1)

    o_c = []
    for g in groups:
        s_c = _dot(cmp_ref[:, gsl[g]], heads_t(qT_ref, g))
        vis = _tile4(jnp.where(cmp_iota * CMP_STRIDE + (CMP_BLOCK - 1) <= t_lane, 1, 0)) > 0
        s_c = jnp.where(vis, s_c, NEG)
        p_c = jnp.where(vis, jnp.exp2(s_c - jnp.max(s_c, axis=0, keepdims=True)), 0.0)
        l_c = jnp.sum(p_c, axis=0, keepdims=True)
        p_c = p_c * (1.0 / jnp.maximum(l_c, 1e-30))
        o_c.append(_dot(vcT_ref[gsl[g], :], p_c))

        p_sum = (p_c[:, 0:Q_BLOCK] + p_c[:, Q_BLOCK:2 * Q_BLOCK]
                 + p_c[:, 2 * Q_BLOCK:3 * Q_BLOCK] + p_c[:, 3 * Q_BLOCK:4 * Q_BLOCK])
        p_hi = p_sum.astype(MXU_DTYPE)
        p_lo = p_sum - p_hi.astype(F32)
        imp = _dot(ov_ref[...], p_hi) + _dot(ov_ref[...], p_lo)
        cur = t_lane // SEL_BLOCK
        forced = jnp.where(blk_iota == 0, 1, jnp.where(blk_iota == cur, 1,
                           jnp.where(blk_iota == cur - 1, 1, 0)))
        imp = jnp.where(forced > 0, SEL_FORCE, imp)
        imp = jnp.where(blk_iota * SEL_BLOCK <= t_lane, imp, -jnp.inf)
        n_grp = n_blk // SUBLANES
        imp_g = [imp[v * SUBLANES:(v + 1) * SUBLANES, :] for v in range(n_grp)]
        rank_g = [jnp.zeros((SUBLANES, Q_BLOCK), I32) for _ in range(n_grp)]
        for mp in range(n_blk):
            row = imp[mp:mp + 1, :]
            for v in range(n_grp):
                if v > mp // SUBLANES:
                    beats = jnp.where(row >= imp_g[v], 1, 0)
                elif v < mp // SUBLANES:
                    beats = jnp.where(row > imp_g[v], 1, 0)
                else:
                    later = blk_iota[v * SUBLANES:(v + 1) * SUBLANES, :] > mp
                    beats = jnp.where(row > imp_g[v], 1, jnp.where(row == imp_g[v], jnp.where(later, 1, 0), 0))
                rank_g[v] = rank_g[v] + beats
        rank = jnp.concatenate(rank_g, axis=0)
        sel_scr[g, 0:n_blk, :] = jnp.where(rank < n_top, 0.0, NEG)

    m_scr[...] = jnp.full_like(m_scr, NEG)
    acc_scr[...] = jnp.zeros_like(acc_scr)
    bpc = kc // SEL_BLOCK

    def sel_body(c, diagonal):
        k0 = c * kc
        for g in groups:
            s = _dot(ks_ref[pl.ds(k0, kc), gsl[g]], heads_t(qrT_ref, g))
            sel8 = sel_scr[g, pl.ds(c * bpc, bpc), :]
            bias = jnp.concatenate(
                [jnp.broadcast_to(sel8[j:j + 1, :], (SEL_BLOCK, Q_BLOCK)) for j in range(bpc)], axis=0)
            if diagonal:
                bias = jnp.where(row_iota + k0 <= t_lane, bias, NEG)
            v_t = jnp.concatenate([vsT_ref[c * (kc // LANES) + j][gsl[g], :]
                                   for j in range(kc // LANES)], axis=1)
            _online_softmax_step(s, _tile4(bias), _with_ones(v_t), m_scr.at[g], acc_scr.at[g])

    def window_branch(g):
        k_parts, v_parts, m_parts = [], [], []
        for j in range(WINDOW // Q_BLOCK + 1):
            cj = i - WINDOW // Q_BLOCK + j
            cjc = jnp.maximum(cj, 0)
            k0 = pl.multiple_of(cjc * Q_BLOCK, Q_BLOCK)
            k_parts.append(kw_ref[pl.ds(k0, Q_BLOCK), gsl[g]])
            v_parts.append(vwT_ref[cjc][gsl[g], :])
            kidx = win_iota + k0
            inside = jnp.where(kidx <= t_lane, jnp.where(kidx > t_lane - WINDOW, 0.0, NEG), NEG)
            m_parts.append(jnp.where(cj >= 0, inside, NEG))
        s_w = (_dot(jnp.concatenate(k_parts, axis=0), heads_t(qrT_ref, g))
               + _tile4(jnp.concatenate(m_parts, axis=0)))
        p_w = jnp.exp2(s_w - jnp.max(s_w, axis=0, keepdims=True))
        acc = _dot(_with_ones(jnp.concatenate(v_parts, axis=1)), p_w)
        win_scr[g] = acc[:HEAD_DIM] / acc[HEAD_DIM:HEAD_DIM + 1]

    def sweep(n_chunks):
        for g in groups:
            window_branch(g)
        for c in range(n_chunks):
            sel_body(c, c == n_chunks - 1)
        return 0

    lax.switch(nch - 1, [functools.partial(sweep, n) for n in range(1, max_chunks + 1)])

    slabs = []
    for g in groups:
        acc = acc_scr[g]
        o_s = acc[:HEAD_DIM] / acc[HEAD_DIM:HEAD_DIM + 1]
        o_w = win_scr[g]
        for hh in range(C_GROUP):
            sl = slice(hh * Q_BLOCK, (hh + 1) * Q_BLOCK)
            r = (g * C_GROUP + hh) * 3
            slabs.append(gates[r:r + 1, :] * o_c[g][:, sl] + gates[r + 1:r + 2, :] * o_s[:, sl]
                         + gates[r + 2:r + 3, :] * o_w[:, sl])

    out_t = jnp.concatenate(slabs, axis=0)
    out_t = _rms(out_t, 0) * gain_ref[...]
    o_ref[...] = out_t.T.astype(o_ref.dtype)


def _sel_overlap_t(n_cmp_rows, n_blk):
    cs = np.arange(n_cmp_rows)[None, :] * CMP_STRIDE
    ss = np.arange(n_blk)[:, None] * SEL_BLOCK
    ov = np.minimum(cs + CMP_BLOCK, ss + SEL_BLOCK) - np.maximum(cs, ss)
    return np.clip(ov, 0, None).astype(np.float32) / CMP_BLOCK


def _nsa(cq_t, cqr_t, misc_t, cmp, vcmp_t, ks, vs_t, kw, vw_t, gain):
    bsz, _, seq = cq_t.shape
    nq = seq // Q_BLOCK
    nseg = cmp.shape[1]
    n_blk = seq // SEL_BLOCK
    n_top = min(SEL_TOPN, n_blk)
    seq_pad = -(-seq // NSA_KEY_CHUNK) * NSA_KEY_CHUNK
    ov_np = _sel_overlap_t(nseg, n_blk)
    ov_np[:, (seq - CMP_BLOCK) // CMP_STRIDE + 1:] = 0.0
    ov = jnp.asarray(ov_np, MXU_DTYPE)
    per_q = lambda w: pl.BlockSpec((None, w, Q_BLOCK), lambda b, i: (b, 0, i))
    per_b = lambda *s: pl.BlockSpec((None,) + s, lambda b, i: (b,) + (0,) * len(s))
    const = lambda shape: pl.BlockSpec(shape, lambda b, i: (0,) * len(shape))
    return pl.pallas_call(
        functools.partial(_nsa_kernel, n_top=n_top, n_blk=n_blk, max_chunks=seq_pad // NSA_KEY_CHUNK),
        out_shape=jax.ShapeDtypeStruct((bsz, seq, C_HEADS * HEAD_DIM), MXU_DTYPE),
        grid=(bsz, nq),
        in_specs=[per_q(512), per_q(512), per_q(128), per_b(nseg, 256), per_b(128, nseg),
                  per_b(seq, 128), per_b(seq // LANES, LANES, LANES),
                  per_b(seq, 128), per_b(seq // LANES, LANES, LANES),
                  const(ov.shape), const((C_HEADS * HEAD_DIM, 1))],
        out_specs=pl.BlockSpec((None, Q_BLOCK, C_HEADS * HEAD_DIM), lambda b, i: (b, i, 0)),
        scratch_shapes=[pltpu.VMEM((C_KV_HEADS, max(n_blk, seq_pad // SEL_BLOCK), Q_BLOCK), F32),
                        pltpu.VMEM((C_KV_HEADS, 1, NQ4), F32),
                        pltpu.VMEM((C_KV_HEADS, HEAD_DIM + ONES_ROWS, NQ4), F32),
                        pltpu.VMEM((C_KV_HEADS, HEAD_DIM, NQ4), F32)],
        compiler_params=_cparams(("arbitrary", "arbitrary")),
        name="nsa_attention",
    )(cq_t, cqr_t, misc_t, cmp, vcmp_t, ks, vs_t, kw, vw_t, ov, gain.reshape(-1, 1))


def _out_mlp_kernel(x_ref, a_ref, b_ref, c_ref, wo_ref, g2_ref, wu_ref, wd_ref, fg_ref,
                    o_ref, x1_scr, n2_scr, acc_scr, *, final_norm):
    j = pl.program_id(1)

    @pl.when(j == 0)
    def _():
        mixed = (jnp.dot(a_ref[...], wo_ref[0:256, :], preferred_element_type=F32)
                 + jnp.dot(b_ref[...], wo_ref[256:512, :], preferred_element_type=F32)
                 + jnp.dot(c_ref[...], wo_ref[512:1024, :], preferred_element_type=F32))
        x1 = x_ref[...] + mixed
        x1_scr[...] = x1
        n2_scr[...] = (_rms(x1, -1) * g2_ref[...]).astype(n2_scr.dtype)
        acc_scr[...] = jnp.zeros_like(acc_scr)

    z = jnp.dot(n2_scr[...], wu_ref[...], preferred_element_type=F32)
    acc_scr[...] += _dot(jnp.square(jnp.maximum(z, 0.0)), wd_ref[...])

    @pl.when(j == pl.num_programs(1) - 1)
    def _():
        x2 = x1_scr[...] + acc_scr[...]
        if final_norm:
            x2 = _rms(x2, -1) * fg_ref[...]
        o_ref[...] = x2


def _out_mlp(x, a_n, b_tm, c_n, w_out, ln2_g, w_up, w_down, final_g, final_norm):
    bsz, seq, _ = x.shape
    tm = min(MLP_TM, seq)
    nj = seq // tm
    nf = D_FF // MLP_TF
    rows = lambda w: pl.BlockSpec((None, tm, w), lambda r, f: (r // nj, r % nj, 0))
    const = lambda shape: pl.BlockSpec(shape, lambda r, f: (0,) * len(shape))
    return pl.pallas_call(
        functools.partial(_out_mlp_kernel, final_norm=final_norm),
        out_shape=jax.ShapeDtypeStruct(x.shape, F32),
        grid=(bsz * nj, nf),
        in_specs=[rows(D_MODEL), rows(256),
                  pl.BlockSpec((tm, B_WIDTH), lambda r, f: (r % nj, r // nj)),
                  rows(512), const((D_MODEL, D_MODEL)), const((1, D_MODEL)),
                  pl.BlockSpec((D_MODEL, MLP_TF), lambda r, f: (0, f)),
                  pl.BlockSpec((MLP_TF, D_MODEL), lambda r, f: (f, 0)),
                  const((1, D_MODEL))],
        out_specs=rows(D_MODEL),
        scratch_shapes=[pltpu.VMEM((tm, D_MODEL), F32), pltpu.VMEM((tm, D_MODEL), MXU_DTYPE),
                        pltpu.VMEM((tm, D_MODEL), F32)],
        compiler_params=_cparams(("arbitrary", "arbitrary")),
        name="out_proj_mlp",
    )(x, a_n, b_tm, c_n, w_out.astype(MXU_DTYPE), ln2_g.reshape(1, -1), w_up.astype(MXU_DTYPE),
      w_down.astype(MXU_DTYPE), final_g.reshape(1, -1))


def kernel(x, positions, ln1_g, w_in, kv_norm_g, w_uk, w_uv, s5_lambda_re, s5_lambda_im, s5_log_step, s5_b_re, s5_b_im, s5_c_re, s5_c_im, s5_d, s5_glu_w, s5_glu_b, cmp_pos_k, cmp_pos_v, cmp_k_w1, cmp_k_w2, cmp_v_w1, cmp_v_w2, gain_a, gain_b, gain_c, w_out, ln2_g, w_up, w_down, final_g):
    bsz, seq, _ = x.shape
    depth = w_in.shape[0]
    src, pat_np = _proj_layout()
    pat = jnp.asarray(pat_np, jnp.bfloat16)
    tab = _rope_table(positions)

    def regather(w, idx):
        cols = jnp.take(w, jnp.asarray(np.maximum(idx, 0)), axis=1)
        return jnp.where(jnp.asarray(idx >= 0)[None, :], cols, 0.0).astype(MXU_DTYPE)

    for layer in range(depth):
        w_r = regather(w_in[layer], src)
        (aq_t, akv, ac_t, iq_t, misc, misc_t, u_tm, cq_t, cqr_t, kvc, ks, vs_t, kw, vw_t) = _in_proj(
            x, tab, ln1_g[layer], kv_norm_g[layer], w_r, pat)

        ar, ai, bbr, bbi = _s5_discretize(s5_lambda_re[layer], s5_lambda_im[layer], s5_log_step[layer],
                                          s5_b_re[layer], s5_b_im[layer])
        b_tm = _s5(u_tm, bsz, ar, ai, bbr, bbi, s5_c_re[layer], s5_c_im[layer], s5_d[layer].reshape(-1),
                   s5_glu_w[layer], s5_glu_b[layer], gain_b[layer])

        cmp, vcmp_t = _compress(kvc, cmp_pos_k[layer], cmp_pos_v[layer], cmp_k_w1[layer],
                                cmp_k_w2[layer], cmp_v_w1[layer], cmp_v_w2[layer])
        a_n = _dsa(aq_t, iq_t, misc_t, akv, misc, ac_t, w_uk[layer], w_uv[layer], gain_a[layer])
        c_n = _nsa(cq_t, cqr_t, misc_t, cmp, vcmp_t, ks, vs_t, kw, vw_t, gain_c[layer])

        x = _out_mlp(x, a_n, b_tm, c_n, w_out[layer], ln2_g[layer], w_up[layer], w_down[layer],
                     final_g, final_norm=(layer == depth - 1))
    return x
```

```python
import functools
import math

import numpy as np
import jax
import jax.numpy as jnp
from jax import lax
from jax.experimental import pallas as pl
from jax.experimental.pallas import tpu as pltpu

F32 = jnp.float32
I32 = jnp.int32
MXU_DTYPE = jnp.bfloat16

D_MODEL = 1024
HEAD_DIM = 64
ROPE_THETA = 500000.0
ROPE_DIM = HEAD_DIM // 4
NORM_EPS = 1e-6
Q_BLOCK = 128
NEG = -1e30
LOG2E = math.log2(math.e)
D_FF = 4 * D_MODEL

A_HEADS = 4
A_NOPE = HEAD_DIM - ROPE_DIM
A_VDIM = HEAD_DIM
A_KV_RANK = 128
IDX_HEADS = 4
IDX_DIM = 32
IDX_ROPE = IDX_DIM // 4
DSA_TOPK = 256

B_WIDTH = 256
S5_GROUP = 16
S5_GROUPS = B_WIDTH // S5_GROUP
S5_STATE = 64
S5_NSTATE = S5_GROUPS * S5_STATE

C_HEADS = 8
C_KV_HEADS = 2
C_GROUP = C_HEADS // C_KV_HEADS
CMP_BLOCK = 32
CMP_STRIDE = 16
SEL_BLOCK = 64
SEL_TOPN = 16
SEL_FORCE = 1e9
WINDOW = 512

IN_SIZES = (A_HEADS * HEAD_DIM, A_KV_RANK, ROPE_DIM, IDX_HEADS * IDX_DIM, IDX_DIM, IDX_HEADS,
            B_WIDTH, C_HEADS * HEAD_DIM, 6 * C_KV_HEADS * HEAD_DIM, 3 * C_HEADS)

LANES = 128
SUBLANES = 8
VMEM_LIMIT = 56 * 1024 * 1024

SEG_AQ, SEG_AKV, SEG_IQ, SEG_MISC, SEG_BU, SEG_CQ, SEG_KVC, SEG_KS, SEG_VS, SEG_KW, SEG_VW = (
    0, 256, 512, 640, 768, 1024, 1536, 1792, 1920, 2048, 2176)
N_PROJ = 2304
MISC_IK, MISC_IW, MISC_GATE = 0, 32, 36
TAB_COS, TAB_SIN, TAB_ICOS, TAB_ISIN, TAB_ONE = 0, 8, 16, 20, 24
TAB_PART = 32

PROJ_TM = 512
DSA_KEY_CHUNK = 512
NSA_KEY_CHUNK = 1024
S5_TC = 128
MLP_TM = 1024
MLP_TF = 1024


def _cparams(sem):
    return pltpu.CompilerParams(dimension_semantics=sem, vmem_limit_bytes=VMEM_LIMIT)


def _dot(a, b):
    return jnp.dot(a.astype(MXU_DTYPE), b.astype(MXU_DTYPE), preferred_element_type=F32)


def _rms(x, axis):
    return x * lax.rsqrt(jnp.mean(x * x, axis=axis, keepdims=True) + NORM_EPS)


def _rope_tab_kernel(pos_ref, freq_ref, out_ref):
    ang = pos_ref[...].astype(F32) * freq_ref[...]
    lane = lax.broadcasted_iota(I32, ang.shape, 1)
    l32 = lane % TAB_PART
    is_cos = (l32 < TAB_SIN) | ((l32 >= TAB_ICOS) & (l32 < TAB_ISIN))
    is_sin = ((l32 >= TAB_SIN) & (l32 < TAB_ICOS)) | ((l32 >= TAB_ISIN) & (l32 < TAB_ONE))
    val = jnp.where(is_cos, jnp.cos(ang),
                    jnp.where(is_sin, jnp.sin(ang), jnp.where(l32 == TAB_ONE, 1.0, 0.0)))
    hi = val.astype(jnp.bfloat16).astype(F32)
    r1 = val - hi
    mid = r1.astype(jnp.bfloat16).astype(F32)
    lo = r1 - mid
    part = lane // TAB_PART
    out = jnp.where(part == 0, hi, jnp.where(part == 1, mid, jnp.where(part == 2, lo, 0.0)))
    out_ref[...] = out.astype(jnp.bfloat16)


def _rope_table(positions):
    bsz, seq = positions.shape
    t = bsz * seq
    inv_r = (np.float32(ROPE_THETA) ** (-np.arange(0, ROPE_DIM, 2, dtype=np.float32) / ROPE_DIM))
    inv_i = (np.float32(ROPE_THETA) ** (-np.arange(0, IDX_ROPE, 2, dtype=np.float32) / IDX_ROPE))
    f32 = np.zeros(TAB_PART, np.float32)
    f32[TAB_COS:TAB_COS + 8] = inv_r
    f32[TAB_SIN:TAB_SIN + 8] = inv_r
    f32[TAB_ICOS:TAB_ICOS + 4] = inv_i
    f32[TAB_ISIN:TAB_ISIN + 4] = inv_i
    freq = jnp.asarray(np.tile(f32, LANES // TAB_PART)[None, :])
    tq = min(1024, t)
    return pl.pallas_call(
        _rope_tab_kernel,
        out_shape=jax.ShapeDtypeStruct((t, LANES), jnp.bfloat16),
        grid=(t // tq,),
        in_specs=[pl.BlockSpec((tq, 1), lambda i: (i, 0)),
                  pl.BlockSpec((1, LANES), lambda i: (0, 0))],
        out_specs=pl.BlockSpec((tq, LANES), lambda i: (i, 0)),
        compiler_params=_cparams(("arbitrary",)),
        name="rope_table",
    )(positions.reshape(t, 1), freq).reshape(bsz, seq, LANES)


PAT_HEAD, PAT_KROPE, PAT_IDX, PAT_MISC = range(4)
N_PAT = 4
PAT_GEOMETRY = {PAT_HEAD: (ROPE_DIM // 2, HEAD_DIM), PAT_KROPE: (ROPE_DIM // 2, LANES),
                PAT_IDX: (IDX_ROPE // 2, IDX_DIM), PAT_MISC: (IDX_ROPE // 2, LANES)}


def _proj_layout():
    offs = np.concatenate([[0], np.cumsum(IN_SIZES)])
    o_aq, o_ckv, o_kr, o_iq, o_ik, o_iw, o_bu, o_cq, o_kv, o_gate = offs[:10]
    src = -np.ones(N_PROJ, np.int64)
    clane = np.full(N_PROJ, TAB_ONE, np.int64)
    slane = -np.ones(N_PROJ, np.int64)
    ssign = np.zeros(N_PROJ, np.float32)

    def plain(c0, o0, w):
        src[c0:c0 + w] = np.arange(o0, o0 + w)

    def rope(c0, o0, half, cos_lane, sin_lane):
        for j in range(half):
            clane[c0 + j] = clane[c0 + half + j] = cos_lane + j
            slane[c0 + j] = slane[c0 + half + j] = sin_lane + j
            ssign[c0 + j] = -1.0
            ssign[c0 + half + j] = 1.0

    plain(SEG_AQ, o_aq, A_HEADS * HEAD_DIM)
    for h in range(A_HEADS):
        rope(SEG_AQ + h * HEAD_DIM, o_aq + h * HEAD_DIM, ROPE_DIM // 2, TAB_COS, TAB_SIN)
    plain(SEG_AKV, o_ckv, A_KV_RANK)
    plain(SEG_AKV + A_KV_RANK, o_kr, ROPE_DIM)
    rope(SEG_AKV + A_KV_RANK, o_kr, ROPE_DIM // 2, TAB_COS, TAB_SIN)
    plain(SEG_IQ, o_iq, IDX_HEADS * IDX_DIM)
    for h in range(IDX_HEADS):
        rope(SEG_IQ + h * IDX_DIM, o_iq + h * IDX_DIM, IDX_ROPE // 2, TAB_ICOS, TAB_ISIN)
    plain(SEG_MISC + MISC_IK, o_ik, IDX_DIM)
    rope(SEG_MISC + MISC_IK, o_ik, IDX_ROPE // 2, TAB_ICOS, TAB_ISIN)
    plain(SEG_MISC + MISC_IW, o_iw, IDX_HEADS)
    plain(SEG_MISC + MISC_GATE, o_gate, 3 * C_HEADS)
    plain(SEG_BU, o_bu, B_WIDTH)
    plain(SEG_CQ, o_cq, C_HEADS * HEAD_DIM)
    for h in range(C_HEADS):
        rope(SEG_CQ + h * HEAD_DIM, o_cq + h * HEAD_DIM, ROPE_DIM // 2, TAB_COS, TAB_SIN)
    plain(SEG_KVC, o_kv, 6 * C_KV_HEADS * HEAD_DIM)
    for seg, sub in ((SEG_KS, 2), (SEG_KW, 4)):
        for g in range(C_KV_HEADS):
            rope(seg + g * HEAD_DIM, o_kv + sub * C_KV_HEADS * HEAD_DIM + g * HEAD_DIM,
                 ROPE_DIM // 2, TAB_COS, TAB_SIN)

    k = np.arange(LANES)[:, None]
    live = k < 3 * TAB_PART
    ec = (live & ((k % TAB_PART) == clane[None, :])).astype(np.float32)
    es = (live & ((k % TAB_PART) == slane[None, :])).astype(np.float32) * ssign[None, :]
    starts = {PAT_HEAD: SEG_AQ, PAT_KROPE: SEG_AKV + A_KV_RANK, PAT_IDX: SEG_IQ, PAT_MISC: SEG_MISC}
    blocks = [m[:, starts[p]:starts[p] + LANES] for m in (ec, es) for p in range(N_PAT)]
    return src, np.concatenate(blocks, axis=1)


def _in_proj_kernel(x_ref, tab_ref, g_ref, kvg_ref, w_ref, pat_ref,
                    aqT_ref, akv_ref, acT_ref, iqT_ref, misc_ref, miscT_ref, u_ref,
                    cqT_ref, cqrT_ref, kvc_ref, ks_ref, vsT_ref, kw_ref, vwT_ref):
    x = x_ref[...]
    n = (_rms(x, -1) * g_ref[...]).astype(MXU_DTYPE)
    cs = jnp.dot(tab_ref[...], pat_ref[...], preferred_element_type=F32)

    def plain(c0, w):
        return jnp.dot(n, w_ref[:, c0:c0 + w], preferred_element_type=F32)

    def rotate(p, pat):
        w = p.shape[1]
        nb = w // LANES
        half, period = PAT_GEOMETRY[pat]
        lane = lax.broadcasted_iota(I32, (1, w), 1)
        first = (lane % period) < half
        partner = jnp.where(first, pltpu.roll(p, w - half, 1), pltpu.roll(p, half, 1))
        cos = cs[:, pat * LANES:(pat + 1) * LANES]
        sin = cs[:, (N_PAT + pat) * LANES:(N_PAT + pat + 1) * LANES]
        return p * jnp.concatenate([cos] * nb, axis=1) + partner * jnp.concatenate([sin] * nb, axis=1)

    def chunked_t(val, ref):
        vt = val.T.astype(ref.dtype)
        for c in range(ref.shape[0]):
            ref[c] = vt[:, c * LANES:(c + 1) * LANES]

    aqT_ref[...] = rotate(plain(SEG_AQ, 256), PAT_HEAD).T.astype(aqT_ref.dtype)

    akv = plain(SEG_AKV, 256)
    lat = _rms(akv[:, :A_KV_RANK], -1) * kvg_ref[...]
    k_rope = rotate(akv[:, A_KV_RANK:], PAT_KROPE)
    akv_ref[...] = jnp.concatenate([lat, k_rope], axis=1).astype(akv_ref.dtype)
    chunked_t(lat, acT_ref)

    iqT_ref[...] = rotate(plain(SEG_IQ, 128), PAT_IDX).T.astype(iqT_ref.dtype)
    misc = rotate(plain(SEG_MISC, 128), PAT_MISC)
    misc_ref[...] = misc
    miscT_ref[...] = misc.T
    u_ref[...] = plain(SEG_BU, 256)
    qk_scale = HEAD_DIM ** -0.5 * LOG2E
    cq = plain(SEG_CQ, 512)
    cqT_ref[...] = (cq * qk_scale).T.astype(cqT_ref.dtype)
    cqrT_ref[...] = (rotate(cq, PAT_HEAD) * qk_scale).T.astype(cqrT_ref.dtype)
    kvc_ref[...] = plain(SEG_KVC, 256)
    ks_ref[...] = rotate(plain(SEG_KS, 128), PAT_HEAD).astype(ks_ref.dtype)
    chunked_t(plain(SEG_VS, 128), vsT_ref)
    kw_ref[...] = rotate(plain(SEG_KW, 128), PAT_HEAD).astype(kw_ref.dtype)
    chunked_t(plain(SEG_VW, 128), vwT_ref)


def _in_proj(x, tab, ln_g, kv_g, w_r, pat):
    bsz, seq, _ = x.shape
    tm = min(PROJ_TM, seq)
    nj = seq // tm
    nck = tm // LANES
    bf = MXU_DTYPE

    def rows(w):
        return pl.BlockSpec((None, tm, w), lambda b, j: (b, j, 0))

    def cols(w):
        return pl.BlockSpec((None, w, tm), lambda b, j: (b, 0, j))

    def chunks():
        return pl.BlockSpec((None, nck, LANES, LANES), lambda b, j: (b, j, 0, 0))

    def const(shape):
        return pl.BlockSpec(shape, lambda b, j: (0,) * len(shape))

    sds = jax.ShapeDtypeStruct
    out_shape = (
        sds((bsz, 256, seq), bf),
        sds((bsz, seq, 256), bf),
        sds((bsz, seq // LANES, LANES, LANES), bf),
        sds((bsz, 128, seq), bf),
        sds((bsz, seq, 128), F32),
        sds((bsz, 128, seq), F32),
        sds((seq, bsz * B_WIDTH), F32),
        sds((bsz, 512, seq), bf),
        sds((bsz, 512, seq), bf),
        sds((bsz, seq, 256), F32),
        sds((bsz, seq, 128), bf),
        sds((bsz, seq // LANES, LANES, LANES), bf),
        sds((bsz, seq, 128), bf),
        sds((bsz, seq // LANES, LANES, LANES), bf),
    )
    out_specs = (cols(256), rows(256), chunks(), cols(128), rows(128), cols(128),
                 pl.BlockSpec((tm, B_WIDTH), lambda b, j: (j, b)),
                 cols(512), cols(512), rows(256), rows(128), chunks(), rows(128), chunks())
    return pl.pallas_call(
        _in_proj_kernel,
        out_shape=out_shape,
        grid=(bsz, nj),
        in_specs=[rows(D_MODEL), rows(LANES), const((1, D_MODEL)), const((1, A_KV_RANK)),
                  const((D_MODEL, N_PROJ)), const((LANES, 2 * N_PAT * LANES))],
        out_specs=out_specs,
        compiler_params=_cparams(("arbitrary", "arbitrary")),
        name="in_proj",
    )(x, tab, ln_g.reshape(1, -1), kv_g.reshape(1, -1), w_r, pat)


def _s5_disc_kernel(lr_ref, li_ref, ls_ref, br_ref, bi_ref, ar_ref, ai_ref, bbr_ref, bbi_ref):
    lr, li = lr_ref[...], li_ref[...]
    step = jnp.exp(ls_ref[...])
    mag = jnp.exp(lr * step)
    ar = mag * jnp.cos(li * step)
    ai = mag * jnp.sin(li * step)
    den = lr * lr + li * li
    zr = ((ar - 1.0) * lr + ai * li) / den
    zi = (ai * lr - (ar - 1.0) * li) / den
    br, bi = br_ref[...], bi_ref[...]
    ar_ref[...] = ar
    ai_ref[...] = ai
    bbr_ref[...] = zr * br - zi * bi
    bbi_ref[...] = zr * bi + zi * br


def _s5_discretize(lam_re, lam_im, log_step, b_re, b_im):
    g, p, h = b_re.shape
    ex = lambda a: jnp.repeat(a.astype(F32), h, axis=1)
    ls = jnp.broadcast_to(log_step.astype(F32)[:, None], (g, p * h))
    sds = jax.ShapeDtypeStruct((g, p * h), F32)
    ar, ai, bbr, bbi = pl.pallas_call(
        _s5_disc_kernel, out_shape=(sds, sds, sds, sds), name="s5_discretize",
    )(ex(lam_re), ex(lam_im), ls, b_re.astype(F32).reshape(g, p * h), b_im.astype(F32).reshape(g, p * h))
    ar = ar.reshape(g, p, h)[:, :, 0]
    ai = ai.reshape(g, p, h)[:, :, 0]
    return ar, ai, bbr.reshape(g, p, h), bbi.reshape(g, p, h)


def _s5_kernel(u_ref, bmat_ref, a_ref, cre_ref, cim_ref, d_ref, gw_ref, gb_ref, gain_ref,
               o_ref, u_scr, x_scr, h_scr, *, tc):
    ns = S5_NSTATE

    @pl.when(pl.program_id(0) == 0)
    def _():
        h_scr[...] = jnp.zeros_like(h_scr)

    halves = range(B_WIDTH // LANES)
    for b in range(SUBLANES):
        for hf in halves:
            c0 = b * B_WIDTH + hf * LANES
            u_scr[hf, pl.ds(b, tc, stride=SUBLANES), :] = u_ref[:, c0:c0 + LANES]
    u = jnp.concatenate([u_scr[hf] for hf in halves], axis=1)
    x_scr[...] = _dot(u, bmat_ref[...])
    ar = a_ref[0:SUBLANES, :]
    ai = a_ref[SUBLANES:2 * SUBLANES, :]

    def step(t, carry):
        hr, hi = carry
        r0 = pl.multiple_of(t * SUBLANES, SUBLANES)
        xr = x_scr[pl.ds(r0, SUBLANES), 0:ns]
        xi = x_scr[pl.ds(r0, SUBLANES), ns:2 * ns]
        nhr = ar * hr - ai * hi + xr
        nhi = ar * hi + ai * hr + xi
        x_scr[pl.ds(r0, SUBLANES), 0:ns] = nhr
        x_scr[pl.ds(r0, SUBLANES), ns:2 * ns] = nhi
        return nhr, nhi

    hr, hi = lax.fori_loop(0, tc, step, (h_scr[:, 0:ns], h_scr[:, ns:2 * ns]), unroll=8)
    h_scr[:, 0:ns] = hr
    h_scr[:, ns:2 * ns] = hi

    y = (_dot(x_scr[:, 0:ns], cre_ref[...]) - _dot(x_scr[:, ns:2 * ns], cim_ref[...])
         + d_ref[...] * u)
    y = jax.nn.gelu(y)
    y = y * jax.nn.sigmoid(_dot(y, gw_ref[...]) + gb_ref[...])
    y = _rms(y, -1) * gain_ref[...]
    for hf in halves:
        u_scr[hf] = y[:, hf * LANES:(hf + 1) * LANES]
    for b in range(SUBLANES):
        for hf in halves:
            c0 = b * B_WIDTH + hf * LANES
            o_ref[:, c0:c0 + LANES] = u_scr[hf, pl.ds(b, tc, stride=SUBLANES), :].astype(o_ref.dtype)


def _s5(u_tm, bsz, ar, ai, bbr, bbi, c_re, c_im, d, glu_w, glu_b, gain):
    assert bsz == SUBLANES, "the S5 scan keeps one batch row per sublane"
    seq = u_tm.shape[0]
    tc = min(S5_TC, seq)
    rows = tc * bsz
    eye = jnp.eye(S5_GROUPS, dtype=F32)
    bmat = jnp.concatenate(
        [jnp.einsum('gph,gk->ghkp', b, eye).reshape(B_WIDTH, S5_NSTATE) for b in (bbr, bbi)], axis=1)
    cre = jnp.einsum('ghp,gk->gpkh', c_re.astype(F32), eye).reshape(S5_NSTATE, B_WIDTH)
    cim = jnp.einsum('ghp,gk->gpkh', c_im.astype(F32), eye).reshape(S5_NSTATE, B_WIDTH)
    avec = jnp.concatenate([jnp.broadcast_to(a.reshape(1, S5_NSTATE), (SUBLANES, S5_NSTATE))
                            for a in (ar, ai)], axis=0)
    const = lambda shape: pl.BlockSpec(shape, lambda i: (0,) * len(shape))
    out = pl.pallas_call(
        functools.partial(_s5_kernel, tc=tc),
        out_shape=jax.ShapeDtypeStruct((seq, bsz * B_WIDTH), MXU_DTYPE),
        grid=(seq // tc,),
        in_specs=[pl.BlockSpec((tc, bsz * B_WIDTH), lambda i: (i, 0)),
                  const((B_WIDTH, 2 * S5_NSTATE)), const((2 * SUBLANES, S5_NSTATE)),
                  const((S5_NSTATE, B_WIDTH)), const((S5_NSTATE, B_WIDTH)),
                  const((1, B_WIDTH)), const((B_WIDTH, B_WIDTH)), const((1, B_WIDTH)),
                  const((1, B_WIDTH))],
        out_specs=pl.BlockSpec((tc, bsz * B_WIDTH), lambda i: (i, 0)),
        scratch_shapes=[pltpu.VMEM((B_WIDTH // LANES, rows, LANES), F32),
                        pltpu.VMEM((rows, 2 * S5_NSTATE), F32),
                        pltpu.VMEM((SUBLANES, 2 * S5_NSTATE), F32)],
        compiler_params=_cparams(("arbitrary",)),
        name="s5_scan",
    )(u_tm, bmat.astype(MXU_DTYPE), avec,
      cre.astype(MXU_DTYPE), cim.astype(MXU_DTYPE), d.astype(F32).reshape(1, B_WIDTH),
      glu_w.astype(MXU_DTYPE), glu_b.astype(F32).reshape(1, B_WIDTH), gain.reshape(1, B_WIDTH))
    return out


def _compress_kernel(k_ref, v_ref, pa_ref, pb_ref, wa_ref, wb_ref, w2_ref, cmp_ref, vT_ref):
    nseg = cmp_ref.shape[0]
    cw = k_ref.shape[1] + v_ref.shape[1]
    xa = jnp.zeros((nseg, cw), F32)
    xb = jnp.zeros((nseg, cw), F32)
    for l in range(CMP_STRIDE):
        rows = pl.ds(l, nseg, stride=CMP_STRIDE)
        tok = jnp.concatenate([k_ref[rows, :], v_ref[rows, :]], axis=1)
        cols = slice(l * cw, (l + 1) * cw)
        xa = xa + _dot(tok + pa_ref[:, cols], wa_ref[cols, :])
        xb = xb + _dot(tok + pb_ref[:, cols], wb_ref[cols, :])
    pre = xa + pltpu.roll(xb, nseg - 1, 0)
    out = _dot(jax.nn.gelu(pre), w2_ref[...])
    cmp_ref[...] = out.astype(cmp_ref.dtype)
    vT_ref[...] = out[:, LANES:].T.astype(vT_ref.dtype)


def _compress(kvc, pos_k, pos_v, k_w1, k_w2, v_w1, v_w2):
    bsz, seq, _ = kvc.shape
    nseg = seq // CMP_STRIDE
    width = CMP_STRIDE * 256
    eye = jnp.eye(4, dtype=F32)
    w1 = jnp.stack([k_w1, k_w1, v_w1, v_w1]).astype(F32).reshape(4, CMP_BLOCK, HEAD_DIM, HEAD_DIM)
    pos = jnp.stack([pos_k, pos_k, pos_v, pos_v]).astype(F32)

    def half(lo):
        w = jnp.einsum('slde,st->lsdte', w1[:, lo:lo + CMP_STRIDE], eye).reshape(width, 256)
        p = jnp.transpose(pos[:, lo:lo + CMP_STRIDE], (1, 0, 2)).reshape(1, width)
        return w.astype(MXU_DTYPE), p

    wa, pa = half(0)
    wb, pb = half(CMP_STRIDE)
    w2 = jnp.einsum('sde,st->sdte', jnp.stack([k_w2, k_w2, v_w2, v_w2]).astype(F32), eye).reshape(256, 256)
    const = lambda shape: pl.BlockSpec(shape, lambda b: (0,) * len(shape))
    return pl.pallas_call(
        _compress_kernel,
        out_shape=(jax.ShapeDtypeStruct((bsz, nseg, 256), MXU_DTYPE),
                   jax.ShapeDtypeStruct((bsz, 128, nseg), MXU_DTYPE)),
        grid=(bsz,),
        in_specs=[pl.BlockSpec((None, seq, LANES), lambda b: (b, 0, 0)),
                  pl.BlockSpec((None, seq, LANES), lambda b: (b, 0, 1)),
                  const((1, width)), const((1, width)), const((width, 256)), const((width, 256)),
                  const((256, 256))],
        out_specs=(pl.BlockSpec((None, nseg, 256), lambda b: (b, 0, 0)),
                   pl.BlockSpec((None, 128, nseg), lambda b: (b, 0, 0))),
        compiler_params=_cparams(("arbitrary",)),
        name="nsa_compress",
    )(kvc, kvc, pa, pb, wa, wb, w2.astype(MXU_DTYPE))


ONES_ROWS = 16
NQ4 = 4 * Q_BLOCK


def _with_ones(v_t):
    return jnp.concatenate([v_t, jnp.ones((ONES_ROWS, v_t.shape[1]), v_t.dtype)], axis=0)


def _online_softmax_step(s, bias, v_ext, m_ref, acc_ref):
    m_old = m_ref[...]
    m_parts, p_parts = [], []
    for h in range(NQ4 // Q_BLOCK):
        sl = slice(h * Q_BLOCK, (h + 1) * Q_BLOCK)
        s_h = s[:, sl] + bias[:, sl]
        m_h = jnp.maximum(m_old[:, sl], jnp.max(s_h, axis=0, keepdims=True))
        m_parts.append(m_h)
        p_parts.append(jnp.exp2(s_h - m_h).astype(MXU_DTYPE))
    m_new = jnp.concatenate(m_parts, axis=1)
    p = jnp.concatenate(p_parts, axis=1)
    acc_ref[...] = jnp.exp2(m_old - m_new) * acc_ref[...] + _dot(v_ext, p)
    m_ref[...] = m_new


def _tile4(x):
    return jnp.concatenate([x, x, x, x], axis=1)


INT_MIN = -2 ** 31
NEG_INF_KEY = int(np.int32(np.uint32(0xFF800000 ^ 0x7FFFFFFF)))


def _dsa_kernel(qT_ref, iqT_ref, miscT_ref, kv_ref, misc_ref, cT_ref, mq_ref, wuv_ref, gain_ref, tri_ref,
                o_ref, score_scr, m_scr, acc_scr, *, k_top, max_chunks):
    i = pl.program_id(1)
    t0 = i * Q_BLOCK
    kc = DSA_KEY_CHUNK
    nch = (t0 + Q_BLOCK + kc - 1) // kc
    t_lane = t0 + lax.broadcasted_iota(I32, (1, Q_BLOCK), 1)
    row_iota = lax.broadcasted_iota(I32, (kc, Q_BLOCK), 0)

    qcat = (_dot(mq_ref[...], qT_ref[...]) * (HEAD_DIM ** -0.5 * LOG2E)).astype(MXU_DTYPE)
    qs_t = jnp.concatenate([qcat[h * 256:(h + 1) * 256] for h in range(A_HEADS)], axis=1)
    iq_t = iqT_ref[...]
    iq_all = jnp.concatenate([iq_t[h * IDX_DIM:(h + 1) * IDX_DIM, :] for h in range(IDX_HEADS)], axis=1)
    w_t = miscT_ref[MISC_IW:MISC_IW + IDX_HEADS, :] * (IDX_HEADS ** -0.5 * IDX_DIM ** -0.5)

    def idx_body(c, diagonal):
        k0 = c * kc
        ik = misc_ref[pl.ds(k0, kc), MISC_IK:MISC_IK + IDX_DIM]
        d = _dot(ik, iq_all)
        score = jnp.zeros((kc, Q_BLOCK), F32)
        for h in range(IDX_HEADS):
            score = score + jnp.maximum(d[:, h * Q_BLOCK:(h + 1) * Q_BLOCK], 0.0) * w_t[h:h + 1, :]
        if diagonal:
            score = jnp.where(row_iota + k0 <= t_lane, score, -jnp.inf)
        score_scr[pl.ds(k0, kc), :] = score

    m_scr[...] = jnp.full_like(m_scr, NEG)
    acc_scr[...] = jnp.zeros_like(acc_scr)

    def passes(n_chunks):
        for c in range(n_chunks):
            idx_body(c, c == n_chunks - 1)

        def as_score(key):
            return pltpu.bitcast(key ^ ((key >> 31) & 0x7FFFFFFF), F32)

        def count(pred):
            acc = jnp.zeros((SUBLANES, Q_BLOCK), I32)
            for c in range(n_chunks):
                hit = pred(score_scr[c * kc:(c + 1) * kc, :])
                acc = acc + jnp.sum(hit.reshape(kc // SUBLANES, SUBLANES, Q_BLOCK), axis=0)
            return jnp.sum(acc, axis=0, keepdims=True)

        def thr_bit(b, prefix):
            cand = prefix | lax.shift_left(jnp.int32(1), 31 - b)
            cand_f = as_score(cand ^ INT_MIN)
            cnt = count(lambda sc: jnp.where(sc >= cand_f, 1, 0))
            return jnp.where(cnt >= k_top, cand, prefix)

        thr_key = jnp.maximum(lax.fori_loop(0, 32, thr_bit, jnp.zeros((1, Q_BLOCK), I32)) ^ INT_MIN,
                              NEG_INF_KEY)
        thr = as_score(thr_key)
        n_gt = count(lambda sc: jnp.where(sc > thr, 1, 0))
        need = jnp.where(thr_key == NEG_INF_KEY, 0, k_top - n_gt).astype(F32)

        n_tied = jnp.zeros((1, Q_BLOCK), F32)
        for c in range(n_chunks):
            keys = score_scr[c * kc:(c + 1) * kc, :]
            tied = _dot(tri_ref[...], jnp.where(keys == thr, 1.0, 0.0)) + n_tied
            bias = jnp.where(keys > thr, 0.0,
                             jnp.where(keys == thr, jnp.where(tied <= need, 0.0, NEG), NEG))
            v_t = jnp.concatenate([cT_ref[c * (kc // LANES) + j] for j in range(kc // LANES)], axis=1)
            s = _dot(kv_ref[c * kc:(c + 1) * kc, :], qs_t)
            _online_softmax_step(s, _tile4(bias), _with_ones(v_t), m_scr, acc_scr)
            n_tied = tied[kc - 1:kc, :]
        return 0

    lax.switch(nch - 1, [functools.partial(passes, n) for n in range(1, max_chunks + 1)])

    acc = acc_scr[...]
    o_lat = (acc[:A_KV_RANK] / acc[A_KV_RANK:A_KV_RANK + 1]).astype(MXU_DTYPE)
    out_t = jnp.concatenate(
        [_dot(wuv_ref[h], o_lat[:, h * Q_BLOCK:(h + 1) * Q_BLOCK]) for h in range(A_HEADS)], axis=0)
    out_t = _rms(out_t, 0) * gain_ref[...]
    o_ref[...] = out_t.T.astype(o_ref.dtype)


def _dsa(aq_t, iq_t, misc_t, akv, misc, ac_t, w_uk, w_uv, gain):
    bsz, _, seq = aq_t.shape
    nq = seq // Q_BLOCK
    k_top = min(DSA_TOPK, seq // 4)
    seq_pad = -(-seq // DSA_KEY_CHUNK) * DSA_KEY_CHUNK
    tri = jnp.asarray(np.tril(np.ones((DSA_KEY_CHUNK, DSA_KEY_CHUNK), np.float32)), MXU_DTYPE)
    mq = jnp.zeros((A_HEADS, 256, A_HEADS, HEAD_DIM), F32)
    for h in range(A_HEADS):
        mq = mq.at[h, :A_KV_RANK, h, ROPE_DIM:].set(w_uk[:, h, :].astype(F32))
        mq = mq.at[h, A_KV_RANK:A_KV_RANK + ROPE_DIM, h, :ROPE_DIM].set(jnp.eye(ROPE_DIM, dtype=F32))
    mq = mq.reshape(A_HEADS * 256, A_HEADS * HEAD_DIM).astype(MXU_DTYPE)
    wuv_t = jnp.transpose(w_uv, (1, 2, 0)).astype(MXU_DTYPE)
    per_q = lambda w: pl.BlockSpec((None, w, Q_BLOCK), lambda b, i: (b, 0, i))
    per_b = lambda *s: pl.BlockSpec((None,) + s, lambda b, i: (b,) + (0,) * len(s))
    const = lambda shape: pl.BlockSpec(shape, lambda b, i: (0,) * len(shape))
    return pl.pallas_call(
        functools.partial(_dsa_kernel, k_top=k_top, max_chunks=seq_pad // DSA_KEY_CHUNK),
        out_shape=jax.ShapeDtypeStruct((bsz, seq, A_HEADS * A_VDIM), MXU_DTYPE),
        grid=(bsz, nq),
        in_specs=[per_q(256), per_q(128), per_q(128), per_b(seq, 256), per_b(seq, 128),
                  per_b(seq // LANES, LANES, LANES), const(mq.shape), const(wuv_t.shape),
                  const((A_HEADS * A_VDIM, 1)), const(tri.shape)],
        out_specs=pl.BlockSpec((None, Q_BLOCK, A_HEADS * A_VDIM), lambda b, i: (b, i, 0)),
        scratch_shapes=[pltpu.VMEM((seq_pad, Q_BLOCK), F32),
                        pltpu.VMEM((1, NQ4), F32),
                        pltpu.VMEM((A_KV_RANK + ONES_ROWS, NQ4), F32)],
        compiler_params=_cparams(("arbitrary", "arbitrary")),
        name="dsa_attention",
    )(aq_t, iq_t, misc_t, akv, misc, ac_t, mq, wuv_t, gain.reshape(-1, 1), tri)


def _nsa_kernel(qT_ref, qrT_ref, miscT_ref, cmp_ref, vcT_ref, ks_ref, vsT_ref, kw_ref, vwT_ref,
                ov_ref, gain_ref, o_ref, sel_scr, m_scr, acc_scr, win_scr, *, n_top, n_blk, max_chunks):
    i = pl.program_id(1)
    t0 = i * Q_BLOCK
    kc = NSA_KEY_CHUNK
    nch = (t0 + Q_BLOCK + kc - 1) // kc
    t_lane = t0 + lax.broadcasted_iota(I32, (1, Q_BLOCK), 1)
    gates = jax.nn.sigmoid(miscT_ref[MISC_GATE:MISC_GATE + 3 * C_HEADS, :])
    n_cmp = cmp_ref.shape[0]
    cmp_iota = lax.broadcasted_iota(I32, (n_cmp, Q_BLOCK), 0)
    blk_iota = lax.broadcasted_iota(I32, (n_blk, Q_BLOCK), 0)
    row_iota = lax.broadcasted_iota(I32, (kc, Q_BLOCK), 0)
    win_iota = lax.broadcasted_iota(I32, (Q_BLOCK, Q_BLOCK), 0)
    groups = range(C_KV_HEADS)
    gsl = [slice(g * HEAD_DIM, (g + 1) * HEAD_DIM) for g in groups]

    def heads_t(ref, g):
        return jnp.concatenate([ref[h * HEAD_DIM:(h + 1) * HEAD_DIM, :]
                                for h in range(g * C_GROUP, (g + 1) * C_GROUP)], axis=1)

    o_c = []
    for g in groups:
        s_c = _dot(cmp_ref[:, gsl[g]], heads_t(qT_ref, g))
        vis = _tile4(jnp.where(cmp_iota * CMP_STRIDE + (CMP_BLOCK - 1) <= t_lane, 1, 0)) > 0
        s_c = jnp.where(vis, s_c, NEG)
        p_c = jnp.where(vis, jnp.exp2(s_c - jnp.max(s_c, axis=0, keepdims=True)), 0.0)
        l_c = jnp.sum(p_c, axis=0, keepdims=True)
        p_c = p_c * (1.0 / jnp.maximum(l_c, 1e-30))
        o_c.append(_dot(vcT_ref[gsl[g], :], p_c))

        p_sum = (p_c[:, 0:Q_BLOCK] + p_c[:, Q_BLOCK:2 * Q_BLOCK]
                 + p_c[:, 2 * Q_BLOCK:3 * Q_BLOCK] + p_c[:, 3 * Q_BLOCK:4 * Q_BLOCK])
        p_hi = p_sum.astype(MXU_DTYPE)
        p_lo = p_sum - p_hi.astype(F32)
        imp = _dot(ov_ref[...], p_hi) + _dot(ov_ref[...], p_lo)
        cur = t_lane // SEL_BLOCK
        forced = jnp.where(blk_iota == 0, 1, jnp.where(blk_iota == cur, 1,
                           jnp.where(blk_iota == cur - 1, 1, 0)))
        imp = jnp.where(forced > 0, SEL_FORCE, imp)
        imp = jnp.where(blk_iota * SEL_BLOCK <= t_lane, imp, -jnp.inf)
        n_grp = n_blk // SUBLANES
        imp_g = [imp[v * SUBLANES:(v + 1) * SUBLANES, :] for v in range(n_grp)]
        rank_g = [jnp.zeros((SUBLANES, Q_BLOCK), I32) for _ in range(n_grp)]
        for mp in range(n_blk):
            row = imp[mp:mp + 1, :]
            for v in range(n_grp):
                if v > mp // SUBLANES:
                    beats = jnp.where(row >= imp_g[v], 1, 0)
                elif v < mp // SUBLANES:
                    beats = jnp.where(row > imp_g[v], 1, 0)
                else:
                    later = blk_iota[v * SUBLANES:(v + 1) * SUBLANES, :] > mp
                    beats = jnp.where(row > imp_g[v], 1, jnp.where(row == imp_g[v], jnp.where(later, 1, 0), 0))
                rank_g[v] = rank_g[v] + beats
        rank = jnp.concatenate(rank_g, axis=0)
        sel_scr[g, 0:n_blk, :] = jnp.where(rank < n_top, 0.0, NEG)

    m_scr[...] = jnp.full_like(m_scr, NEG)
    acc_scr[...] = jnp.zeros_like(acc_scr)
    bpc = kc // SEL_BLOCK

    def sel_body(c, diagonal):
        k0 = c * kc
        for g in groups:
            s = _dot(ks_ref[pl.ds(k0, kc), gsl[g]], heads_t(qrT_ref, g))
            sel8 = sel_scr[g, pl.ds(c * bpc, bpc), :]
            bias = jnp.concatenate(
                [jnp.broadcast_to(sel8[j:j + 1, :], (SEL_BLOCK, Q_BLOCK)) for j in range(bpc)], axis=0)
            if diagonal:
                bias = jnp.where(row_iota + k0 <= t_lane, bias, NEG)
            v_t = jnp.concatenate([vsT_ref[c * (kc // LANES) + j][gsl[g], :]
                                   for j in range(kc // LANES)], axis=1)
            _online_softmax_step(s, _tile4(bias), _with_ones(v_t), m_scr.at[g], acc_scr.at[g])

    def window_branch(g):
        k_parts, v_parts, m_parts = [], [], []
        for j in range(WINDOW // Q_BLOCK + 1):
            cj = i - WINDOW // Q_BLOCK + j
            cjc = jnp.maximum(cj, 0)
            k0 = pl.multiple_of(cjc * Q_BLOCK, Q_BLOCK)
            k_parts.append(kw_ref[pl.ds(k0, Q_BLOCK), gsl[g]])
            v_parts.append(vwT_ref[cjc][gsl[g], :])
            kidx = win_iota + k0
            inside = jnp.where(kidx <= t_lane, jnp.where(kidx > t_lane - WINDOW, 0.0, NEG), NEG)
            m_parts.append(jnp.where(cj >= 0, inside, NEG))
        s_w = (_dot(jnp.concatenate(k_parts, axis=0), heads_t(qrT_ref, g))
               + _tile4(jnp.concatenate(m_parts, axis=0)))
        p_w = jnp.exp2(s_w - jnp.max(s_w, axis=0, keepdims=True))
        acc = _dot(_with_ones(jnp.concatenate(v_parts, axis=1)), p_w)
        win_scr[g] = acc[:HEAD_DIM] / acc[HEAD_DIM:HEAD_DIM + 1]

    def sweep(n_chunks):
        for g in groups:
            window_branch(g)
        for c in range(n_chunks):
            sel_body(c, c == n_chunks - 1)
        return 0

    lax.switch(nch - 1, [functools.partial(sweep, n) for n in range(1, max_chunks + 1)])

    slabs = []
    for g in groups:
        acc = acc_scr[g]
        o_s = acc[:HEAD_DIM] / acc[HEAD_DIM:HEAD_DIM + 1]
        o_w = win_scr[g]
        for hh in range(C_GROUP):
            sl = slice(hh * Q_BLOCK, (hh + 1) * Q_BLOCK)
            r = (g * C_GROUP + hh) * 3
            slabs.append(gates[r:r + 1, :] * o_c[g][:, sl] + gates[r + 1:r + 2, :] * o_s[:, sl]
                         + gates[r + 2:r + 3, :] * o_w[:, sl])

    out_t = jnp.concatenate(slabs, axis=0)
    out_t = _rms(out_t, 0) * gain_ref[...]
    o_ref[...] = out_t.T.astype(o_ref.dtype)


def _sel_overlap_t(n_cmp_rows, n_blk):
    cs = np.arange(n_cmp_rows)[None, :] * CMP_STRIDE
    ss = np.arange(n_blk)[:, None] * SEL_BLOCK
    ov = np.minimum(cs + CMP_BLOCK, ss + SEL_BLOCK) - np.maximum(cs, ss)
    return np.clip(ov, 0, None).astype(np.float32) / CMP_BLOCK


def _nsa(cq_t, cqr_t, misc_t, cmp, vcmp_t, ks, vs_t, kw, vw_t, gain):
    bsz, _, seq = cq_t.shape
    nq = seq // Q_BLOCK
    nseg = cmp.shape[1]
    n_blk = seq // SEL_BLOCK
    n_top = min(SEL_TOPN, n_blk)
    seq_pad = -(-seq // NSA_KEY_CHUNK) * NSA_KEY_CHUNK
    ov_np = _sel_overlap_t(nseg, n_blk)
    ov_np[:, (seq - CMP_BLOCK) // CMP_STRIDE + 1:] = 0.0
    ov = jnp.asarray(ov_np, MXU_DTYPE)
    per_q = lambda w: pl.BlockSpec((None, w, Q_BLOCK), lambda b, i: (b, 0, i))
    per_b = lambda *s: pl.BlockSpec((None,) + s, lambda b, i: (b,) + (0,) * len(s))
    const = lambda shape: pl.BlockSpec(shape, lambda b, i: (0,) * len(shape))
    return pl.pallas_call(
        functools.partial(_nsa_kernel, n_top=n_top, n_blk=n_blk, max_chunks=seq_pad // NSA_KEY_CHUNK),
        out_shape=jax.ShapeDtypeStruct((bsz, seq, C_HEADS * HEAD_DIM), MXU_DTYPE),
        grid=(bsz, nq),
        in_specs=[per_q(512), per_q(512), per_q(128), per_b(nseg, 256), per_b(128, nseg),
                  per_b(seq, 128), per_b(seq // LANES, LANES, LANES),
                  per_b(seq, 128), per_b(seq // LANES, LANES, LANES),
                  const(ov.shape), const((C_HEADS * HEAD_DIM, 1))],
        out_specs=pl.BlockSpec((None, Q_BLOCK, C_HEADS * HEAD_DIM), lambda b, i: (b, i, 0)),
        scratch_shapes=[pltpu.VMEM((C_KV_HEADS, max(n_blk, seq_pad // SEL_BLOCK), Q_BLOCK), F32),
                        pltpu.VMEM((C_KV_HEADS, 1, NQ4), F32),
                        pltpu.VMEM((C_KV_HEADS, HEAD_DIM + ONES_ROWS, NQ4), F32),
                        pltpu.VMEM((C_KV_HEADS, HEAD_DIM, NQ4), F32)],
        compiler_params=_cparams(("arbitrary", "arbitrary")),
        name="nsa_attention",
    )(cq_t, cqr_t, misc_t, cmp, vcmp_t, ks, vs_t, kw, vw_t, ov, gain.reshape(-1, 1))


def _out_mlp_kernel(x_ref, a_ref, b_ref, c_ref, wo_ref, g2_ref, wu_ref, wd_ref, fg_ref,
                    o_ref, x1_scr, n2_scr, acc_scr, *, final_norm):
    j = pl.program_id(1)

    @pl.when(j == 0)
    def _():
        mixed = (jnp.dot(a_ref[...], wo_ref[0:256, :], preferred_element_type=F32)
                 + jnp.dot(b_ref[...], wo_ref[256:512, :], preferred_element_type=F32)
                 + jnp.dot(c_ref[...], wo_ref[512:1024, :], preferred_element_type=F32))
        x1 = x_ref[...] + mixed
        x1_scr[...] = x1
        n2_scr[...] = (_rms(x1, -1) * g2_ref[...]).astype(n2_scr.dtype)
        acc_scr[...] = jnp.zeros_like(acc_scr)

    z = jnp.dot(n2_scr[...], wu_ref[...], preferred_element_type=F32)
    acc_scr[...] += _dot(jnp.square(jnp.maximum(z, 0.0)), wd_ref[...])

    @pl.when(j == pl.num_programs(1) - 1)
    def _():
        x2 = x1_scr[...] + acc_scr[...]
        if final_norm:
            x2 = _rms(x2, -1) * fg_ref[...]
        o_ref[...] = x2


def _out_mlp(x, a_n, b_tm, c_n, w_out, ln2_g, w_up, w_down, final_g, final_norm):
    bsz, seq, _ = x.shape
    tm = min(MLP_TM, seq)
    nj = seq // tm
    nf = D_FF // MLP_TF
    rows = lambda w: pl.BlockSpec((None, tm, w), lambda r, f: (r // nj, r % nj, 0))
    const = lambda shape: pl.BlockSpec(shape, lambda r, f: (0,) * len(shape))
    return pl.pallas_call(
        functools.partial(_out_mlp_kernel, final_norm=final_norm),
        out_shape=jax.ShapeDtypeStruct(x.shape, F32),
        grid=(bsz * nj, nf),
        in_specs=[rows(D_MODEL), rows(256),
                  pl.BlockSpec((tm, B_WIDTH), lambda r, f: (r % nj, r // nj)),
                  rows(512), const((D_MODEL, D_MODEL)), const((1, D_MODEL)),
                  pl.BlockSpec((D_MODEL, MLP_TF), lambda r, f: (0, f)),
                  pl.BlockSpec((MLP_TF, D_MODEL), lambda r, f: (f, 0)),
                  const((1, D_MODEL))],
        out_specs=rows(D_MODEL),
        scratch_shapes=[pltpu.VMEM((tm, D_MODEL), F32), pltpu.VMEM((tm, D_MODEL), MXU_DTYPE),
                        pltpu.VMEM((tm, D_MODEL), F32)],
        compiler_params=_cparams(("arbitrary", "arbitrary")),
        name="out_proj_mlp",
    )(x, a_n, b_tm, c_n, w_out.astype(MXU_DTYPE), ln2_g.reshape(1, -1), w_up.astype(MXU_DTYPE),
      w_down.astype(MXU_DTYPE), final_g.reshape(1, -1))


def kernel(x, positions, ln1_g, w_in, kv_norm_g, w_uk, w_uv, s5_lambda_re, s5_lambda_im, s5_log_step, s5_b_re, s5_b_im, s5_c_re, s5_c_im, s5_d, s5_glu_w, s5_glu_b, cmp_pos_k, cmp_pos_v, cmp_k_w1, cmp_k_w2, cmp_v_w1, cmp_v_w2, gain_a, gain_b, gain_c, w_out, ln2_g, w_up, w_down, final_g):
    bsz, seq, _ = x.shape
    depth = w_in.shape[0]
    src, pat_np = _proj_layout()
    pat = jnp.asarray(pat_np, jnp.bfloat16)
    tab = _rope_table(positions)

    def regather(w, idx):
        cols = jnp.take(w, jnp.asarray(np.maximum(idx, 0)), axis=1)
        return jnp.where(jnp.asarray(idx >= 0)[None, :], cols, 0.0).astype(MXU_DTYPE)

    for layer in range(depth):
        w_r = regather(w_in[layer], src)
        (aq_t, akv, ac_t, iq_t, misc, misc_t, u_tm, cq_t, cqr_t, kvc, ks, vs_t, kw, vw_t) = _in_proj(
            x, tab, ln1_g[layer], kv_norm_g[layer], w_r, pat)

        ar, ai, bbr, bbi = _s5_discretize(s5_lambda_re[layer], s5_lambda_im[layer], s5_log_step[layer],
                                          s5_b_re[layer], s5_b_im[layer])
        b_tm = _s5(u_tm, bsz, ar, ai, bbr, bbi, s5_c_re[layer], s5_c_im[layer], s5_d[layer].reshape(-1),
                   s5_glu_w[layer], s5_glu_b[layer], gain_b[layer])

        cmp, vcmp_t = _compress(kvc, cmp_pos_k[layer], cmp_pos_v[layer], cmp_k_w1[layer],
                                cmp_k_w2[layer], cmp_v_w1[layer], cmp_v_w2[layer])
        a_n = _dsa(aq_t, iq_t, misc_t, akv, misc, ac_t, w_uk[layer], w_uv[layer], gain_a[layer])
        c_n = _nsa(cq_t, cqr_t, misc_t, cmp, vcmp_t, ks, vs_t, kw, vw_t, gain_c[layer])

        x = _out_mlp(x, a_n, b_tm, c_n, w_out[layer], ln2_g[layer], w_up[layer], w_down[layer],
                     final_g, final_norm=(layer == depth - 1))
    return x
```
